```python
import math
import jax, jax.numpy as jnp
from jax import lax
import numpy as np

D_MODEL = 1024
BATCH = 2
SEQ = 8192
DEPTH = 4

HEAD_DIM = 64
A_HEADS = D_MODEL // (4 * HEAD_DIM)
A_VDIM = 2 * HEAD_DIM
B_HEADS = D_MODEL // (2 * HEAD_DIM)
C_HEADS = D_MODEL // HEAD_DIM
Q_BLOCK = 128
N_GROUPS = 4
EXPERTS_PER_GROUP = 4
N_EXPERTS = N_GROUPS * EXPERTS_PER_GROUP
TOP_K = 2
D_EXPERT = D_MODEL
MOE_BLOCK = 128
EPS = 1e-6
N_EVEN = (DEPTH + 1) // 2
N_ODD = DEPTH // 2

kernel_name = 'hybrid_diff_stickbreak_fox_hmoe'


def rms_norm(x, g):
    xf = x.astype(jnp.float32)
    y = xf * lax.rsqrt(jnp.mean(xf * xf, axis=-1, keepdims=True) + EPS)
    return (y * g.astype(jnp.float32)).astype(x.dtype)


def to_heads(x, n, d):
    b, s, _ = x.shape
    return x.reshape(b, s, n, d).transpose(0, 2, 1, 3)


def from_heads(x):
    b, h, s, d = x.shape
    return x.transpose(0, 2, 1, 3).reshape(b, s, h * d)


def to_blocks(x):
    b, h, s = x.shape[:3]
    y = x.reshape((b, h, s // Q_BLOCK, Q_BLOCK) + x.shape[3:])
    return jnp.moveaxis(y, 2, 0)


def from_blocks(y):
    y = jnp.moveaxis(y, 0, 2)
    b, h, nb, q = y.shape[:4]
    return y.reshape((b, h, nb * q) + y.shape[4:])


def alibi_slopes(n):
    return np.array([2.0 ** (-8.0 * (i + 1) / n) for i in range(n)], dtype=np.float32)


def diff_attention(q1, q2, k1, k2, v, lam):
    s_len = q1.shape[2]
    scale = HEAD_DIM ** -0.5
    kpos = jnp.arange(s_len)
    slopes = jnp.asarray(alibi_slopes(A_HEADS))

    def block(args):
        qb1, qb2, i = args
        tpos = i * Q_BLOCK + jnp.arange(Q_BLOCK)
        dist = (tpos[:, None] - kpos[None, :]).astype(jnp.float32)
        causal = dist >= 0
        bias = -slopes[:, None, None] * dist

        def probs(qb, k):
            sc = jnp.einsum('bhqd,bhkd->bhqk', qb, k).astype(jnp.float32) * scale + bias
            return jax.nn.softmax(jnp.where(causal, sc, -jnp.inf), axis=-1)

        p = probs(qb1, k1) - lam * probs(qb2, k2)
        return jnp.einsum('bhqk,bhkd->bhqd', p.astype(v.dtype), v)

    nb = s_len // Q_BLOCK
    out = lax.map(block, (to_blocks(q1), to_blocks(q2), jnp.arange(nb)))
    return from_blocks(out)


def stick_breaking_attention(q, k, v):
    s_len = q.shape[2]
    scale = HEAD_DIM ** -0.5
    kpos = jnp.arange(s_len)

    def block(args):
        qb, i = args
        tpos = i * Q_BLOCK + jnp.arange(Q_BLOCK)
        before = tpos[:, None] > kpos[None, :]
        z = jnp.einsum('bhqd,bhkd->bhqk', qb, k).astype(jnp.float32) * scale
        log_beta = jax.nn.log_sigmoid(z)
        log_keep = jnp.where(before, jax.nn.log_sigmoid(-z), 0.0)
        acc = lax.cumsum(log_keep, axis=3, reverse=True) - log_keep
        w = jnp.where(before, jnp.exp(log_beta + acc), 0.0)
        return jnp.einsum('bhqk,bhkd->bhqd', w.astype(v.dtype), v)

    nb = s_len // Q_BLOCK
    out = lax.map(block, (to_blocks(q), jnp.arange(nb)))
    return from_blocks(out)


def forgetting_attention(q, k, v, cum_logf):
    s_len = q.shape[2]
    scale = HEAD_DIM ** -0.5
    kpos = jnp.arange(s_len)

    def block(args):
        qb, fb, i = args
        tpos = i * Q_BLOCK + jnp.arange(Q_BLOCK)
        causal = tpos[:, None] >= kpos[None, :]
        sc = jnp.einsum('bhqd,bhkd->bhqk', qb, k).astype(jnp.float32) * scale
        sc = sc + (fb[..., :, None] - cum_logf[..., None, :])
        p = jax.nn.softmax(jnp.where(causal, sc, -jnp.inf), axis=-1)
        return jnp.einsum('bhqk,bhkd->bhqd', p.astype(v.dtype), v)

    nb = s_len // Q_BLOCK
    out = lax.map(block, (to_blocks(q), to_blocks(cum_logf), jnp.arange(nb)))
    return from_blocks(out)


def even_mixer(h, w_in, q_gain, k_gain, lam_q1, lam_k1, lam_q2, lam_k2, subln_g, w_out, layer):
    b, s, _ = h.shape
    qa = A_HEADS * 2 * HEAD_DIM
    va = A_HEADS * A_VDIM
    qb = B_HEADS * HEAD_DIM
    cuts = list(np.cumsum([qa, qa, va, qb, qb]))
    proj = h @ w_in
    a_q, a_k, a_v, b_q, b_k, b_v = jnp.split(proj, cuts, axis=-1)
    a_q = rms_norm(a_q.reshape(b, s, A_HEADS, 2, HEAD_DIM), q_gain)
    a_k = rms_norm(a_k.reshape(b, s, A_HEADS, 2, HEAD_DIM), k_gain)
    q1 = a_q[..., 0, :].transpose(0, 2, 1, 3)
    q2 = a_q[..., 1, :].transpose(0, 2, 1, 3)
    k1 = a_k[..., 0, :].transpose(0, 2, 1, 3)
    k2 = a_k[..., 1, :].transpose(0, 2, 1, 3)
    lam_init = 0.8 - 0.6 * math.exp(-0.3 * layer)
    f32 = jnp.float32
    lam = (jnp.exp(jnp.sum(lam_q1.astype(f32) * lam_k1.astype(f32)))
           - jnp.exp(jnp.sum(lam_q2.astype(f32) * lam_k2.astype(f32))) + lam_init)
    o_a = diff_attention(q1, q2, k1, k2, to_heads(a_v, A_HEADS, A_VDIM), lam)
    o_a = rms_norm(o_a, subln_g) * (1.0 - lam_init)
    o_b = stick_breaking_attention(to_heads(b_q, B_HEADS, HEAD_DIM),
                                   to_heads(b_k, B_HEADS, HEAD_DIM),
                                   to_heads(b_v, B_HEADS, HEAD_DIM))
    o = jnp.concatenate([from_heads(o_a), from_heads(o_b)], axis=-1)
    return o @ w_out


def odd_mixer(h, w_in, b_f, q_gain, k_gain, w_out):
    b, s, _ = h.shape
    w = C_HEADS * HEAD_DIM
    proj = h @ w_in
    q, k, v, f = jnp.split(proj, [w, 2 * w, 3 * w], axis=-1)
    q = rms_norm(q.reshape(b, s, C_HEADS, HEAD_DIM), q_gain).transpose(0, 2, 1, 3)
    k = rms_norm(k.reshape(b, s, C_HEADS, HEAD_DIM), k_gain).transpose(0, 2, 1, 3)
    v = to_heads(v, C_HEADS, HEAD_DIM)
    log_f = jax.nn.log_sigmoid(f.astype(jnp.float32) + b_f.astype(jnp.float32))
    cum = jnp.cumsum(log_f, axis=1).transpose(0, 2, 1)
    o = forgetting_attention(q, k, v, cum)
    return from_heads(o) @ w_out


def hier_moe(h, w_group, b_group, w_router, b_router, w_gate, w_up, w_down):
    b, s, d = h.shape
    xt = h.reshape(-1, d)
    n = xt.shape[0]
    f32 = jnp.float32
    g_logits = (xt @ w_group).astype(f32) + b_group.astype(f32)
    g_probs = jax.nn.softmax(g_logits, axis=-1)
    g_idx = jnp.argmax(g_logits, axis=-1).astype(jnp.int32)
    g_w = jnp.take_along_axis(g_probs, g_idx[:, None], axis=1)
    e_logits = ((xt @ w_router).astype(f32) + b_router.astype(f32)).reshape(n, N_GROUPS, EXPERTS_PER_GROUP)
    e_in = jnp.take_along_axis(e_logits, g_idx[:, None, None], axis=1)[:, 0]
    top_v, top_i = lax.top_k(e_in, TOP_K)
    top_w = jax.nn.softmax(top_v, axis=-1) * g_w
    expert = g_idx[:, None] * EXPERTS_PER_GROUP + top_i.astype(jnp.int32)
    nk = n * TOP_K
    e_flat = expert.reshape(-1)
    w_flat = top_w.reshape(-1)
    tok_flat = jnp.arange(nk, dtype=jnp.int32) // TOP_K
    order = jnp.argsort(e_flat)
    e_sorted = e_flat[order]
    counts = jnp.bincount(e_flat, length=N_EXPERTS)
    starts = jnp.cumsum(counts) - counts
    padded = (counts + MOE_BLOCK - 1) // MOE_BLOCK * MOE_BLOCK
    pad_ends = jnp.cumsum(padded)
    pad_starts = pad_ends - padded
    dest = pad_starts[e_sorted] + jnp.arange(nk) - starts[e_sorted]
    n_blocks = -(-nk // MOE_BLOCK) + N_EXPERTS
    p = n_blocks * MOE_BLOCK
    tok_buf = jnp.zeros((p,), jnp.int32).at[dest].set(tok_flat[order])
    w_buf = jnp.zeros((p,), f32).at[dest].set(w_flat[order])
    x_buf = xt[tok_buf].reshape(n_blocks, MOE_BLOCK, d)
    blk_e = jnp.searchsorted(pad_ends, jnp.arange(n_blocks) * MOE_BLOCK, side='right')
    blk_e = jnp.minimum(blk_e, N_EXPERTS - 1)

    def expert_block(args):
        xb, e = args
        return (jax.nn.silu(xb @ w_gate[e]) * (xb @ w_up[e])) @ w_down[e]

    y_buf = lax.map(expert_block, (x_buf, blk_e)).reshape(p, d)
    y_buf = y_buf * w_buf[:, None].astype(y_buf.dtype)
    out = jax.ops.segment_sum(y_buf, tok_buf, num_segments=n)
    return out.reshape(b, s, d).astype(h.dtype)


def setup_inputs(seed: int = 0) -> dict:
    key = jax.random.key(seed)
    ks = iter(jax.random.split(key, 32))
    nrm = lambda shape, scale: jax.random.normal(next(ks), shape, jnp.float32) * scale
    gain = lambda shape: 1.0 + nrm(shape, 0.02)
    d = D_MODEL
    return {
        'x': nrm((BATCH, SEQ, d), 1.0),
        'norm_mix_g': gain((DEPTH, d)),
        'norm_ffn_g': gain((DEPTH, d)),
        'ab_w_in': nrm((N_EVEN, d, 3 * d), d ** -0.5),
        'a_q_gain': gain((N_EVEN, HEAD_DIM)),
        'a_k_gain': gain((N_EVEN, HEAD_DIM)),
        'a_lam_q1': nrm((N_EVEN, HEAD_DIM), 0.1),
        'a_lam_k1': nrm((N_EVEN, HEAD_DIM), 0.1),
        'a_lam_q2': nrm((N_EVEN, HEAD_DIM), 0.1),
        'a_lam_k2': nrm((N_EVEN, HEAD_DIM), 0.1),
        'a_subln_g': gain((N_EVEN, A_VDIM)),
        'ab_w_out': nrm((N_EVEN, d, d), d ** -0.5),
        'c_w_in': nrm((N_ODD, d, 3 * d + C_HEADS), d ** -0.5),
        'c_b_f': jax.random.uniform(next(ks), (N_ODD, C_HEADS), jnp.float32, 2.0, 5.0),
        'c_q_gain': gain((N_ODD, HEAD_DIM)),
        'c_k_gain': gain((N_ODD, HEAD_DIM)),
        'c_w_out': nrm((N_ODD, d, d), d ** -0.5),
        'moe_w_group': nrm((DEPTH, d, N_GROUPS), d ** -0.5),
        'moe_b_group': nrm((DEPTH, N_GROUPS), 0.01),
        'moe_w_router': nrm((DEPTH, d, N_EXPERTS), d ** -0.5),
        'moe_b_router': nrm((DEPTH, N_EXPERTS), 0.01),
        'moe_w_gate': nrm((DEPTH, N_EXPERTS, d, D_EXPERT), d ** -0.5),
        'moe_w_up': nrm((DEPTH, N_EXPERTS, d, D_EXPERT), d ** -0.5),
        'moe_w_down': nrm((DEPTH, N_EXPERTS, D_EXPERT, d), D_EXPERT ** -0.5),
    }


def reference(x, norm_mix_g, norm_ffn_g, ab_w_in, a_q_gain, a_k_gain, a_lam_q1, a_lam_k1,
              a_lam_q2, a_lam_k2, a_subln_g, ab_w_out, c_w_in, c_b_f, c_q_gain, c_k_gain,
              c_w_out, moe_w_group, moe_b_group, moe_w_router, moe_b_router, moe_w_gate,
              moe_w_up, moe_w_down):
    for layer in range(DEPTH):
        i = layer // 2
        h = rms_norm(x, norm_mix_g[layer])
        if layer % 2 == 0:
            mix = even_mixer(h, ab_w_in[i], a_q_gain[i], a_k_gain[i], a_lam_q1[i], a_lam_k1[i],
                             a_lam_q2[i], a_lam_k2[i], a_subln_g[i], ab_w_out[i], layer)
        else:
            mix = odd_mixer(h, c_w_in[i], c_b_f[i], c_q_gain[i], c_k_gain[i], c_w_out[i])
        x = x + mix
        h = rms_norm(x, norm_ffn_g[layer])
        x = x + hier_moe(h, moe_w_group[layer], moe_b_group[layer], moe_w_router[layer],
                         moe_b_router[layer], moe_w_gate[layer], moe_w_up[layer], moe_w_down[layer])
    return x
```

```python
import functools
import math

import jax
import jax.numpy as jnp
from jax import lax
from jax.experimental import pallas as pl
from jax.experimental.pallas import tpu as pltpu

F32 = jnp.float32
BF16 = jnp.bfloat16

HEAD_DIM = 64
LANES = 128
EPS = 1e-6
SCALE = HEAD_DIM ** -0.5
N_GROUPS = 4
EXPERTS_PER_GROUP = 4
N_EXPERTS = N_GROUPS * EXPERTS_PER_GROUP
A_HEADS = 4
NEG = -1e30

ROW_TILE = 512
ATT_TILE = 512
SB_TILE = 256
MOE_TILE = 256
CMB_TILE = 256
SEC = 512
VMEM_LIMIT = 48 * 1024 * 1024


def _cparams(sem):
    return pltpu.CompilerParams(dimension_semantics=sem, vmem_limit_bytes=VMEM_LIMIT)


def _dot(a, b):
    return jnp.dot(a, b, preferred_element_type=F32)


def _dot_nt(a, b):
    return lax.dot_general(a, b, (((1,), (1,)), ((), ())), preferred_element_type=F32)


def _split2(x):
    hi = x.astype(BF16)
    lo = (x - hi.astype(F32)).astype(BF16)
    return hi, lo


def _rms_rows(x, g):
    return x * lax.rsqrt(jnp.mean(x * x, axis=-1, keepdims=True) + EPS) * g


def _group_rsqrt(acc, e_ref, et_ref):
    hi, lo = _split2(acc * acc)
    ss = _dot(hi, e_ref[...]) + _dot(lo, e_ref[...])
    rhi, rlo = _split2(lax.rsqrt(ss * (1.0 / HEAD_DIM) + EPS))
    return _dot(rhi, et_ref[...]) + _dot(rlo, et_ref[...])


def _softplus(z):
    return jnp.maximum(z, 0.0) + jnp.log(1.0 + jnp.exp(-jnp.abs(z)))


def _even_in_kernel(x_ref, g_ref, w_ref, gq_ref, gk_ref, e_ref, et_ref, o_ref):
    xn = _rms_rows(x_ref[...], g_ref[...]).astype(BF16)
    for sec in range(6):
        cols = slice(sec * SEC, (sec + 1) * SEC)
        acc = _dot(xn, w_ref[:, cols])
        if sec == 0:
            acc = acc * _group_rsqrt(acc, e_ref, et_ref) * gq_ref[...]
        elif sec == 1:
            acc = acc * _group_rsqrt(acc, e_ref, et_ref) * gk_ref[...]
        elif sec == 3:
            acc = acc * SCALE
        o_ref[:, cols] = acc.astype(BF16)


def _odd_in_kernel(x_ref, g_ref, w_ref, wf_ref, bf_ref, gq_ref, gk_ref, e_ref, et_ref, tri_ref,
                   o_ref, cum_ref, carry_ref, *, tiles_per_seq):
    i = pl.program_id(0)
    xn = _rms_rows(x_ref[...], g_ref[...]).astype(BF16)
    for sec in range(6):
        cols = slice(sec * SEC, (sec + 1) * SEC)
        acc = _dot(xn, w_ref[:, cols])
        if sec < 2:
            acc = acc * _group_rsqrt(acc, e_ref, et_ref) * gq_ref[...]
        elif sec < 4:
            acc = acc * _group_rsqrt(acc, e_ref, et_ref) * gk_ref[...]
        o_ref[:, cols] = acc.astype(BF16)

    log_f = -_softplus(-(_dot(xn, wf_ref[...]) + bf_ref[...]))
    p1 = log_f.astype(BF16)
    r1 = log_f - p1.astype(F32)
    p2 = r1.astype(BF16)
    p3 = (r1 - p2.astype(F32)).astype(BF16)
    tri = tri_ref[...]
    within = _dot(tri, p1) + _dot(tri, p2) + _dot(tri, p3)

    @pl.when(i % tiles_per_seq == 0)
    def _():
        carry_ref[...] = jnp.zeros_like(carry_ref)

    cum = within + carry_ref[...]
    cum_ref[...] = cum
    carry_ref[...] = cum[-1:, :]


def _group_maps():
    lane_grp = jnp.arange(SEC)[:, None] // HEAD_DIM
    e = (lane_grp == jnp.arange(LANES)[None, :]).astype(BF16)
    return e, e.T


def _even_in_proj(x, g, w_in, q_gain, k_gain):
    n, d = x.shape
    e, et = _group_maps()
    reps = SEC // HEAD_DIM
    gq = jnp.tile(q_gain.astype(F32) * SCALE, reps).reshape(1, SEC)
    gk = jnp.tile(k_gain.astype(F32), reps).reshape(1, SEC)
    const = lambda shape: pl.BlockSpec(shape, lambda i: (0, 0))
    return pl.pallas_call(
        _even_in_kernel,
        grid=(n // ROW_TILE,),
        in_specs=[pl.BlockSpec((ROW_TILE, d), lambda i: (i, 0)), const((1, d)), const((d, 3 * d)),
                  const((1, SEC)), const((1, SEC)), const((SEC, LANES)), const((LANES, SEC))],
        out_specs=pl.BlockSpec((ROW_TILE, 3 * d), lambda i: (i, 0)),
        out_shape=jax.ShapeDtypeStruct((n, 3 * d), BF16),
        compiler_params=_cparams(("arbitrary",)),
        name="even_in_proj",
    )(x, g.reshape(1, d), w_in.astype(BF16), gq, gk, e, et)


def _odd_in_proj(x, g, w_in, b_f, q_gain, k_gain, seq):
    n, d = x.shape
    heads = d // HEAD_DIM
    e, et = _group_maps()
    reps = SEC // HEAD_DIM
    gq = jnp.tile(q_gain.astype(F32) * SCALE, reps).reshape(1, SEC)
    gk = jnp.tile(k_gain.astype(F32), reps).reshape(1, SEC)
    wf = jnp.zeros((d, LANES), F32).at[:, :heads].set(w_in[:, 3 * d:]).astype(BF16)
    bf = jnp.zeros((1, LANES), F32).at[0, :heads].set(b_f.astype(F32))
    tri = (jnp.arange(ROW_TILE)[:, None] >= jnp.arange(ROW_TILE)[None, :]).astype(BF16)
    const = lambda shape: pl.BlockSpec(shape, lambda i: (0, 0))
    return pl.pallas_call(
        functools.partial(_odd_in_kernel, tiles_per_seq=seq // ROW_TILE),
        grid=(n // ROW_TILE,),
        in_specs=[pl.BlockSpec((ROW_TILE, d), lambda i: (i, 0)), const((1, d)), const((d, 3 * d)),
                  const((d, LANES)), const((1, LANES)), const((1, SEC)), const((1, SEC)),
                  const((SEC, LANES)), const((LANES, SEC)), const((ROW_TILE, ROW_TILE))],
        out_specs=[pl.BlockSpec((ROW_TILE, 3 * d), lambda i: (i, 0)),
                   pl.BlockSpec((ROW_TILE, LANES), lambda i: (i, 0))],
        out_shape=[jax.ShapeDtypeStruct((n, 3 * d), BF16), jax.ShapeDtypeStruct((n, LANES), F32)],
        scratch_shapes=[pltpu.VMEM((1, LANES), F32)],
        compiler_params=_cparams(("arbitrary",)),
        name="odd_in_proj",
    )(x, g.reshape(1, d), w_in[:, :3 * d].astype(BF16), wf, bf, gq, gk, e, et, tri)


def _flash_two_maps(q_ref, k_ref, v_ref, bias0, bias1, qi, tile):
    q = q_ref[...]
    lane = lax.broadcasted_iota(jnp.int32, q.shape, 1)
    zero = jnp.zeros_like(q)
    qs = (jnp.where(lane < HEAD_DIM, q, zero), jnp.where(lane >= HEAD_DIM, q, zero))
    biases = (bias0, bias1)
    row = lax.broadcasted_iota(jnp.int32, (tile, tile), 0)
    col = lax.broadcasted_iota(jnp.int32, (tile, tile), 1)
    causal = row >= col

    def step(kt, carry, masked):
        start = pl.multiple_of(kt * tile, tile)
        k = k_ref[pl.ds(start, tile), :]
        v = v_ref[pl.ds(start, tile), :]
        new = []
        for mp in range(2):
            m, l, acc = carry[mp]
            s = _dot_nt(qs[mp], k) + biases[mp](kt)
            if masked:
                s = jnp.where(causal, s, NEG)
            m_new = jnp.maximum(m, jnp.max(s, axis=-1, keepdims=True))
            alpha = jnp.exp(m - m_new)
            p = jnp.exp(s - m_new)
            l = alpha * l + jnp.sum(p, axis=-1, keepdims=True)
            acc = alpha * acc + _dot(p.astype(BF16), v)
            new.append((m_new, l, acc))
        return tuple(new)

    init_one = (jnp.full((tile, 1), NEG, F32), jnp.zeros((tile, 1), F32), jnp.zeros((tile, LANES), F32))
    carry = lax.fori_loop(0, qi, lambda kt, c: step(kt, c, False), (init_one, init_one))
    (_, l0, acc0), (_, l1, acc1) = step(qi, carry, True)
    return acc0, l0, acc1, l1


def _fox_kernel(q_ref, k_ref, v_ref, f_ref, o_ref, *, tile):
    qi = pl.program_id(2)
    qstart = pl.multiple_of(qi * tile, tile)

    def bias(mp):
        f_q0 = f_ref[mp:mp + 1, pl.ds(qstart, LANES)][:, 0:1]
        return lambda kt: f_q0 - f_ref[mp:mp + 1, pl.ds(pl.multiple_of(kt * tile, tile), tile)]

    acc0, l0, acc1, l1 = _flash_two_maps(q_ref, k_ref, v_ref, bias(0), bias(1), qi, tile)
    lane = lax.broadcasted_iota(jnp.int32, acc0.shape, 1)
    o_ref[...] = jnp.where(lane < HEAD_DIM, acc0 / l0, acc1 / l1).astype(BF16)


def _fox_attention(qkv, cum_t, batch, seq):
    n, d3 = qkv.shape
    d = d3 // 3
    pairs = d // LANES
    tile = ATT_TILE
    nq = seq // tile
    return pl.pallas_call(
        functools.partial(_fox_kernel, tile=tile),
        grid=(batch, pairs, nq),
        in_specs=[pl.BlockSpec((tile, LANES), lambda b, h, i: (b * nq + i, h)),
                  pl.BlockSpec((seq, LANES), lambda b, h, i: (b, pairs + h)),
                  pl.BlockSpec((seq, LANES), lambda b, h, i: (b, 2 * pairs + h)),
                  pl.BlockSpec((None, None, 2, seq), lambda b, h, i: (b, h, 0, 0))],
        out_specs=pl.BlockSpec((tile, LANES), lambda b, h, i: (b * nq + i, h)),
        out_shape=jax.ShapeDtypeStruct((n, d), BF16),
        compiler_params=_cparams(("arbitrary", "arbitrary", "arbitrary")),
        name="fox_attention",
    )(qkv, qkv, qkv, cum_t)


def _diff_kernel(slope_ref, q_ref, k_ref, v_ref, lam_ref, sub_ref, o_ref, *, tile, lam_init):
    h = pl.program_id(1)
    qi = pl.program_id(2)
    slope = slope_ref[h]
    key_off = lax.broadcasted_iota(jnp.int32, (1, tile), 1)

    def bias(kt):
        return slope * ((kt - qi) * tile + key_off).astype(F32)

    acc0, l0, acc1, l1 = _flash_two_maps(q_ref, k_ref, v_ref, bias, bias, qi, tile)
    lp = lam_ref[...]
    lam = (jnp.exp(jnp.sum(lp[0:1] * lp[1:2], axis=-1, keepdims=True))
           - jnp.exp(jnp.sum(lp[2:3] * lp[3:4], axis=-1, keepdims=True)) + lam_init)
    o = acc0 / l0 - lam * (acc1 / l1)
    o_ref[...] = (_rms_rows(o, sub_ref[...]) * (1.0 - lam_init)).astype(BF16)


def _diff_attention(proj, lam_params, subln_g, batch, seq, layer):
    n = proj.shape[0]
    tile = ATT_TILE
    nq = seq // tile
    lam_init = 0.8 - 0.6 * math.exp(-0.3 * layer)
    slopes = jnp.asarray([2.0 ** (-8.0 * (i + 1) / A_HEADS) for i in range(A_HEADS)], F32)
    return pl.pallas_call(
        functools.partial(_diff_kernel, tile=tile, lam_init=lam_init),
        grid_spec=pltpu.PrefetchScalarGridSpec(
            num_scalar_prefetch=1,
            grid=(batch, A_HEADS, nq),
            in_specs=[pl.BlockSpec((tile, LANES), lambda b, h, i, s: (b * nq + i, h)),
                      pl.BlockSpec((seq, LANES), lambda b, h, i, s: (b, A_HEADS + h)),
                      pl.BlockSpec((seq, LANES), lambda b, h, i, s: (b, 2 * A_HEADS + h)),
                      pl.BlockSpec((4, HEAD_DIM), lambda b, h, i, s: (0, 0)),
                      pl.BlockSpec((1, LANES), lambda b, h, i, s: (0, 0))],
            out_specs=pl.BlockSpec((tile, LANES), lambda b, h, i, s: (b * nq + i, h))),
        out_shape=jax.ShapeDtypeStruct((n, A_HEADS * LANES), BF16),
        compiler_params=_cparams(("arbitrary", "arbitrary", "arbitrary")),
        name="diff_attention",
    )(slopes, proj, proj, proj, lam_params.astype(F32), subln_g.astype(F32).reshape(1, LANES))


def _stick_kernel(q_ref, k_ref, v_ref, tri_ref, o_ref, *, tile):
    qi = pl.program_id(2)
    q = q_ref[...]
    lane = lax.broadcasted_iota(jnp.int32, q.shape, 1)
    zero = jnp.zeros_like(q)
    row = lax.broadcasted_iota(jnp.int32, (tile, tile), 0)
    col = lax.broadcasted_iota(jnp.int32, (tile, tile), 1)
    before = row > col
    tri = tri_ref[...]

    def head(qh):
        def step(kt, carry, masked):
            run, acc = carry
            start = pl.multiple_of(kt * tile, tile)
            k = k_ref[pl.ds(start, tile), :]
            v = v_ref[pl.ds(start, tile), :]
            z = _dot_nt(qh, k)
            sp = _softplus(z)
            log_keep = -sp
            if masked:
                log_keep = jnp.where(before, log_keep, 0.0)
            hi, lo = _split2(log_keep)
            later = _dot(hi, tri) + _dot(lo, tri)
            w = jnp.exp((z - sp) + later + run)
            if masked:
                w = jnp.where(before, w, 0.0)
            acc = acc + _dot(w.astype(BF16), v)
            run = run + jnp.sum(log_keep, axis=-1, keepdims=True)
            return run, acc

        carry = step(qi, (jnp.zeros((tile, 1), F32), jnp.zeros((tile, LANES), F32)), True)
        _, acc = lax.fori_loop(0, qi, lambda j, c: step(qi - 1 - j, c, False), carry)
        return acc

    acc0 = head(jnp.where(lane < HEAD_DIM, q, zero))
    acc1 = head(jnp.where(lane >= HEAD_DIM, q, zero))
    o_ref[...] = jnp.where(lane < HEAD_DIM, acc0, acc1).astype(BF16)


def _stick_attention(proj, batch, seq, col0):
    n = proj.shape[0]
    pairs = 4
    tile = SB_TILE
    nq = seq // tile
    tri = (jnp.arange(tile)[:, None] > jnp.arange(tile)[None, :]).astype(BF16)
    return pl.pallas_call(
        functools.partial(_stick_kernel, tile=tile),
        grid=(batch, pairs, nq),
        in_specs=[pl.BlockSpec((tile, LANES), lambda b, h, i: (b * nq + i, col0 + h)),
                  pl.BlockSpec((seq, LANES), lambda b, h, i: (b, col0 + pairs + h)),
                  pl.BlockSpec((seq, LANES), lambda b, h, i: (b, col0 + 2 * pairs + h)),
                  pl.BlockSpec((tile, tile), lambda b, h, i: (0, 0))],
        out_specs=pl.BlockSpec((tile, LANES), lambda b, h, i: (b * nq + i, h)),
        out_shape=jax.ShapeDtypeStruct((n, pairs * LANES), BF16),
        compiler_params=_cparams(("arbitrary", "arbitrary", "arbitrary")),
        name="stick_attention",
    )(proj, proj, proj, tri)


def _out_router_kernel(*refs, n_parts):
    x_ref = refs[0]
    o_refs = refs[1:1 + n_parts]
    w_refs = refs[1 + n_parts:1 + 2 * n_parts]
    g_ref, wr_hi_ref, wr_lo_ref, br_ref, xo_ref, h_ref, route_ref = refs[1 + 2 * n_parts:]
    mix = _dot(o_refs[0][...], w_refs[0][...])
    for p in range(1, n_parts):
        mix = mix + _dot(o_refs[p][...], w_refs[p][...])
    x = x_ref[...] + mix
    xo_ref[...] = x
    h = _rms_rows(x, g_ref[...])
    h_ref[...] = h

    h_hi, h_lo = _split2(h)
    logits = (_dot(h_hi, wr_hi_ref[...]) + _dot(h_hi, wr_lo_ref[...]) + _dot(h_lo, wr_hi_ref[...])
              + br_ref[...])
    lane = lax.broadcasted_iota(jnp.int32, logits.shape, 1)

    def first_max(vals):
        top = jnp.max(vals, axis=-1, keepdims=True)
        idx = jnp.min(jnp.where(vals == top, lane, LANES), axis=-1, keepdims=True)
        return top, idx

    g_logits = jnp.where(lane < N_GROUPS, logits, NEG)
    g_top, g_idx = first_max(g_logits)
    g_w = 1.0 / jnp.sum(jnp.exp(g_logits - g_top), axis=-1, keepdims=True)
    lo_lane = N_GROUPS + EXPERTS_PER_GROUP * g_idx
    in_group = (lane >= lo_lane) & (lane < lo_lane + EXPERTS_PER_GROUP)
    e_logits = jnp.where(in_group, logits, NEG)
    v1, i1 = first_max(e_logits)
    v2, i2 = first_max(jnp.where(lane == i1, NEG, e_logits))
    e2 = jnp.exp(v2 - v1)
    w1 = g_w / (1.0 + e2)
    w2 = g_w * e2 / (1.0 + e2)
    route = jnp.where(lane == 0, (i1 - N_GROUPS).astype(F32),
                      jnp.where(lane == 1, (i2 - N_GROUPS).astype(F32),
                                jnp.where(lane == 2, w1, jnp.where(lane == 3, w2, 0.0))))
    route_ref[...] = route


def _out_router(x, o_parts, w_out, g_ffn, w_group, b_group, w_router, b_router):
    n, d = x.shape
    n_parts = len(o_parts)
    w_bf = w_out.astype(BF16)
    w_parts, r0 = [], 0
    for o in o_parts:
        w_parts.append(w_bf[r0:r0 + o.shape[1]])
        r0 += o.shape[1]
    wr = jnp.zeros((d, LANES), F32).at[:, :N_GROUPS].set(w_group)
    wr = wr.at[:, N_GROUPS:N_GROUPS + N_EXPERTS].set(w_router)
    wr_hi = wr.astype(BF16)
    wr_lo = (wr - wr_hi.astype(F32)).astype(BF16)
    br = jnp.zeros((1, LANES), F32).at[0, :N_GROUPS].set(b_group.astype(F32))
    br = br.at[0, N_GROUPS:N_GROUPS + N_EXPERTS].set(b_router.astype(F32))
    rows = lambda width: pl.BlockSpec((ROW_TILE, width), lambda i: (i, 0))
    const = lambda shape: pl.BlockSpec(shape, lambda i: (0, 0))
    return pl.pallas_call(
        functools.partial(_out_router_kernel, n_parts=n_parts),
        grid=(n // ROW_TILE,),
        in_specs=([rows(d)] + [rows(o.shape[1]) for o in o_parts]
                  + [const(w.shape) for w in w_parts]
                  + [const((1, d)), const((d, LANES)), const((d, LANES)), const((1, LANES))]),
        out_specs=[rows(d), rows(d), rows(LANES)],
        out_shape=[jax.ShapeDtypeStruct((n, d), F32), jax.ShapeDtypeStruct((n, d), F32),
                   jax.ShapeDtypeStruct((n, LANES), F32)],
        compiler_params=_cparams(("arbitrary",)),
        name="out_router",
    )(x, *o_parts, *w_parts, g_ffn.reshape(1, d), wr_hi, wr_lo, br)


def _dispatch_plan(route, n_blocks):
    n = route.shape[0]
    e_flat = route[:, 0:2].astype(jnp.int32).reshape(-1)
    w_flat = route[:, 2:4].reshape(-1)
    onehot = (e_flat[:, None] == jnp.arange(N_EXPERTS)[None, :]).astype(jnp.int32)
    csum = jnp.cumsum(onehot, axis=0)
    rank = jnp.take_along_axis(csum, e_flat[:, None], axis=1)[:, 0] - 1
    counts = csum[-1]
    padded = (counts + MOE_TILE - 1) // MOE_TILE * MOE_TILE
    pad_ends = jnp.cumsum(padded)
    dest = ((pad_ends - padded)[e_flat] + rank).astype(jnp.int32)
    p = n_blocks * MOE_TILE
    tok_buf = jnp.zeros((p,), jnp.int32).at[dest].set(jnp.arange(2 * n, dtype=jnp.int32) // 2)
    w_buf = jnp.zeros((p,), F32).at[dest].set(w_flat)
    blk_e = jnp.searchsorted(pad_ends, jnp.arange(n_blocks) * MOE_TILE, side='right')
    blk_e = jnp.minimum(blk_e, N_EXPERTS - 1).astype(jnp.int32)
    n_used = (pad_ends[-1] // MOE_TILE).astype(jnp.int32).reshape(1)
    return dest, tok_buf, w_buf, blk_e, n_used


def _moe_kernel(blk_e_ref, tok_ref, nused_ref, h_hbm, wrow_ref, wg_ref, wu_ref, wd_ref, y_ref,
                xbuf, sem):
    i = pl.program_id(0)
    n_used = nused_ref[0]
    slot = i % 2

    def gather_copy(blk, slot_, r):
        tok = tok_ref[blk * MOE_TILE + r]
        return pltpu.make_async_copy(h_hbm.at[pl.ds(tok, 1), :], xbuf.at[slot_, pl.ds(r, 1), :],
                                     sem.at[slot_])

    def start_gather(blk, slot_):
        def body(r, c):
            gather_copy(blk, slot_, r).start()
            return c
        lax.fori_loop(0, MOE_TILE, body, 0, unroll=8)

    @pl.when((i == 0) & (n_used > 0))
    def _():
        start_gather(0, 0)

    @pl.when(i + 1 < n_used)
    def _():
        start_gather(i + 1, 1 - slot)

    @pl.when(i < n_used)
    def _():
        def body(r, c):
            gather_copy(i, slot, r).wait()
            return c
        lax.fori_loop(0, MOE_TILE, body, 0, unroll=8)
        xb = xbuf[slot].astype(BF16)
        gate = _dot(xb, wg_ref[...])
        up = _dot(xb, wu_ref[...])
        act = (gate * jax.nn.sigmoid(gate) * up).astype(BF16)
        y_ref[...] = _dot(act, wd_ref[...]) * wrow_ref[...]

    @pl.when(i >= n_used)
    def _():
        y_ref[...] = jnp.zeros_like(y_ref)


def _moe_experts(h, tok_buf, w_buf, blk_e, n_used, w_gate, w_up, w_down, n_blocks):
    n, d = h.shape
    f = w_gate.shape[-1]
    p = n_blocks * MOE_TILE
    return pl.pallas_call(
        _moe_kernel,
        grid_spec=pltpu.PrefetchScalarGridSpec(
            num_scalar_prefetch=3,
            grid=(n_blocks,),
            in_specs=[pl.BlockSpec(memory_space=pl.ANY),
                      pl.BlockSpec((MOE_TILE, 1), lambda i, be, tk, nu: (i, 0)),
                      pl.BlockSpec((None, d, f), lambda i, be, tk, nu: (be[i], 0, 0)),
                      pl.BlockSpec((None, d, f), lambda i, be, tk, nu: (be[i], 0, 0)),
                      pl.BlockSpec((None, f, d), lambda i, be, tk, nu: (be[i], 0, 0))],
            out_specs=pl.BlockSpec((MOE_TILE, d), lambda i, be, tk, nu: (i, 0)),
            scratch_shapes=[pltpu.VMEM((2, MOE_TILE, d), F32), pltpu.SemaphoreType.DMA((2,))]),
        out_shape=jax.ShapeDtypeStruct((p, d), F32),
        compiler_params=_cparams(("arbitrary",)),
        name="moe_experts",
    )(blk_e, tok_buf, n_used, h, w_buf.reshape(p, 1), w_gate, w_up, w_down)


def _combine_kernel(dest_ref, x_ref, y_hbm, o_ref, ybuf, sem):
    i = pl.program_id(0)
    nsteps = pl.num_programs(0)
    slot = i % 2

    def row_copy(step, slot_, r, k):
        src = dest_ref[(step * CMB_TILE + r) * 2 + k]
        return pltpu.make_async_copy(y_hbm.at[pl.ds(src, 1), :], ybuf.at[slot_, k, pl.ds(r, 1), :],
                                     sem.at[slot_])

    def start_gather(step, slot_):
        def body(r, c):
            row_copy(step, slot_, r, 0).start()
            row_copy(step, slot_, r, 1).start()
            return c
        lax.fori_loop(0, CMB_TILE, body, 0, unroll=4)

    @pl.when(i == 0)
    def _():
        start_gather(0, 0)

    @pl.when(i + 1 < nsteps)
    def _():
        start_gather(i + 1, 1 - slot)

    def wait_body(r, c):
        row_copy(i, slot, r, 0).wait()
        row_copy(i, slot, r, 1).wait()
        return c
    lax.fori_loop(0, CMB_TILE, wait_body, 0, unroll=4)
    o_ref[...] = x_ref[...] + (ybuf[slot, 0] + ybuf[slot, 1])


def _combine(x, y_buf, dest):
    n, d = x.shape
    return pl.pallas_call(
        _combine_kernel,
        grid_spec=pltpu.PrefetchScalarGridSpec(
            num_scalar_prefetch=1,
            grid=(n // CMB_TILE,),
            in_specs=[pl.BlockSpec((CMB_TILE, d), lambda i, ds: (i, 0)),
                      pl.BlockSpec(memory_space=pl.ANY)],
            out_specs=pl.BlockSpec((CMB_TILE, d), lambda i, ds: (i, 0)),
            scratch_shapes=[pltpu.VMEM((2, 2, CMB_TILE, d), F32), pltpu.SemaphoreType.DMA((2,))]),
        out_shape=jax.ShapeDtypeStruct((n, d), F32),
        compiler_params=_cparams(("arbitrary",)),
        name="moe_combine",
    )(dest, x, y_buf)


def kernel(x, norm_mix_g, norm_ffn_g, ab_w_in, a_q_gain, a_k_gain, a_lam_q1, a_lam_k1, a_lam_q2,
           a_lam_k2, a_subln_g, ab_w_out, c_w_in, c_b_f, c_q_gain, c_k_gain, c_w_out, moe_w_group,
           moe_b_group, moe_w_router, moe_b_router, moe_w_gate, moe_w_up, moe_w_down):
    batch, seq, d = x.shape
    n = batch * seq
    depth = norm_mix_g.shape[0]
    heads = d // HEAD_DIM
    n_blocks = (2 * n) // MOE_TILE + N_EXPERTS
    xs = x.reshape(n, d).astype(F32)
    for layer in range(depth):
        i = layer // 2
        if layer % 2 == 0:
            proj = _even_in_proj(xs, norm_mix_g[layer], ab_w_in[i], a_q_gain[i], a_k_gain[i])
            lam_params = jnp.stack([a_lam_q1[i], a_lam_k1[i], a_lam_q2[i], a_lam_k2[i]])
            o_a = _diff_attention(proj, lam_params, a_subln_g[i], batch, seq, layer)
            o_b = _stick_attention(proj, batch, seq, 3 * A_HEADS)
            o_parts, w_out = [o_a, o_b], ab_w_out[i]
        else:
            qkv, cum = _odd_in_proj(xs, norm_mix_g[layer], c_w_in[i], c_b_f[i], c_q_gain[i],
                                    c_k_gain[i], seq)
            cum_t = cum[:, :heads].reshape(batch, seq, heads // 2, 2).transpose(0, 2, 3, 1)
            o_parts, w_out = [_fox_attention(qkv, cum_t, batch, seq)], c_w_out[i]
        xs, h, route = _out_router(xs, o_parts, w_out, norm_ffn_g[layer], moe_w_group[layer],
                                   moe_b_group[layer], moe_w_router[layer], moe_b_router[layer])
        dest, tok_buf, w_buf, blk_e, n_used = _dispatch_plan(route, n_blocks)
        y_buf = _moe_experts(h, tok_buf, w_buf, blk_e, n_used, moe_w_gate[layer].astype(BF16),
                             moe_w_up[layer].astype(BF16), moe_w_down[layer].astype(BF16), n_blocks)
        xs = _combine(xs, y_buf, dest)
    return xs.reshape(batch, seq, d).astype(x.dtype)
```

```python
import functools
import math

import jax
import jax.numpy as jnp
from jax import lax
from jax.experimental import pallas as pl
from jax.experimental.pallas import tpu as pltpu

F32 = jnp.float32
BF16 = jnp.bfloat16

HEAD_DIM = 64
LANES = 128
EPS = 1e-6
SCALE = HEAD_DIM ** -0.5
N_GROUPS = 4
EXPERTS_PER_GROUP = 4
N_EXPERTS = N_GROUPS * EXPERTS_PER_GROUP
A_HEADS = 4
NEG = -1e30

ROW_TILE = 512
ATT_TILE = 512
SB_Q_TILE = 512
SB_K_TILE = 256
EXP_ZERO_BELOW = -104.0
MOE_TILE = 256
CMB_TILE = 256
SEC = 512
VMEM_LIMIT = 48 * 1024 * 1024


def _cparams(sem):
    return pltpu.CompilerParams(dimension_semantics=sem, vmem_limit_bytes=VMEM_LIMIT)


def _dot(a, b):
    return jnp.dot(a, b, preferred_element_type=F32)


def _dot_nt(a, b):
    return lax.dot_general(a, b, (((1,), (1,)), ((), ())), preferred_element_type=F32)


def _split2(x):
    hi = x.astype(BF16)
    lo = (x - hi.astype(F32)).astype(BF16)
    return hi, lo


def _rms_rows(x, g):
    return x * lax.rsqrt(jnp.mean(x * x, axis=-1, keepdims=True) + EPS) * g


def _group_rsqrt(acc, e_ref, et_ref):
    hi, lo = _split2(acc * acc)
    ss = _dot(hi, e_ref[...]) + _dot(lo, e_ref[...])
    rhi, rlo = _split2(lax.rsqrt(ss * (1.0 / HEAD_DIM) + EPS))
    return _dot(rhi, et_ref[...]) + _dot(rlo, et_ref[...])


def _softplus(z):
    return jnp.maximum(z, 0.0) + jnp.log(1.0 + jnp.exp(-jnp.abs(z)))


def _even_in_kernel(x_ref, g_ref, w_ref, gq_ref, gk_ref, e_ref, et_ref, o_ref):
    xn = _rms_rows(x_ref[...], g_ref[...]).astype(BF16)
    for sec in range(6):
        cols = slice(sec * SEC, (sec + 1) * SEC)
        acc = _dot(xn, w_ref[:, cols])
        if sec == 0:
            acc = acc * _group_rsqrt(acc, e_ref, et_ref) * gq_ref[...]
        elif sec == 1:
            acc = acc * _group_rsqrt(acc, e_ref, et_ref) * gk_ref[...]
        elif sec == 3:
            acc = acc * SCALE
        o_ref[:, cols] = acc.astype(BF16)


def _odd_in_kernel(x_ref, g_ref, w_ref, wf_ref, bf_ref, gq_ref, gk_ref, e_ref, et_ref, tri_ref,
                   o_ref, cum_ref, carry_ref, *, tiles_per_seq):
    i = pl.program_id(0)
    xn = _rms_rows(x_ref[...], g_ref[...]).astype(BF16)
    for sec in range(6):
        cols = slice(sec * SEC, (sec + 1) * SEC)
        acc = _dot(xn, w_ref[:, cols])
        if sec < 2:
            acc = acc * _group_rsqrt(acc, e_ref, et_ref) * gq_ref[...]
        elif sec < 4:
            acc = acc * _group_rsqrt(acc, e_ref, et_ref) * gk_ref[...]
        o_ref[:, cols] = acc.astype(BF16)

    log_f = -_softplus(-(_dot(xn, wf_ref[...]) + bf_ref[...]))
    p1 = log_f.astype(BF16)
    r1 = log_f - p1.astype(F32)
    p2 = r1.astype(BF16)
    p3 = (r1 - p2.astype(F32)).astype(BF16)
    tri = tri_ref[...]
    within = _dot(tri, p1) + _dot(tri, p2) + _dot(tri, p3)

    @pl.when(i % tiles_per_seq == 0)
    def _():
        carry_ref[...] = jnp.zeros_like(carry_ref)

    cum = within + carry_ref[...]
    cum_ref[...] = cum
    carry_ref[...] = cum[-1:, :]


def _group_maps():
    lane_grp = jnp.arange(SEC)[:, None] // HEAD_DIM
    e = (lane_grp == jnp.arange(LANES)[None, :]).astype(BF16)
    return e, e.T


def _even_in_proj(x, g, w_in, q_gain, k_gain):
    n, d = x.shape
    e, et = _group_maps()
    reps = SEC // HEAD_DIM
    gq = jnp.tile(q_gain.astype(F32) * SCALE, reps).reshape(1, SEC)
    gk = jnp.tile(k_gain.astype(F32), reps).reshape(1, SEC)
    const = lambda shape: pl.BlockSpec(shape, lambda i: (0, 0))
    return pl.pallas_call(
        _even_in_kernel,
        grid=(n // ROW_TILE,),
        in_specs=[pl.BlockSpec((ROW_TILE, d), lambda i: (i, 0)), const((1, d)), const((d, 3 * d)),
                  const((1, SEC)), const((1, SEC)), const((SEC, LANES)), const((LANES, SEC))],
        out_specs=pl.BlockSpec((ROW_TILE, 3 * d), lambda i: (i, 0)),
        out_shape=jax.ShapeDtypeStruct((n, 3 * d), BF16),
        compiler_params=_cparams(("arbitrary",)),
        name="even_in_proj",
    )(x, g.reshape(1, d), w_in.astype(BF16), gq, gk, e, et)


def _odd_in_proj(x, g, w_in, b_f, q_gain, k_gain, seq):
    n, d = x.shape
    heads = d // HEAD_DIM
    e, et = _group_maps()
    reps = SEC // HEAD_DIM
    gq = jnp.tile(q_gain.astype(F32) * SCALE, reps).reshape(1, SEC)
    gk = jnp.tile(k_gain.astype(F32), reps).reshape(1, SEC)
    wf = jnp.zeros((d, LANES), F32).at[:, :heads].set(w_in[:, 3 * d:]).astype(BF16)
    bf = jnp.zeros((1, LANES), F32).at[0, :heads].set(b_f.astype(F32))
    tri = (jnp.arange(ROW_TILE)[:, None] >= jnp.arange(ROW_TILE)[None, :]).astype(BF16)
    const = lambda shape: pl.BlockSpec(shape, lambda i: (0, 0))
    return pl.pallas_call(
        functools.partial(_odd_in_kernel, tiles_per_seq=seq // ROW_TILE),
        grid=(n // ROW_TILE,),
        in_specs=[pl.BlockSpec((ROW_TILE, d), lambda i: (i, 0)), const((1, d)), const((d, 3 * d)),
                  const((d, LANES)), const((1, LANES)), const((1, SEC)), const((1, SEC)),
                  const((SEC, LANES)), const((LANES, SEC)), const((ROW_TILE, ROW_TILE))],
        out_specs=[pl.BlockSpec((ROW_TILE, 3 * d), lambda i: (i, 0)),
                   pl.BlockSpec((ROW_TILE, LANES), lambda i: (i, 0))],
        out_shape=[jax.ShapeDtypeStruct((n, 3 * d), BF16), jax.ShapeDtypeStruct((n, LANES), F32)],
        scratch_shapes=[pltpu.VMEM((1, LANES), F32)],
        compiler_params=_cparams(("arbitrary",)),
        name="odd_in_proj",
    )(x, g.reshape(1, d), w_in[:, :3 * d].astype(BF16), wf, bf, gq, gk, e, et, tri)


def _flash_two_maps(q_ref, k_ref, v_ref, bias0, bias1, qi, tile):
    q = q_ref[...]
    lane = lax.broadcasted_iota(jnp.int32, q.shape, 1)
    zero = jnp.zeros_like(q)
    qs = (jnp.where(lane < HEAD_DIM, q, zero), jnp.where(lane >= HEAD_DIM, q, zero))
    biases = (bias0, bias1)
    row = lax.broadcasted_iota(jnp.int32, (tile, tile), 0)
    col = lax.broadcasted_iota(jnp.int32, (tile, tile), 1)
    causal = row >= col

    def step(kt, carry, masked):
        start = pl.multiple_of(kt * tile, tile)
        k = k_ref[pl.ds(start, tile), :]
        v = v_ref[pl.ds(start, tile), :]
        new = []
        for mp in range(2):
            m, l, acc = carry[mp]
            s = _dot_nt(qs[mp], k) + biases[mp](kt)
            if masked:
                s = jnp.where(causal, s, NEG)
            m_new = jnp.maximum(m, jnp.max(s, axis=-1, keepdims=True))
            alpha = jnp.exp(m - m_new)
            p = jnp.exp(s - m_new)
            l = alpha * l + jnp.sum(p, axis=-1, keepdims=True)
            acc = alpha * acc + _dot(p.astype(BF16), v)
            new.append((m_new, l, acc))
        return tuple(new)

    init_one = (jnp.full((tile, 1), NEG, F32), jnp.zeros((tile, 1), F32), jnp.zeros((tile, LANES), F32))
    carry = lax.fori_loop(0, qi, lambda kt, c: step(kt, c, False), (init_one, init_one))
    (_, l0, acc0), (_, l1, acc1) = step(qi, carry, True)
    return acc0, l0, acc1, l1


def _fox_kernel(q_ref, k_ref, v_ref, f_ref, o_ref, *, tile):
    qi = pl.program_id(2)
    qstart = pl.multiple_of(qi * tile, tile)

    def bias(mp):
        f_q0 = f_ref[mp:mp + 1, pl.ds(qstart, LANES)][:, 0:1]
        return lambda kt: f_q0 - f_ref[mp:mp + 1, pl.ds(pl.multiple_of(kt * tile, tile), tile)]

    acc0, l0, acc1, l1 = _flash_two_maps(q_ref, k_ref, v_ref, bias(0), bias(1), qi, tile)
    lane = lax.broadcasted_iota(jnp.int32, acc0.shape, 1)
    o_ref[...] = jnp.where(lane < HEAD_DIM, acc0 / l0, acc1 / l1).astype(BF16)


def _fox_attention(qkv, cum_t, batch, seq):
    n, d3 = qkv.shape
    d = d3 // 3
    pairs = d // LANES
    tile = ATT_TILE
    nq = seq // tile
    return pl.pallas_call(
        functools.partial(_fox_kernel, tile=tile),
        grid=(batch, pairs, nq),
        in_specs=[pl.BlockSpec((tile, LANES), lambda b, h, i: (b * nq + i, h)),
                  pl.BlockSpec((seq, LANES), lambda b, h, i: (b, pairs + h)),
                  pl.BlockSpec((seq, LANES), lambda b, h, i: (b, 2 * pairs + h)),
                  pl.BlockSpec((None, None, 2, seq), lambda b, h, i: (b, h, 0, 0))],
        out_specs=pl.BlockSpec((tile, LANES), lambda b, h, i: (b * nq + i, h)),
        out_shape=jax.ShapeDtypeStruct((n, d), BF16),
        compiler_params=_cparams(("arbitrary", "arbitrary", "arbitrary")),
        name="fox_attention",
    )(qkv, qkv, qkv, cum_t)


def _diff_kernel(slope_ref, q_ref, k_ref, v_ref, lam_ref, sub_ref, o_ref, *, tile, lam_init):
    h = pl.program_id(1)
    qi = pl.program_id(2)
    slope = slope_ref[h]
    key_off = lax.broadcasted_iota(jnp.int32, (1, tile), 1)

    def bias(kt):
        return slope * ((kt - qi) * tile + key_off).astype(F32)

    acc0, l0, acc1, l1 = _flash_two_maps(q_ref, k_ref, v_ref, bias, bias, qi, tile)
    lp = lam_ref[...]
    lam = (jnp.exp(jnp.sum(lp[0:1] * lp[1:2], axis=-1, keepdims=True))
           - jnp.exp(jnp.sum(lp[2:3] * lp[3:4], axis=-1, keepdims=True)) + lam_init)
    o = acc0 / l0 - lam * (acc1 / l1)
    o_ref[...] = (_rms_rows(o, sub_ref[...]) * (1.0 - lam_init)).astype(BF16)


def _diff_attention(proj, lam_params, subln_g, batch, seq, layer):
    n = proj.shape[0]
    tile = ATT_TILE
    nq = seq // tile
    lam_init = 0.8 - 0.6 * math.exp(-0.3 * layer)
    slopes = jnp.asarray([2.0 ** (-8.0 * (i + 1) / A_HEADS) for i in range(A_HEADS)], F32)
    return pl.pallas_call(
        functools.partial(_diff_kernel, tile=tile, lam_init=lam_init),
        grid_spec=pltpu.PrefetchScalarGridSpec(
            num_scalar_prefetch=1,
            grid=(batch, A_HEADS, nq),
            in_specs=[pl.BlockSpec((tile, LANES), lambda b, h, i, s: (b * nq + i, h)),
                      pl.BlockSpec((seq, LANES), lambda b, h, i, s: (b, A_HEADS + h)),
                      pl.BlockSpec((seq, LANES), lambda b, h, i, s: (b, 2 * A_HEADS + h)),
                      pl.BlockSpec((4, HEAD_DIM), lambda b, h, i, s: (0, 0)),
                      pl.BlockSpec((1, LANES), lambda b, h, i, s: (0, 0))],
            out_specs=pl.BlockSpec((tile, LANES), lambda b, h, i, s: (b * nq + i, h))),
        out_shape=jax.ShapeDtypeStruct((n, A_HEADS * LANES), BF16),
        compiler_params=_cparams(("arbitrary", "arbitrary", "arbitrary")),
        name="diff_attention",
    )(slopes, proj, proj, proj, lam_params.astype(F32), subln_g.astype(F32).reshape(1, LANES))


def _stick_kernel(q_ref, k_ref, v_ref, ntri_ref, o_ref, *, tq, tk):
    qi = pl.program_id(2)
    ratio = tq // tk
    q = q_ref[...]
    lane = lax.broadcasted_iota(jnp.int32, q.shape, 1)
    zero = jnp.zeros_like(q)
    qs = (jnp.where(lane < HEAD_DIM, q, zero), jnp.where(lane >= HEAD_DIM, q, zero))
    rel = (lax.broadcasted_iota(jnp.int32, (tq, tk), 0)
           - lax.broadcasted_iota(jnp.int32, (tq, tk), 1))
    ntri = ntri_ref[...]

    def step(kt, carry, masked):
        start = pl.multiple_of(kt * tk, tk)
        k = k_ref[pl.ds(start, tk), :]
        v = v_ref[pl.ds(start, tk), :]
        if masked:
            before = rel > kt * tk - qi * tq
        new = []
        for hh in range(2):
            run, acc = carry[hh]
            z = _dot_nt(qs[hh], k)
            sp = _softplus(z)
            drop = jnp.where(before, sp, 0.0) if masked else sp
            hi, lo = _split2(drop)
            later = _dot(hi, ntri) + _dot(lo, ntri)
            w = jnp.exp((z - sp) + later + run)
            if masked:
                w = jnp.where(before, w, 0.0)
            acc = acc + _dot(w.astype(BF16), v)
            run = run - jnp.sum(drop, axis=-1, keepdims=True)
            new.append((run, acc))
        return tuple(new)

    init_one = (jnp.zeros((tq, 1), F32), jnp.zeros((tq, LANES), F32))
    carry = (init_one, init_one)
    for d in range(ratio - 1, -1, -1):
        carry = step(qi * ratio + d, carry, True)

    def live(c):
        top = jnp.maximum(jnp.max(c[0][0]), jnp.max(c[1][0]))
        return (top > EXP_ZERO_BELOW).astype(jnp.int32)

    def cond(state):
        kt, alive, _ = state
        return (kt >= 0) & (alive > 0)

    def body(state):
        kt, _, c = state
        c = step(kt, c, False)
        return kt - 1, live(c), c

    _, _, carry = lax.while_loop(cond, body, (qi * ratio - 1, live(carry), carry))
    o_ref[...] = jnp.where(lane < HEAD_DIM, carry[0][1], carry[1][1]).astype(BF16)


def _stick_attention(proj, batch, seq, col0):
    n = proj.shape[0]
    pairs = 4
    tq, tk = SB_Q_TILE, SB_K_TILE
    nq = seq // tq
    ntri = -(jnp.arange(tk)[:, None] > jnp.arange(tk)[None, :]).astype(BF16)
    return pl.pallas_call(
        functools.partial(_stick_kernel, tq=tq, tk=tk),
        grid=(batch, pairs, nq),
        in_specs=[pl.BlockSpec((tq, LANES), lambda b, h, i: (b * nq + i, col0 + h)),
                  pl.BlockSpec((seq, LANES), lambda b, h, i: (b, col0 + pairs + h)),
                  pl.BlockSpec((seq, LANES), lambda b, h, i: (b, col0 + 2 * pairs + h)),
                  pl.BlockSpec((tk, tk), lambda b, h, i: (0, 0))],
        out_specs=pl.BlockSpec((tq, LANES), lambda b, h, i: (b * nq + i, h)),
        out_shape=jax.ShapeDtypeStruct((n, pairs * LANES), BF16),
        compiler_params=_cparams(("arbitrary", "arbitrary", "arbitrary")),
        name="stick_attention",
    )(proj, proj, proj, ntri)


def _out_router_kernel(*refs, n_parts):
    x_ref = refs[0]
    o_refs = refs[1:1 + n_parts]
    w_refs = refs[1 + n_parts:1 + 2 * n_parts]
    g_ref, wr_hi_ref, wr_lo_ref, br_ref, xo_ref, h_ref, route_ref = refs[1 + 2 * n_parts:]
    mix = _dot(o_refs[0][...], w_refs[0][...])
    for p in range(1, n_parts):
        mix = mix + _dot(o_refs[p][...], w_refs[p][...])
    x = x_ref[...] + mix
    xo_ref[...] = x
    h = _rms_rows(x, g_ref[...])
    h_ref[...] = h

    h_hi, h_lo = _split2(h)
    logits = (_dot(h_hi, wr_hi_ref[...]) + _dot(h_hi, wr_lo_ref[...]) + _dot(h_lo, wr_hi_ref[...])
              + br_ref[...])
    lane = lax.broadcasted_iota(jnp.int32, logits.shape, 1)

    def first_max(vals):
        top = jnp.max(vals, axis=-1, keepdims=True)
        idx = jnp.min(jnp.where(vals == top, lane, LANES), axis=-1, keepdims=True)
        return top, idx

    g_logits = jnp.where(lane < N_GROUPS, logits, NEG)
    g_top, g_idx = first_max(g_logits)
    g_w = 1.0 / jnp.sum(jnp.exp(g_logits - g_top), axis=-1, keepdims=True)
    lo_lane = N_GROUPS + EXPERTS_PER_GROUP * g_idx
    in_group = (lane >= lo_lane) & (lane < lo_lane + EXPERTS_PER_GROUP)
    e_logits = jnp.where(in_group, logits, NEG)
    v1, i1 = first_max(e_logits)
    v2, i2 = first_max(jnp.where(lane == i1, NEG, e_logits))
    e2 = jnp.exp(v2 - v1)
    w1 = g_w / (1.0 + e2)
    w2 = g_w * e2 / (1.0 + e2)
    route = jnp.where(lane == 0, (i1 - N_GROUPS).astype(F32),
                      jnp.where(lane == 1, (i2 - N_GROUPS).astype(F32),
                                jnp.where(lane == 2, w1, jnp.where(lane == 3, w2, 0.0))))
    route_ref[...] = route


def _out_router(x, o_parts, w_out, g_ffn, w_group, b_group, w_router, b_router):
    n, d = x.shape
    n_parts = len(o_parts)
    w_bf = w_out.astype(BF16)
    w_parts, r0 = [], 0
    for o in o_parts:
        w_parts.append(w_bf[r0:r0 + o.shape[1]])
        r0 += o.shape[1]
    wr = jnp.zeros((d, LANES), F32).at[:, :N_GROUPS].set(w_group)
    wr = wr.at[:, N_GROUPS:N_GROUPS + N_EXPERTS].set(w_router)
    wr_hi = wr.astype(BF16)
    wr_lo = (wr - wr_hi.astype(F32)).astype(BF16)
    br = jnp.zeros((1, LANES), F32).at[0, :N_GROUPS].set(b_group.astype(F32))
    br = br.at[0, N_GROUPS:N_GROUPS + N_EXPERTS].set(b_router.astype(F32))
    rows = lambda width: pl.BlockSpec((ROW_TILE, width), lambda i: (i, 0))
    const = lambda shape: pl.BlockSpec(shape, lambda i: (0, 0))
    return pl.pallas_call(
        functools.partial(_out_router_kernel, n_parts=n_parts),
        grid=(n // ROW_TILE,),
        in_specs=([rows(d)] + [rows(o.shape[1]) for o in o_parts]
                  + [const(w.shape) for w in w_parts]
                  + [const((1, d)), const((d, LANES)), const((d, LANES)), const((1, LANES))]),
        out_specs=[rows(d), rows(d), rows(LANES)],
        out_shape=[jax.ShapeDtypeStruct((n, d), F32), jax.ShapeDtypeStruct((n, d), F32),
                   jax.ShapeDtypeStruct((n, LANES), F32)],
        compiler_params=_cparams(("arbitrary",)),
        name="out_router",
    )(x, *o_parts, *w_parts, g_ffn.reshape(1, d), wr_hi, wr_lo, br)


def _dispatch_plan(route, n_blocks):
    n = route.shape[0]
    e_flat = route[:, 0:2].astype(jnp.int32).reshape(-1)
    w_flat = route[:, 2:4].reshape(-1)
    onehot = (e_flat[:, None] == jnp.arange(N_EXPERTS)[None, :]).astype(jnp.int32)
    csum = jnp.cumsum(onehot, axis=0)
    rank = jnp.take_along_axis(csum, e_flat[:, None], axis=1)[:, 0] - 1
    counts = csum[-1]
    padded = (counts + MOE_TILE - 1) // MOE_TILE * MOE_TILE
    pad_ends = jnp.cumsum(padded)
    dest = ((pad_ends - padded)[e_flat] + rank).astype(jnp.int32)
    p = n_blocks * MOE_TILE
    tok_buf = jnp.zeros((p,), jnp.int32).at[dest].set(jnp.arange(2 * n, dtype=jnp.int32) // 2)
    w_buf = jnp.zeros((p,), F32).at[dest].set(w_flat)
    blk_e = jnp.searchsorted(pad_ends, jnp.arange(n_blocks) * MOE_TILE, side='right')
    blk_e = jnp.minimum(blk_e, N_EXPERTS - 1).astype(jnp.int32)
    n_used = (pad_ends[-1] // MOE_TILE).astype(jnp.int32).reshape(1)
    return dest, tok_buf, w_buf, blk_e, n_used


def _moe_kernel(blk_e_ref, tok_ref, nused_ref, h_hbm, wrow_ref, wg_ref, wu_ref, wd_ref, y_ref,
                xbuf, sem):
    i = pl.program_id(0)
    n_used = nused_ref[0]
    slot = i % 2

    def gather_copy(blk, slot_, r):
        tok = tok_ref[blk * MOE_TILE + r]
        return pltpu.make_async_copy(h_hbm.at[pl.ds(tok, 1), :], xbuf.at[slot_, pl.ds(r, 1), :],
                                     sem.at[slot_])

    def start_gather(blk, slot_):
        def body(r, c):
            gather_copy(blk, slot_, r).start()
            return c
        lax.fori_loop(0, MOE_TILE, body, 0, unroll=8)

    @pl.when((i == 0) & (n_used > 0))
    def _():
        start_gather(0, 0)

    @pl.when(i + 1 < n_used)
    def _():
        start_gather(i + 1, 1 - slot)

    @pl.when(i < n_used)
    def _():
        def body(r, c):
            gather_copy(i, slot, r).wait()
            return c
        lax.fori_loop(0, MOE_TILE, body, 0, unroll=8)
        xb = xbuf[slot].astype(BF16)
        gate = _dot(xb, wg_ref[...])
        up = _dot(xb, wu_ref[...])
        act = (gate * jax.nn.sigmoid(gate) * up).astype(BF16)
        y_ref[...] = _dot(act, wd_ref[...]) * wrow_ref[...]

    @pl.when(i >= n_used)
    def _():
        y_ref[...] = jnp.zeros_like(y_ref)


def _moe_experts(h, tok_buf, w_buf, blk_e, n_used, w_gate, w_up, w_down, n_blocks):
    n, d = h.shape
    f = w_gate.shape[-1]
    p = n_blocks * MOE_TILE
    return pl.pallas_call(
        _moe_kernel,
        grid_spec=pltpu.PrefetchScalarGridSpec(
            num_scalar_prefetch=3,
            grid=(n_blocks,),
            in_specs=[pl.BlockSpec(memory_space=pl.ANY),
                      pl.BlockSpec((MOE_TILE, 1), lambda i, be, tk, nu: (i, 0)),
                      pl.BlockSpec((None, d, f), lambda i, be, tk, nu: (be[i], 0, 0)),
                      pl.BlockSpec((None, d, f), lambda i, be, tk, nu: (be[i], 0, 0)),
                      pl.BlockSpec((None, f, d), lambda i, be, tk, nu: (be[i], 0, 0))],
            out_specs=pl.BlockSpec((MOE_TILE, d), lambda i, be, tk, nu: (i, 0)),
            scratch_shapes=[pltpu.VMEM((2, MOE_TILE, d), F32), pltpu.SemaphoreType.DMA((2,))]),
        out_shape=jax.ShapeDtypeStruct((p, d), F32),
        compiler_params=_cparams(("arbitrary",)),
        name="moe_experts",
    )(blk_e, tok_buf, n_used, h, w_buf.reshape(p, 1), w_gate, w_up, w_down)


def _combine_kernel(dest_ref, x_ref, y_hbm, o_ref, ybuf, sem):
    i = pl.program_id(0)
    nsteps = pl.num_programs(0)
    slot = i % 2

    def row_copy(step, slot_, r, k):
        src = dest_ref[(step * CMB_TILE + r) * 2 + k]
        return pltpu.make_async_copy(y_hbm.at[pl.ds(src, 1), :], ybuf.at[slot_, k, pl.ds(r, 1), :],
                                     sem.at[slot_])

    def start_gather(step, slot_):
        def body(r, c):
            row_copy(step, slot_, r, 0).start()
            row_copy(step, slot_, r, 1).start()
            return c
        lax.fori_loop(0, CMB_TILE, body, 0, unroll=4)

    @pl.when(i == 0)
    def _():
        start_gather(0, 0)

    @pl.when(i + 1 < nsteps)
    def _():
        start_gather(i + 1, 1 - slot)

    def wait_body(r, c):
        row_copy(i, slot, r, 0).wait()
        row_copy(i, slot, r, 1).wait()
        return c
    lax.fori_loop(0, CMB_TILE, wait_body, 0, unroll=4)
    o_ref[...] = x_ref[...] + (ybuf[slot, 0] + ybuf[slot, 1])


def _combine(x, y_buf, dest):
    n, d = x.shape
    return pl.pallas_call(
        _combine_kernel,
        grid_spec=pltpu.PrefetchScalarGridSpec(
            num_scalar_prefetch=1,
            grid=(n // CMB_TILE,),
            in_specs=[pl.BlockSpec((CMB_TILE, d), lambda i, ds: (i, 0)),
                      pl.BlockSpec(memory_space=pl.ANY)],
            out_specs=pl.BlockSpec((CMB_TILE, d), lambda i, ds: (i, 0)),
            scratch_shapes=[pltpu.VMEM((2, 2, CMB_TILE, d), F32), pltpu.SemaphoreType.DMA((2,))]),
        out_shape=jax.ShapeDtypeStruct((n, d), F32),
        compiler_params=_cparams(("arbitrary",)),
        name="moe_combine",
    )(dest, x, y_buf)


def kernel(x, norm_mix_g, norm_ffn_g, ab_w_in, a_q_gain, a_k_gain, a_lam_q1, a_lam_k1, a_lam_q2,
           a_lam_k2, a_subln_g, ab_w_out, c_w_in, c_b_f, c_q_gain, c_k_gain, c_w_out, moe_w_group,
           moe_b_group, moe_w_router, moe_b_router, moe_w_gate, moe_w_up, moe_w_down):
    batch, seq, d = x.shape
    n = batch * seq
    depth = norm_mix_g.shape[0]
    heads = d // HEAD_DIM
    n_blocks = (2 * n) // MOE_TILE + N_EXPERTS
    xs = x.reshape(n, d).astype(F32)
    for layer in range(depth):
        i = layer // 2
        if layer % 2 == 0:
            proj = _even_in_proj(xs, norm_mix_g[layer], ab_w_in[i], a_q_gain[i], a_k_gain[i])
            lam_params = jnp.stack([a_lam_q1[i], a_lam_k1[i], a_lam_q2[i], a_lam_k2[i]])
            o_a = _diff_attention(proj, lam_params, a_subln_g[i], batch, seq, layer)
            o_b = _stick_attention(proj, batch, seq, 3 * A_HEADS)
            o_parts, w_out = [o_a, o_b], ab_w_out[i]
        else:
            qkv, cum = _odd_in_proj(xs, norm_mix_g[layer], c_w_in[i], c_b_f[i], c_q_gain[i],
                                    c_k_gain[i], seq)
            cum_t = cum[:, :heads].reshape(batch, seq, heads // 2, 2).transpose(0, 2, 3, 1)
            o_parts, w_out = [_fox_attention(qkv, cum_t, batch, seq)], c_w_out[i]
        xs, h, route = _out_router(xs, o_parts, w_out, norm_ffn_g[layer], moe_w_group[layer],
                                   moe_b_group[layer], moe_w_router[layer], moe_b_router[layer])
        dest, tok_buf, w_buf, blk_e, n_used = _dispatch_plan(route, n_blocks)
        y_buf = _moe_experts(h, tok_buf, w_buf, blk_e, n_used, moe_w_gate[layer].astype(BF16),
                             moe_w_up[layer].astype(BF16), moe_w_down[layer].astype(BF16), n_blocks)
        xs = _combine(xs, y_buf, dest)
    return xs.reshape(batch, seq, d).astype(x.dtype)
```

```python
import functools
import math

import jax
import jax.numpy as jnp
from jax import lax
from jax.experimental import pallas as pl
from jax.experimental.pallas import tpu as pltpu

F32 = jnp.float32
BF16 = jnp.bfloat16

HEAD_DIM = 64
LANES = 128
EPS = 1e-6
SCALE = HEAD_DIM ** -0.5
N_GROUPS = 4
EXPERTS_PER_GROUP = 4
N_EXPERTS = N_GROUPS * EXPERTS_PER_GROUP
A_HEADS = 4
NEG = -1e30
LOG2E = math.log2(math.e)

ROW_TILE = 512
ATT_TILE = 512
SB_Q_TILE = 512
SB_K_TILE = 256
EXP_ZERO_BELOW = -104.0
MOE_TILE = 256
CMB_TILE = 256
SM_ROWS = 32
SEC = 512
VMEM_LIMIT = 48 * 1024 * 1024


def _cparams(sem):
    return pltpu.CompilerParams(dimension_semantics=sem, vmem_limit_bytes=VMEM_LIMIT)


def _dot(a, b):
    return jnp.dot(a, b, preferred_element_type=F32)


def _dot_nt(a, b):
    return lax.dot_general(a, b, (((1,), (1,)), ((), ())), preferred_element_type=F32)


def _split2(x):
    hi = x.astype(BF16)
    lo = (x - hi.astype(F32)).astype(BF16)
    return hi, lo


def _rms_rows(x, g):
    return x * lax.rsqrt(jnp.mean(x * x, axis=-1, keepdims=True) + EPS) * g


def _group_rsqrt(acc, e_ref, et_ref):
    hi, lo = _split2(acc * acc)
    ss = _dot(hi, e_ref[...]) + _dot(lo, e_ref[...])
    rhi, rlo = _split2(lax.rsqrt(ss * (1.0 / HEAD_DIM) + EPS))
    return _dot(rhi, et_ref[...]) + _dot(rlo, et_ref[...])


def _softplus(z):
    return jnp.maximum(z, 0.0) + jnp.log(1.0 + jnp.exp(-jnp.abs(z)))


def _even_in_kernel(x_ref, g_ref, w_ref, gq_ref, gk_ref, e_ref, et_ref, o_ref):
    xn = _rms_rows(x_ref[...], g_ref[...]).astype(BF16)
    for sec in range(6):
        cols = slice(sec * SEC, (sec + 1) * SEC)
        acc = _dot(xn, w_ref[:, cols])
        if sec == 0:
            acc = acc * _group_rsqrt(acc, e_ref, et_ref) * gq_ref[...]
        elif sec == 1:
            acc = acc * _group_rsqrt(acc, e_ref, et_ref) * gk_ref[...]
        elif sec == 3:
            acc = acc * SCALE
        o_ref[:, cols] = acc.astype(BF16)


def _odd_in_kernel(x_ref, g_ref, w_ref, wf_ref, bf_ref, gq_ref, gk_ref, e_ref, et_ref, tri_ref,
                   o_ref, cum_ref, carry_ref, *, tiles_per_seq):
    i = pl.program_id(0)
    xn = _rms_rows(x_ref[...], g_ref[...]).astype(BF16)
    for sec in range(6):
        cols = slice(sec * SEC, (sec + 1) * SEC)
        acc = _dot(xn, w_ref[:, cols])
        if sec < 2:
            acc = acc * _group_rsqrt(acc, e_ref, et_ref) * gq_ref[...]
        elif sec < 4:
            acc = acc * _group_rsqrt(acc, e_ref, et_ref) * gk_ref[...]
        o_ref[:, cols] = acc.astype(BF16)

    log_f = -_softplus(-(_dot(xn, wf_ref[...]) + bf_ref[...]))
    p1 = log_f.astype(BF16)
    r1 = log_f - p1.astype(F32)
    p2 = r1.astype(BF16)
    p3 = (r1 - p2.astype(F32)).astype(BF16)
    tri = tri_ref[...]
    within = _dot(tri, p1) + _dot(tri, p2) + _dot(tri, p3)

    @pl.when(i % tiles_per_seq == 0)
    def _():
        carry_ref[...] = jnp.zeros_like(carry_ref)

    cum = within + carry_ref[...]
    cum_ref[...] = cum
    carry_ref[...] = cum[-1:, :]


def _group_maps():
    lane_grp = jnp.arange(SEC)[:, None] // HEAD_DIM
    e = (lane_grp == jnp.arange(LANES)[None, :]).astype(BF16)
    return e, e.T


def _even_in_proj(x, g, w_in, q_gain, k_gain):
    n, d = x.shape
    e, et = _group_maps()
    reps = SEC // HEAD_DIM
    gq = jnp.tile(q_gain.astype(F32) * (SCALE * LOG2E), reps).reshape(1, SEC)
    gk = jnp.tile(k_gain.astype(F32), reps).reshape(1, SEC)
    const = lambda shape: pl.BlockSpec(shape, lambda i: (0, 0))
    return pl.pallas_call(
        _even_in_kernel,
        grid=(n // ROW_TILE,),
        in_specs=[pl.BlockSpec((ROW_TILE, d), lambda i: (i, 0)), const((1, d)), const((d, 3 * d)),
                  const((1, SEC)), const((1, SEC)), const((SEC, LANES)), const((LANES, SEC))],
        out_specs=pl.BlockSpec((ROW_TILE, 3 * d), lambda i: (i, 0)),
        out_shape=jax.ShapeDtypeStruct((n, 3 * d), BF16),
        compiler_params=_cparams(("arbitrary",)),
        name="even_in_proj",
    )(x, g.reshape(1, d), w_in.astype(BF16), gq, gk, e, et)


def _odd_in_proj(x, g, w_in, b_f, q_gain, k_gain, seq):
    n, d = x.shape
    heads = d // HEAD_DIM
    e, et = _group_maps()
    reps = SEC // HEAD_DIM
    gq = jnp.tile(q_gain.astype(F32) * (SCALE * LOG2E), reps).reshape(1, SEC)
    gk = jnp.tile(k_gain.astype(F32), reps).reshape(1, SEC)
    wf = jnp.zeros((d, LANES), F32).at[:, :heads].set(w_in[:, 3 * d:]).astype(BF16)
    bf = jnp.zeros((1, LANES), F32).at[0, :heads].set(b_f.astype(F32))
    tri = (jnp.arange(ROW_TILE)[:, None] >= jnp.arange(ROW_TILE)[None, :]).astype(BF16)
    const = lambda shape: pl.BlockSpec(shape, lambda i: (0, 0))
    return pl.pallas_call(
        functools.partial(_odd_in_kernel, tiles_per_seq=seq // ROW_TILE),
        grid=(n // ROW_TILE,),
        in_specs=[pl.BlockSpec((ROW_TILE, d), lambda i: (i, 0)), const((1, d)), const((d, 3 * d)),
                  const((d, LANES)), const((1, LANES)), const((1, SEC)), const((1, SEC)),
                  const((SEC, LANES)), const((LANES, SEC)), const((ROW_TILE, ROW_TILE))],
        out_specs=[pl.BlockSpec((ROW_TILE, 3 * d), lambda i: (i, 0)),
                   pl.BlockSpec((ROW_TILE, LANES), lambda i: (i, 0))],
        out_shape=[jax.ShapeDtypeStruct((n, 3 * d), BF16), jax.ShapeDtypeStruct((n, LANES), F32)],
        scratch_shapes=[pltpu.VMEM((1, LANES), F32)],
        compiler_params=_cparams(("arbitrary",)),
        name="odd_in_proj",
    )(x, g.reshape(1, d), w_in[:, :3 * d].astype(BF16), wf, bf, gq, gk, e, et, tri)


def _flash_two_maps(q_ref, k_ref, v_ref, bias_fns, qi, kstart, tile, vaug_ref, m_refs, acc_refs):
    @pl.when(qi == 0)
    def _():
        vaug_ref[:, :LANES] = v_ref[...]
        vaug_ref[:, LANES:] = jnp.ones(v_ref.shape, BF16)

    q = q_ref[...]
    lane = lax.broadcasted_iota(jnp.int32, q.shape, 1)
    zero = jnp.zeros_like(q)
    qs = (jnp.where(lane < HEAD_DIM, q, zero), jnp.where(lane >= HEAD_DIM, q, zero))
    for mp in range(2):
        m_refs[mp][...] = jnp.full(m_refs[mp].shape, NEG, F32)
        acc_refs[mp][...] = jnp.zeros(acc_refs[mp].shape, F32)
    rel = (lax.broadcasted_iota(jnp.int32, (SM_ROWS, tile), 0)
           - lax.broadcasted_iota(jnp.int32, (SM_ROWS, tile), 1))

    def softmax_rows(s, brow, m_all, masked):
        width = s.shape[1]
        m_news, chunks = [], []
        for c in range(tile // SM_ROWS):
            rows = slice(c * SM_ROWS, (c + 1) * SM_ROWS)
            sc = s[rows] + brow
            if masked:
                sc = jnp.where(rel >= -c * SM_ROWS, sc, NEG)
            blocks = [sc[:, j * LANES:(j + 1) * LANES] for j in range(width // LANES)]
            top = functools.reduce(jnp.maximum, blocks)
            m_new = jnp.maximum(m_all[rows], jnp.max(top, axis=-1, keepdims=True))
            m_news.append(m_new)
            chunks.append(jnp.concatenate([jnp.exp2((b - m_new).astype(BF16)) for b in blocks], axis=1))
        return jnp.concatenate(m_news, axis=0), jnp.concatenate(chunks, axis=0)

    def step(key_start, width, masked):
        key_start = pl.multiple_of(key_start, tile)
        k = k_ref[pl.ds(key_start, width), :]
        va = vaug_ref[pl.ds(key_start, width), :]
        s = [_dot_nt(qs[mp], k) for mp in range(2)]
        m_old = [m_refs[mp][...] for mp in range(2)]
        acc_old = [acc_refs[mp][...] for mp in range(2)]
        m_new, acc_new = [], []
        for mp in range(2):
            m_mp, p = softmax_rows(s[mp], bias_fns[mp](key_start, width), m_old[mp], masked)
            alpha = jnp.exp2(m_old[mp] - m_mp)
            pv = _dot(p, va)
            m_new.append(m_mp)
            acc_new.append(jnp.concatenate([alpha * acc_old[mp][:, :LANES] + pv[:, :LANES],
                                            alpha * acc_old[mp][:, LANES:] + pv[:, LANES:]], axis=1))
        for mp in range(2):
            m_refs[mp][...] = m_new[mp]
            acc_refs[mp][...] = acc_new[mp]

    n_plain = qi - kstart

    def pair_body(j, c):
        step((kstart + 2 * j) * tile, 2 * tile, False)
        return c

    lax.fori_loop(0, n_plain // 2, pair_body, 0)

    @pl.when(n_plain % 2 == 1)
    def _():
        step((qi - 1) * tile, tile, False)

    step(qi * tile, tile, True)


def _first_live_tile(qi, decay_to_tile, cutoff):
    def cond(kt):
        return (kt < qi) & (decay_to_tile(jnp.minimum(kt, qi - 1)) > cutoff)
    return lax.while_loop(cond, lambda kt: kt + 1, jnp.int32(0))


def _flash_scratch(seq, tile):
    return [pltpu.VMEM((seq, 2 * LANES), BF16), pltpu.VMEM((tile, LANES), F32), pltpu.VMEM((tile, LANES), F32),
            pltpu.VMEM((tile, 2 * LANES), F32), pltpu.VMEM((tile, 2 * LANES), F32)]


def _fox_kernel(ftab_ref, cut_ref, q_ref, k_ref, v_ref, f_ref, o_ref, vaug_ref, m0_ref, m1_ref, acc0_ref,
                acc1_ref, *, tile, nq):
    b = pl.program_id(0)
    hp = pl.program_id(1)
    qi = pl.program_id(2)
    qstart = pl.multiple_of(qi * tile, tile)

    def bias(mp):
        f_q0 = f_ref[mp:mp + 1, pl.ds(qstart, LANES)][:, 0:1]
        return lambda start, width: LOG2E * (f_q0 - f_ref[mp:mp + 1, pl.ds(start, width)])

    def decay(kt):
        base = ((b * pl.num_programs(1) + hp) * 2) * nq
        d0 = ftab_ref[base + kt + 1] - ftab_ref[base + qi]
        d1 = ftab_ref[base + nq + kt + 1] - ftab_ref[base + nq + qi]
        return jnp.minimum(d0, d1) * LOG2E

    kstart = _first_live_tile(qi, decay, cut_ref[0])
    _flash_two_maps(q_ref, k_ref, v_ref, (bias(0), bias(1)), qi, kstart, tile, vaug_ref, (m0_ref, m1_ref),
                    (acc0_ref, acc1_ref))
    acc0, acc1 = acc0_ref[...], acc1_ref[...]
    lane = lax.broadcasted_iota(jnp.int32, (tile, LANES), 1)
    o_ref[...] = jnp.where(lane < HEAD_DIM, acc0[:, :LANES] / acc0[:, LANES:],
                           acc1[:, :LANES] / acc1[:, LANES:]).astype(BF16)


def _zero_cutoff(q_gain, k_gain):
    bound = (HEAD_DIM * SCALE * LOG2E * 1.02) * jnp.max(jnp.abs(q_gain)) * jnp.max(jnp.abs(k_gain))
    return (150.0 + 2.0 * bound).astype(F32).reshape(1)


def _fox_attention(qkv, cum_t, cutoff, batch, seq):
    n, d3 = qkv.shape
    d = d3 // 3
    pairs = d // LANES
    tile = ATT_TILE
    nq = seq // tile
    ftab = cum_t[:, :, :, ::tile].reshape(-1)
    return pl.pallas_call(
        functools.partial(_fox_kernel, tile=tile, nq=nq),
        grid_spec=pltpu.PrefetchScalarGridSpec(
            num_scalar_prefetch=2,
            grid=(batch, pairs, nq),
            in_specs=[pl.BlockSpec((tile, LANES), lambda b, h, i, ft, ct: (b * nq + i, h)),
                      pl.BlockSpec((seq, LANES), lambda b, h, i, ft, ct: (b, pairs + h)),
                      pl.BlockSpec((seq, LANES), lambda b, h, i, ft, ct: (b, 2 * pairs + h)),
                      pl.BlockSpec((None, None, 2, seq), lambda b, h, i, ft, ct: (b, h, 0, 0))],
            out_specs=pl.BlockSpec((tile, LANES), lambda b, h, i, ft, ct: (b * nq + i, h)),
            scratch_shapes=_flash_scratch(seq, tile)),
        out_shape=jax.ShapeDtypeStruct((n, d), BF16),
        compiler_params=_cparams(("arbitrary", "arbitrary", "arbitrary")),
        name="fox_attention",
    )(ftab, cutoff, qkv, qkv, qkv, cum_t)


def _diff_kernel(slope_ref, cut_ref, q_ref, k_ref, v_ref, lam_ref, sub_ref, o_ref, vaug_ref, m0_ref, m1_ref,
                 acc0_ref, acc1_ref, *, tile, lam_init):
    h = pl.program_id(1)
    qi = pl.program_id(2)
    slope = slope_ref[h] * LOG2E
    qstart = qi * tile

    def bias(start, width):
        key_off = lax.broadcasted_iota(jnp.int32, (1, width), 1)
        return slope * (start - qstart + key_off).astype(F32)

    def decay(kt):
        return slope * (qstart - (kt + 1) * tile + 1).astype(F32)

    kstart = _first_live_tile(qi, decay, cut_ref[0])
    _flash_two_maps(q_ref, k_ref, v_ref, (bias, bias), qi, kstart, tile, vaug_ref, (m0_ref, m1_ref),
                    (acc0_ref, acc1_ref))
    acc0, acc1 = acc0_ref[...], acc1_ref[...]
    lp = lam_ref[...]
    lam = (jnp.exp(jnp.sum(lp[0:1] * lp[1:2], axis=-1, keepdims=True))
           - jnp.exp(jnp.sum(lp[2:3] * lp[3:4], axis=-1, keepdims=True)) + lam_init)
    o = acc0[:, :LANES] / acc0[:, LANES:] - lam * (acc1[:, :LANES] / acc1[:, LANES:])
    o_ref[...] = (_rms_rows(o, sub_ref[...]) * (1.0 - lam_init)).astype(BF16)


def _diff_attention(proj, lam_params, subln_g, cutoff, batch, seq, layer):
    n = proj.shape[0]
    tile = ATT_TILE
    nq = seq // tile
    lam_init = 0.8 - 0.6 * math.exp(-0.3 * layer)
    slopes = jnp.asarray([2.0 ** (-8.0 * (i + 1) / A_HEADS) for i in range(A_HEADS)], F32)
    imap = lambda f: (lambda b, h, i, s, c: f(b, h, i))
    return pl.pallas_call(
        functools.partial(_diff_kernel, tile=tile, lam_init=lam_init),
        grid_spec=pltpu.PrefetchScalarGridSpec(
            num_scalar_prefetch=2,
            grid=(batch, A_HEADS, nq),
            in_specs=[pl.BlockSpec((tile, LANES), imap(lambda b, h, i: (b * nq + i, h))),
                      pl.BlockSpec((seq, LANES), imap(lambda b, h, i: (b, A_HEADS + h))),
                      pl.BlockSpec((seq, LANES), imap(lambda b, h, i: (b, 2 * A_HEADS + h))),
                      pl.BlockSpec((4, HEAD_DIM), imap(lambda b, h, i: (0, 0))),
                      pl.BlockSpec((1, LANES), imap(lambda b, h, i: (0, 0)))],
            out_specs=pl.BlockSpec((tile, LANES), imap(lambda b, h, i: (b * nq + i, h))),
            scratch_shapes=_flash_scratch(seq, tile)),
        out_shape=jax.ShapeDtypeStruct((n, A_HEADS * LANES), BF16),
        compiler_params=_cparams(("arbitrary", "arbitrary", "arbitrary")),
        name="diff_attention",
    )(slopes, cutoff, proj, proj, proj, lam_params.astype(F32), subln_g.astype(F32).reshape(1, LANES))


def _stick_kernel(q_ref, k_ref, v_ref, ntri_ref, o_ref, *, tq, tk):
    qi = pl.program_id(2)
    ratio = tq // tk
    q = q_ref[...]
    lane = lax.broadcasted_iota(jnp.int32, q.shape, 1)
    zero = jnp.zeros_like(q)
    qs = (jnp.where(lane < HEAD_DIM, q, zero), jnp.where(lane >= HEAD_DIM, q, zero))
    rel = (lax.broadcasted_iota(jnp.int32, (tq, tk), 0)
           - lax.broadcasted_iota(jnp.int32, (tq, tk), 1))
    ntri = ntri_ref[...]

    def step(kt, carry, masked):
        start = pl.multiple_of(kt * tk, tk)
        k = k_ref[pl.ds(start, tk), :]
        v = v_ref[pl.ds(start, tk), :]
        if masked:
            before = rel > kt * tk - qi * tq
        new = []
        for hh in range(2):
            run, acc = carry[hh]
            z = _dot_nt(qs[hh], k)
            sp = _softplus(z)
            drop = jnp.where(before, sp, 0.0) if masked else sp
            hi, lo = _split2(drop)
            later = _dot(hi, ntri) + _dot(lo, ntri)
            w = jnp.exp((z - sp) + later + run)
            if masked:
                w = jnp.where(before, w, 0.0)
            acc = acc + _dot(w.astype(BF16), v)
            run = run - jnp.sum(drop, axis=-1, keepdims=True)
            new.append((run, acc))
        return tuple(new)

    init_one = (jnp.zeros((tq, 1), F32), jnp.zeros((tq, LANES), F32))
    carry = (init_one, init_one)
    for d in range(ratio - 1, -1, -1):
        carry = step(qi * ratio + d, carry, True)

    def live(c):
        top = jnp.maximum(jnp.max(c[0][0]), jnp.max(c[1][0]))
        return (top > EXP_ZERO_BELOW).astype(jnp.int32)

    def cond(state):
        kt, alive, _ = state
        return (kt >= 0) & (alive > 0)

    def body(state):
        kt, _, c = state
        c = step(kt, c, False)
        return kt - 1, live(c), c

    _, _, carry = lax.while_loop(cond, body, (qi * ratio - 1, live(carry), carry))
    o_ref[...] = jnp.where(lane < HEAD_DIM, carry[0][1], carry[1][1]).astype(BF16)


def _stick_attention(proj, batch, seq, col0):
    n = proj.shape[0]
    pairs = 4
    tq, tk = SB_Q_TILE, SB_K_TILE
    nq = seq // tq
    ntri = -(jnp.arange(tk)[:, None] > jnp.arange(tk)[None, :]).astype(BF16)
    return pl.pallas_call(
        functools.partial(_stick_kernel, tq=tq, tk=tk),
        grid=(batch, pairs, nq),
        in_specs=[pl.BlockSpec((tq, LANES), lambda b, h, i: (b * nq + i, col0 + h)),
                  pl.BlockSpec((seq, LANES), lambda b, h, i: (b, col0 + pairs + h)),
                  pl.BlockSpec((seq, LANES), lambda b, h, i: (b, col0 + 2 * pairs + h)),
                  pl.BlockSpec((tk, tk), lambda b, h, i: (0, 0))],
        out_specs=pl.BlockSpec((tq, LANES), lambda b, h, i: (b * nq + i, h)),
        out_shape=jax.ShapeDtypeStruct((n, pairs * LANES), BF16),
        compiler_params=_cparams(("arbitrary", "arbitrary", "arbitrary")),
        name="stick_attention",
    )(proj, proj, proj, ntri)


def _out_router_kernel(*refs, n_parts):
    x_ref = refs[0]
    o_refs = refs[1:1 + n_parts]
    w_refs = refs[1 + n_parts:1 + 2 * n_parts]
    g_ref, wr_hi_ref, wr_lo_ref, br_ref, xo_ref, h_ref, route_ref = refs[1 + 2 * n_parts:]
    mix = _dot(o_refs[0][...], w_refs[0][...])
    for p in range(1, n_parts):
        mix = mix + _dot(o_refs[p][...], w_refs[p][...])
    x = x_ref[...] + mix
    xo_ref[...] = x
    h = _rms_rows(x, g_ref[...])
    h_ref[...] = h

    h_hi, h_lo = _split2(h)
    logits = (_dot(h_hi, wr_hi_ref[...]) + _dot(h_hi, wr_lo_ref[...]) + _dot(h_lo, wr_hi_ref[...])
              + br_ref[...])
    lane = lax.broadcasted_iota(jnp.int32, logits.shape, 1)

    def first_max(vals):
        top = jnp.max(vals, axis=-1, keepdims=True)
        idx = jnp.min(jnp.where(vals == top, lane, LANES), axis=-1, keepdims=True)
        return top, idx

    g_logits = jnp.where(lane < N_GROUPS, logits, NEG)
    g_top, g_idx = first_max(g_logits)
    g_w = 1.0 / jnp.sum(jnp.exp(g_logits - g_top), axis=-1, keepdims=True)
    lo_lane = N_GROUPS + EXPERTS_PER_GROUP * g_idx
    in_group = (lane >= lo_lane) & (lane < lo_lane + EXPERTS_PER_GROUP)
    e_logits = jnp.where(in_group, logits, NEG)
    v1, i1 = first_max(e_logits)
    v2, i2 = first_max(jnp.where(lane == i1, NEG, e_logits))
    e2 = jnp.exp(v2 - v1)
    w1 = g_w / (1.0 + e2)
    w2 = g_w * e2 / (1.0 + e2)
    route = jnp.where(lane == 0, (i1 - N_GROUPS).astype(F32),
                      jnp.where(lane == 1, (i2 - N_GROUPS).astype(F32),
                                jnp.where(lane == 2, w1, jnp.where(lane == 3, w2, 0.0))))
    route_ref[...] = route


def _out_router(x, o_parts, w_out, g_ffn, w_group, b_group, w_router, b_router):
    n, d = x.shape
    n_parts = len(o_parts)
    w_bf = w_out.astype(BF16)
    w_parts, r0 = [], 0
    for o in o_parts:
        w_parts.append(w_bf[r0:r0 + o.shape[1]])
        r0 += o.shape[1]
    wr = jnp.zeros((d, LANES), F32).at[:, :N_GROUPS].set(w_group)
    wr = wr.at[:, N_GROUPS:N_GROUPS + N_EXPERTS].set(w_router)
    wr_hi = wr.astype(BF16)
    wr_lo = (wr - wr_hi.astype(F32)).astype(BF16)
    br = jnp.zeros((1, LANES), F32).at[0, :N_GROUPS].set(b_group.astype(F32))
    br = br.at[0, N_GROUPS:N_GROUPS + N_EXPERTS].set(b_router.astype(F32))
    rows = lambda width: pl.BlockSpec((ROW_TILE, width), lambda i: (i, 0))
    const = lambda shape: pl.BlockSpec(shape, lambda i: (0, 0))
    return pl.pallas_call(
        functools.partial(_out_router_kernel, n_parts=n_parts),
        grid=(n // ROW_TILE,),
        in_specs=([rows(d)] + [rows(o.shape[1]) for o in o_parts]
                  + [const(w.shape) for w in w_parts]
                  + [const((1, d)), const((d, LANES)), const((d, LANES)), const((1, LANES))]),
        out_specs=[rows(d), rows(d), rows(LANES)],
        out_shape=[jax.ShapeDtypeStruct((n, d), F32), jax.ShapeDtypeStruct((n, d), F32),
                   jax.ShapeDtypeStruct((n, LANES), F32)],
        compiler_params=_cparams(("arbitrary",)),
        name="out_router",
    )(x, *o_parts, *w_parts, g_ffn.reshape(1, d), wr_hi, wr_lo, br)


def _dispatch_plan(route, n_blocks):
    n = route.shape[0]
    e_flat = route[:, 0:2].astype(jnp.int32).reshape(-1)
    w_flat = route[:, 2:4].reshape(-1)
    onehot = (e_flat[:, None] == jnp.arange(N_EXPERTS)[None, :]).astype(jnp.int32)
    csum = jnp.cumsum(onehot, axis=0)
    rank = jnp.take_along_axis(csum, e_flat[:, None], axis=1)[:, 0] - 1
    counts = csum[-1]
    padded = (counts + MOE_TILE - 1) // MOE_TILE * MOE_TILE
    pad_ends = jnp.cumsum(padded)
    dest = ((pad_ends - padded)[e_flat] + rank).astype(jnp.int32)
    p = n_blocks * MOE_TILE
    tok_buf = jnp.zeros((p,), jnp.int32).at[dest].set(jnp.arange(2 * n, dtype=jnp.int32) // 2)
    w_buf = jnp.zeros((p,), F32).at[dest].set(w_flat)
    blk_e = jnp.searchsorted(pad_ends, jnp.arange(n_blocks) * MOE_TILE, side='right')
    blk_e = jnp.minimum(blk_e, N_EXPERTS - 1).astype(jnp.int32)
    n_used = (pad_ends[-1] // MOE_TILE).astype(jnp.int32).reshape(1)
    return dest, tok_buf, w_buf, blk_e, n_used


def _moe_kernel(blk_e_ref, tok_ref, nused_ref, h_hbm, wrow_ref, wg_ref, wu_ref, wd_ref, y_ref,
                xbuf, sem):
    i = pl.program_id(0)
    n_used = nused_ref[0]
    slot = i % 2

    def gather_copy(blk, slot_, r):
        tok = tok_ref[blk * MOE_TILE + r]
        return pltpu.make_async_copy(h_hbm.at[pl.ds(tok, 1), :], xbuf.at[slot_, pl.ds(r, 1), :],
                                     sem.at[slot_])

    def start_gather(blk, slot_):
        def body(r, c):
            gather_copy(blk, slot_, r).start()
            return c
        lax.fori_loop(0, MOE_TILE, body, 0, unroll=8)

    @pl.when((i == 0) & (n_used > 0))
    def _():
        start_gather(0, 0)

    @pl.when(i + 1 < n_used)
    def _():
        start_gather(i + 1, 1 - slot)

    @pl.when(i < n_used)
    def _():
        def body(r, c):
            gather_copy(i, slot, r).wait()
            return c
        lax.fori_loop(0, MOE_TILE, body, 0, unroll=8)
        xb = xbuf[slot].astype(BF16)
        gate = _dot(xb, wg_ref[...])
        up = _dot(xb, wu_ref[...])
        act = (gate * jax.nn.sigmoid(gate) * up).astype(BF16)
        y_ref[...] = _dot(act, wd_ref[...]) * wrow_ref[...]

    @pl.when(i >= n_used)
    def _():
        y_ref[...] = jnp.zeros_like(y_ref)


def _moe_experts(h, tok_buf, w_buf, blk_e, n_used, w_gate, w_up, w_down, n_blocks):
    n, d = h.shape
    f = w_gate.shape[-1]
    p = n_blocks * MOE_TILE
    return pl.pallas_call(
        _moe_kernel,
        grid_spec=pltpu.PrefetchScalarGridSpec(
            num_scalar_prefetch=3,
            grid=(n_blocks,),
            in_specs=[pl.BlockSpec(memory_space=pl.ANY),
                      pl.BlockSpec((MOE_TILE, 1), lambda i, be, tk, nu: (i, 0)),
                      pl.BlockSpec((None, d, f), lambda i, be, tk, nu: (be[i], 0, 0)),
                      pl.BlockSpec((None, d, f), lambda i, be, tk, nu: (be[i], 0, 0)),
                      pl.BlockSpec((None, f, d), lambda i, be, tk, nu: (be[i], 0, 0))],
            out_specs=pl.BlockSpec((MOE_TILE, d), lambda i, be, tk, nu: (i, 0)),
            scratch_shapes=[pltpu.VMEM((2, MOE_TILE, d), F32), pltpu.SemaphoreType.DMA((2,))]),
        out_shape=jax.ShapeDtypeStruct((p, d), F32),
        compiler_params=_cparams(("arbitrary",)),
        name="moe_experts",
    )(blk_e, tok_buf, n_used, h, w_buf.reshape(p, 1), w_gate, w_up, w_down)


def _combine_kernel(dest_ref, x_ref, y_hbm, o_ref, ybuf, sem):
    i = pl.program_id(0)
    nsteps = pl.num_programs(0)
    slot = i % 2

    def row_copy(step, slot_, r, k):
        src = dest_ref[(step * CMB_TILE + r) * 2 + k]
        return pltpu.make_async_copy(y_hbm.at[pl.ds(src, 1), :], ybuf.at[slot_, k, pl.ds(r, 1), :],
                                     sem.at[slot_])

    def start_gather(step, slot_):
        def body(r, c):
            row_copy(step, slot_, r, 0).start()
            row_copy(step, slot_, r, 1).start()
            return c
        lax.fori_loop(0, CMB_TILE, body, 0, unroll=4)

    @pl.when(i == 0)
    def _():
        start_gather(0, 0)

    @pl.when(i + 1 < nsteps)
    def _():
        start_gather(i + 1, 1 - slot)

    def wait_body(r, c):
        row_copy(i, slot, r, 0).wait()
        row_copy(i, slot, r, 1).wait()
        return c
    lax.fori_loop(0, CMB_TILE, wait_body, 0, unroll=4)
    o_ref[...] = x_ref[...] + (ybuf[slot, 0] + ybuf[slot, 1])


def _combine(x, y_buf, dest):
    n, d = x.shape
    return pl.pallas_call(
        _combine_kernel,
        grid_spec=pltpu.PrefetchScalarGridSpec(
            num_scalar_prefetch=1,
            grid=(n // CMB_TILE,),
            in_specs=[pl.BlockSpec((CMB_TILE, d), lambda i, ds: (i, 0)),
                      pl.BlockSpec(memory_space=pl.ANY)],
            out_specs=pl.BlockSpec((CMB_TILE, d), lambda i, ds: (i, 0)),
            scratch_shapes=[pltpu.VMEM((2, 2, CMB_TILE, d), F32), pltpu.SemaphoreType.DMA((2,))]),
        out_shape=jax.ShapeDtypeStruct((n, d), F32),
        compiler_params=_cparams(("arbitrary",)),
        name="moe_combine",
    )(dest, x, y_buf)


def kernel(x, norm_mix_g, norm_ffn_g, ab_w_in, a_q_gain, a_k_gain, a_lam_q1, a_lam_k1, a_lam_q2,
           a_lam_k2, a_subln_g, ab_w_out, c_w_in, c_b_f, c_q_gain, c_k_gain, c_w_out, moe_w_group,
           moe_b_group, moe_w_router, moe_b_router, moe_w_gate, moe_w_up, moe_w_down):
    batch, seq, d = x.shape
    n = batch * seq
    depth = norm_mix_g.shape[0]
    heads = d // HEAD_DIM
    n_blocks = (2 * n) // MOE_TILE + N_EXPERTS
    xs = x.reshape(n, d).astype(F32)
    for layer in range(depth):
        i = layer // 2
        if layer % 2 == 0:
            proj = _even_in_proj(xs, norm_mix_g[layer], ab_w_in[i], a_q_gain[i], a_k_gain[i])
            lam_params = jnp.stack([a_lam_q1[i], a_lam_k1[i], a_lam_q2[i], a_lam_k2[i]])
            o_a = _diff_attention(proj, lam_params, a_subln_g[i], _zero_cutoff(a_q_gain[i], a_k_gain[i]),
                                  batch, seq, layer)
            o_b = _stick_attention(proj, batch, seq, 3 * A_HEADS)
            o_parts, w_out = [o_a, o_b], ab_w_out[i]
        else:
            qkv, cum = _odd_in_proj(xs, norm_mix_g[layer], c_w_in[i], c_b_f[i], c_q_gain[i],
                                    c_k_gain[i], seq)
            cum_t = cum[:, :heads].reshape(batch, seq, heads // 2, 2).transpose(0, 2, 3, 1)
            cutoff = _zero_cutoff(c_q_gain[i], c_k_gain[i])
            o_parts, w_out = [_fox_attention(qkv, cum_t, cutoff, batch, seq)], c_w_out[i]
        xs, h, route = _out_router(xs, o_parts, w_out, norm_ffn_g[layer], moe_w_group[layer],
                                   moe_b_group[layer], moe_w_router[layer], moe_b_router[layer])
        dest, tok_buf, w_buf, blk_e, n_used = _dispatch_plan(route, n_blocks)
        y_buf = _moe_experts(h, tok_buf, w_buf, blk_e, n_used, moe_w_gate[layer].astype(BF16),
                             moe_w_up[layer].astype(BF16), moe_w_down[layer].astype(BF16), n_blocks)
        xs = _combine(xs, y_buf, dest)
    return xs.reshape(batch, seq, d).astype(x.dtype)
```

```python
import functools
import math

import jax
import jax.numpy as jnp
from jax import lax
from jax.experimental import pallas as pl
from jax.experimental.pallas import tpu as pltpu

F32 = jnp.float32
BF16 = jnp.bfloat16

HEAD_DIM = 64
LANES = 128
EPS = 1e-6
SCALE = HEAD_DIM ** -0.5
N_GROUPS = 4
EXPERTS_PER_GROUP = 4
N_EXPERTS = N_GROUPS * EXPERTS_PER_GROUP
A_HEADS = 4
NEG = -1e30
LOG2E = math.log2(math.e)

ROW_TILE = 512
ATT_TILE = 512
SB_Q_TILE = 512
SB_K_TILE = 256
EXP_ZERO_BELOW = -104.0
MOE_TILE = 256
CMB_TILE = 256
SM_ROWS = 32
SEC = 512
VMEM_LIMIT = 48 * 1024 * 1024


def _cparams(sem):
    return pltpu.CompilerParams(dimension_semantics=sem, vmem_limit_bytes=VMEM_LIMIT)


def _dot(a, b):
    return jnp.dot(a, b, preferred_element_type=F32)


def _dot_nt(a, b):
    return lax.dot_general(a, b, (((1,), (1,)), ((), ())), preferred_element_type=F32)


def _split2(x):
    hi = x.astype(BF16)
    lo = (x - hi.astype(F32)).astype(BF16)
    return hi, lo


def _rms_rows(x, g):
    return x * lax.rsqrt(jnp.mean(x * x, axis=-1, keepdims=True) + EPS) * g


def _group_rsqrt(acc, e_ref, et_ref):
    hi, lo = _split2(acc * acc)
    ss = _dot(hi, e_ref[...]) + _dot(lo, e_ref[...])
    rhi, rlo = _split2(lax.rsqrt(ss * (1.0 / HEAD_DIM) + EPS))
    return _dot(rhi, et_ref[...]) + _dot(rlo, et_ref[...])


def _softplus(z):
    return jnp.maximum(z, 0.0) + jnp.log(1.0 + jnp.exp(-jnp.abs(z)))


def _even_in_kernel(x_ref, g_ref, w_ref, gq_ref, gk_ref, e_ref, et_ref, o_ref):
    xn = _rms_rows(x_ref[...], g_ref[...]).astype(BF16)
    for sec in range(6):
        cols = slice(sec * SEC, (sec + 1) * SEC)
        acc = _dot(xn, w_ref[:, cols])
        if sec == 0:
            acc = acc * _group_rsqrt(acc, e_ref, et_ref) * gq_ref[...]
        elif sec == 1:
            acc = acc * _group_rsqrt(acc, e_ref, et_ref) * gk_ref[...]
        elif sec == 3:
            acc = acc * SCALE
        o_ref[:, cols] = acc.astype(BF16)


def _odd_in_kernel(x_ref, g_ref, w_ref, wf_ref, bf_ref, gq_ref, gk_ref, e_ref, et_ref, tri_ref,
                   o_ref, cum_ref, carry_ref, *, tiles_per_seq):
    i = pl.program_id(0)
    xn = _rms_rows(x_ref[...], g_ref[...]).astype(BF16)
    for sec in range(6):
        cols = slice(sec * SEC, (sec + 1) * SEC)
        acc = _dot(xn, w_ref[:, cols])
        if sec < 2:
            acc = acc * _group_rsqrt(acc, e_ref, et_ref) * gq_ref[...]
        elif sec < 4:
            acc = acc * _group_rsqrt(acc, e_ref, et_ref) * gk_ref[...]
        o_ref[:, cols] = acc.astype(BF16)

    log_f = -_softplus(-(_dot(xn, wf_ref[...]) + bf_ref[...]))
    p1 = log_f.astype(BF16)
    r1 = log_f - p1.astype(F32)
    p2 = r1.astype(BF16)
    p3 = (r1 - p2.astype(F32)).astype(BF16)
    tri = tri_ref[...]
    within = _dot(tri, p1) + _dot(tri, p2) + _dot(tri, p3)

    @pl.when(i % tiles_per_seq == 0)
    def _():
        carry_ref[...] = jnp.zeros_like(carry_ref)

    cum = within + carry_ref[...]
    cum_ref[...] = cum
    carry_ref[...] = cum[-1:, :]


def _group_maps():
    lane_grp = jnp.arange(SEC)[:, None] // HEAD_DIM
    e = (lane_grp == jnp.arange(LANES)[None, :]).astype(BF16)
    return e, e.T


def _even_in_proj(x, g, w_in, q_gain, k_gain):
    n, d = x.shape
    e, et = _group_maps()
    reps = SEC // HEAD_DIM
    gq = jnp.tile(q_gain.astype(F32) * (SCALE * LOG2E), reps).reshape(1, SEC)
    gk = jnp.tile(k_gain.astype(F32), reps).reshape(1, SEC)
    const = lambda shape: pl.BlockSpec(shape, lambda i: (0, 0))
    return pl.pallas_call(
        _even_in_kernel,
        grid=(n // ROW_TILE,),
        in_specs=[pl.BlockSpec((ROW_TILE, d), lambda i: (i, 0)), const((1, d)), const((d, 3 * d)),
                  const((1, SEC)), const((1, SEC)), const((SEC, LANES)), const((LANES, SEC))],
        out_specs=pl.BlockSpec((ROW_TILE, 3 * d), lambda i: (i, 0)),
        out_shape=jax.ShapeDtypeStruct((n, 3 * d), BF16),
        compiler_params=_cparams(("arbitrary",)),
        name="even_in_proj",
    )(x, g.reshape(1, d), w_in.astype(BF16), gq, gk, e, et)


def _odd_in_proj(x, g, w_in, b_f, q_gain, k_gain, seq):
    n, d = x.shape
    heads = d // HEAD_DIM
    e, et = _group_maps()
    reps = SEC // HEAD_DIM
    gq = jnp.tile(q_gain.astype(F32) * (SCALE * LOG2E), reps).reshape(1, SEC)
    gk = jnp.tile(k_gain.astype(F32), reps).reshape(1, SEC)
    wf = jnp.zeros((d, LANES), F32).at[:, :heads].set(w_in[:, 3 * d:]).astype(BF16)
    bf = jnp.zeros((1, LANES), F32).at[0, :heads].set(b_f.astype(F32))
    tri = (jnp.arange(ROW_TILE)[:, None] >= jnp.arange(ROW_TILE)[None, :]).astype(BF16)
    const = lambda shape: pl.BlockSpec(shape, lambda i: (0, 0))
    return pl.pallas_call(
        functools.partial(_odd_in_kernel, tiles_per_seq=seq // ROW_TILE),
        grid=(n // ROW_TILE,),
        in_specs=[pl.BlockSpec((ROW_TILE, d), lambda i: (i, 0)), const((1, d)), const((d, 3 * d)),
                  const((d, LANES)), const((1, LANES)), const((1, SEC)), const((1, SEC)),
                  const((SEC, LANES)), const((LANES, SEC)), const((ROW_TILE, ROW_TILE))],
        out_specs=[pl.BlockSpec((ROW_TILE, 3 * d), lambda i: (i, 0)),
                   pl.BlockSpec((ROW_TILE, LANES), lambda i: (i, 0))],
        out_shape=[jax.ShapeDtypeStruct((n, 3 * d), BF16), jax.ShapeDtypeStruct((n, LANES), F32)],
        scratch_shapes=[pltpu.VMEM((1, LANES), F32)],
        compiler_params=_cparams(("arbitrary",)),
        name="odd_in_proj",
    )(x, g.reshape(1, d), w_in[:, :3 * d].astype(BF16), wf, bf, gq, gk, e, et, tri)


def _flash_two_maps(q_ref, k_ref, v_ref, bias_fns, qi, kstart, tile, vaug_ref, m_refs, acc_refs):
    @pl.when(qi == 0)
    def _():
        vaug_ref[:, :LANES] = v_ref[...]
        vaug_ref[:, LANES:] = jnp.ones(v_ref.shape, BF16)

    q = q_ref[...]
    lane = lax.broadcasted_iota(jnp.int32, q.shape, 1)
    zero = jnp.zeros_like(q)
    qs = (jnp.where(lane < HEAD_DIM, q, zero), jnp.where(lane >= HEAD_DIM, q, zero))
    for mp in range(2):
        m_refs[mp][...] = jnp.full(m_refs[mp].shape, NEG, F32)
        acc_refs[mp][...] = jnp.zeros(acc_refs[mp].shape, F32)
    rel = (lax.broadcasted_iota(jnp.int32, (SM_ROWS, tile), 0)
           - lax.broadcasted_iota(jnp.int32, (SM_ROWS, tile), 1))

    def softmax_rows(s, brow, m_all, masked):
        width = s.shape[1]
        m_news, chunks = [], []
        for c in range(tile // SM_ROWS):
            rows = slice(c * SM_ROWS, (c + 1) * SM_ROWS)
            sc = s[rows] + brow
            if masked:
                sc = jnp.where(rel >= -c * SM_ROWS, sc, NEG)
            blocks = [sc[:, j * LANES:(j + 1) * LANES] for j in range(width // LANES)]
            top = functools.reduce(jnp.maximum, blocks)
            m_new = jnp.maximum(m_all[rows], jnp.max(top, axis=-1, keepdims=True))
            m_news.append(m_new)
            chunks.append(jnp.concatenate([jnp.exp2((b - m_new).astype(BF16)) for b in blocks], axis=1))
        return jnp.concatenate(m_news, axis=0), jnp.concatenate(chunks, axis=0)

    def step(key_start, width, masked):
        key_start = pl.multiple_of(key_start, tile)
        k = k_ref[pl.ds(key_start, width), :]
        va = vaug_ref[pl.ds(key_start, width), :]
        s = [_dot_nt(qs[mp], k) for mp in range(2)]
        m_old = [m_refs[mp][...] for mp in range(2)]
        acc_old = [acc_refs[mp][...] for mp in range(2)]
        m_new, acc_new = [], []
        for mp in range(2):
            m_mp, p = softmax_rows(s[mp], bias_fns[mp](key_start, width), m_old[mp], masked)
            alpha = jnp.exp2(m_old[mp] - m_mp)
            pv = _dot(p, va)
            m_new.append(m_mp)
            acc_new.append(jnp.concatenate([alpha * acc_old[mp][:, :LANES] + pv[:, :LANES],
                                            alpha * acc_old[mp][:, LANES:] + pv[:, LANES:]], axis=1))
        for mp in range(2):
            m_refs[mp][...] = m_new[mp]
            acc_refs[mp][...] = acc_new[mp]

    n_plain = qi - kstart

    def pair_body(j, c):
        step((kstart + 2 * j) * tile, 2 * tile, False)
        return c

    lax.fori_loop(0, n_plain // 2, pair_body, 0)

    @pl.when(n_plain % 2 == 1)
    def _():
        step((qi - 1) * tile, tile, False)

    step(qi * tile, tile, True)


def _first_live_tile(qi, decay_to_tile, cutoff):
    def cond(kt):
        return (kt < qi) & (decay_to_tile(jnp.minimum(kt, qi - 1)) > cutoff)
    return lax.while_loop(cond, lambda kt: kt + 1, jnp.int32(0))


def _flash_scratch(seq, tile):
    return [pltpu.VMEM((seq, 2 * LANES), BF16), pltpu.VMEM((tile, LANES), F32), pltpu.VMEM((tile, LANES), F32),
            pltpu.VMEM((tile, 2 * LANES), F32), pltpu.VMEM((tile, 2 * LANES), F32)]


def _fox_kernel(ftab_ref, cut_ref, q_ref, k_ref, v_ref, f_ref, o_ref, vaug_ref, m0_ref, m1_ref, acc0_ref,
                acc1_ref, *, tile, nq):
    b = pl.program_id(0)
    hp = pl.program_id(1)
    qi = pl.program_id(2)
    qstart = pl.multiple_of(qi * tile, tile)

    def bias(mp):
        f_q0 = f_ref[mp:mp + 1, pl.ds(qstart, LANES)][:, 0:1]
        return lambda start, width: LOG2E * (f_q0 - f_ref[mp:mp + 1, pl.ds(start, width)])

    def decay(kt):
        base = ((b * pl.num_programs(1) + hp) * 2) * nq
        d0 = ftab_ref[base + kt + 1] - ftab_ref[base + qi]
        d1 = ftab_ref[base + nq + kt + 1] - ftab_ref[base + nq + qi]
        return jnp.minimum(d0, d1) * LOG2E

    kstart = _first_live_tile(qi, decay, cut_ref[0])
    _flash_two_maps(q_ref, k_ref, v_ref, (bias(0), bias(1)), qi, kstart, tile, vaug_ref, (m0_ref, m1_ref),
                    (acc0_ref, acc1_ref))
    acc0, acc1 = acc0_ref[...], acc1_ref[...]
    lane = lax.broadcasted_iota(jnp.int32, (tile, LANES), 1)
    o_ref[...] = jnp.where(lane < HEAD_DIM, acc0[:, :LANES] / acc0[:, LANES:],
                           acc1[:, :LANES] / acc1[:, LANES:]).astype(BF16)


def _zero_cutoff(q_gain, k_gain):
    bound = (HEAD_DIM * SCALE * LOG2E * 1.02) * jnp.max(jnp.abs(q_gain)) * jnp.max(jnp.abs(k_gain))
    return (150.0 + 2.0 * bound).astype(F32).reshape(1)


def _fox_attention(qkv, cum_t, cutoff, batch, seq):
    n, d3 = qkv.shape
    d = d3 // 3
    pairs = d // LANES
    tile = ATT_TILE
    nq = seq // tile
    ftab = cum_t[:, :, :, ::tile].reshape(-1)
    return pl.pallas_call(
        functools.partial(_fox_kernel, tile=tile, nq=nq),
        grid_spec=pltpu.PrefetchScalarGridSpec(
            num_scalar_prefetch=2,
            grid=(batch, pairs, nq),
            in_specs=[pl.BlockSpec((tile, LANES), lambda b, h, i, ft, ct: (b * nq + i, h)),
                      pl.BlockSpec((seq, LANES), lambda b, h, i, ft, ct: (b, pairs + h)),
                      pl.BlockSpec((seq, LANES), lambda b, h, i, ft, ct: (b, 2 * pairs + h)),
                      pl.BlockSpec((None, None, 2, seq), lambda b, h, i, ft, ct: (b, h, 0, 0))],
            out_specs=pl.BlockSpec((tile, LANES), lambda b, h, i, ft, ct: (b * nq + i, h)),
            scratch_shapes=_flash_scratch(seq, tile)),
        out_shape=jax.ShapeDtypeStruct((n, d), BF16),
        compiler_params=_cparams(("arbitrary", "arbitrary", "arbitrary")),
        name="fox_attention",
    )(ftab, cutoff, qkv, qkv, qkv, cum_t)


def _diff_kernel(slope_ref, cut_ref, q_ref, k_ref, v_ref, lam_ref, sub_ref, o_ref, vaug_ref, m0_ref, m1_ref,
                 acc0_ref, acc1_ref, *, tile, lam_init):
    h = pl.program_id(1)
    qi = pl.program_id(2)
    slope = slope_ref[h] * LOG2E
    qstart = qi * tile

    def bias(start, width):
        key_off = lax.broadcasted_iota(jnp.int32, (1, width), 1)
        return slope * (start - qstart + key_off).astype(F32)

    def decay(kt):
        return slope * (qstart - (kt + 1) * tile + 1).astype(F32)

    kstart = _first_live_tile(qi, decay, cut_ref[0])
    _flash_two_maps(q_ref, k_ref, v_ref, (bias, bias), qi, kstart, tile, vaug_ref, (m0_ref, m1_ref),
                    (acc0_ref, acc1_ref))
    acc0, acc1 = acc0_ref[...], acc1_ref[...]
    lp = lam_ref[...]
    lam = (jnp.exp(jnp.sum(lp[0:1] * lp[1:2], axis=-1, keepdims=True))
           - jnp.exp(jnp.sum(lp[2:3] * lp[3:4], axis=-1, keepdims=True)) + lam_init)
    o = acc0[:, :LANES] / acc0[:, LANES:] - lam * (acc1[:, :LANES] / acc1[:, LANES:])
    o_ref[...] = (_rms_rows(o, sub_ref[...]) * (1.0 - lam_init)).astype(BF16)


def _diff_attention(proj, lam_params, subln_g, cutoff, batch, seq, layer):
    n = proj.shape[0]
    tile = ATT_TILE
    nq = seq // tile
    lam_init = 0.8 - 0.6 * math.exp(-0.3 * layer)
    slopes = jnp.asarray([2.0 ** (-8.0 * (i + 1) / A_HEADS) for i in range(A_HEADS)], F32)
    imap = lambda f: (lambda b, h, i, s, c: f(b, h, i))
    return pl.pallas_call(
        functools.partial(_diff_kernel, tile=tile, lam_init=lam_init),
        grid_spec=pltpu.PrefetchScalarGridSpec(
            num_scalar_prefetch=2,
            grid=(batch, A_HEADS, nq),
            in_specs=[pl.BlockSpec((tile, LANES), imap(lambda b, h, i: (b * nq + i, h))),
                      pl.BlockSpec((seq, LANES), imap(lambda b, h, i: (b, A_HEADS + h))),
                      pl.BlockSpec((seq, LANES), imap(lambda b, h, i: (b, 2 * A_HEADS + h))),
                      pl.BlockSpec((4, HEAD_DIM), imap(lambda b, h, i: (0, 0))),
                      pl.BlockSpec((1, LANES), imap(lambda b, h, i: (0, 0)))],
            out_specs=pl.BlockSpec((tile, LANES), imap(lambda b, h, i: (b * nq + i, h))),
            scratch_shapes=_flash_scratch(seq, tile)),
        out_shape=jax.ShapeDtypeStruct((n, A_HEADS * LANES), BF16),
        compiler_params=_cparams(("arbitrary", "arbitrary", "arbitrary")),
        name="diff_attention",
    )(slopes, cutoff, proj, proj, proj, lam_params.astype(F32), subln_g.astype(F32).reshape(1, LANES))


def _stick_kernel(q_ref, k_ref, v_ref, ntri_ref, o_ref, *, tq, tk):
    qi = pl.program_id(2)
    ratio = tq // tk
    q = q_ref[...]
    lane = lax.broadcasted_iota(jnp.int32, q.shape, 1)
    zero = jnp.zeros_like(q)
    qs = (jnp.where(lane < HEAD_DIM, q, zero), jnp.where(lane >= HEAD_DIM, q, zero))
    rel = (lax.broadcasted_iota(jnp.int32, (tq, tk), 0)
           - lax.broadcasted_iota(jnp.int32, (tq, tk), 1))
    ntri = ntri_ref[...]

    def step(kt, carry, masked):
        start = pl.multiple_of(kt * tk, tk)
        k = k_ref[pl.ds(start, tk), :]
        v = v_ref[pl.ds(start, tk), :]
        if masked:
            before = rel > kt * tk - qi * tq
        new = []
        for hh in range(2):
            run, acc = carry[hh]
            z = _dot_nt(qs[hh], k)
            sp = _softplus(z)
            drop = jnp.where(before, sp, 0.0) if masked else sp
            hi, lo = _split2(drop)
            later = _dot(hi, ntri) + _dot(lo, ntri)
            w = jnp.exp((z - sp) + later + run)
            if masked:
                w = jnp.where(before, w, 0.0)
            acc = acc + _dot(w.astype(BF16), v)
            run = run - jnp.sum(drop, axis=-1, keepdims=True)
            new.append((run, acc))
        return tuple(new)

    init_one = (jnp.zeros((tq, 1), F32), jnp.zeros((tq, LANES), F32))
    carry = (init_one, init_one)
    for d in range(ratio - 1, -1, -1):
        carry = step(qi * ratio + d, carry, True)

    def live(c):
        top = jnp.maximum(jnp.max(c[0][0]), jnp.max(c[1][0]))
        return (top > EXP_ZERO_BELOW).astype(jnp.int32)

    def cond(state):
        kt, alive, _ = state
        return (kt >= 0) & (alive > 0)

    def body(state):
        kt, _, c = state
        c = step(kt, c, False)
        return kt - 1, live(c), c

    _, _, carry = lax.while_loop(cond, body, (qi * ratio - 1, live(carry), carry))
    o_ref[...] = jnp.where(lane < HEAD_DIM, carry[0][1], carry[1][1]).astype(BF16)


def _stick_attention(proj, batch, seq, col0):
    n = proj.shape[0]
    pairs = 4
    tq, tk = SB_Q_TILE, SB_K_TILE
    nq = seq // tq
    ntri = -(jnp.arange(tk)[:, None] > jnp.arange(tk)[None, :]).astype(BF16)
    return pl.pallas_call(
        functools.partial(_stick_kernel, tq=tq, tk=tk),
        grid=(batch, pairs, nq),
        in_specs=[pl.BlockSpec((tq, LANES), lambda b, h, i: (b * nq + i, col0 + h)),
                  pl.BlockSpec((seq, LANES), lambda b, h, i: (b, col0 + pairs + h)),
                  pl.BlockSpec((seq, LANES), lambda b, h, i: (b, col0 + 2 * pairs + h)),
                  pl.BlockSpec((tk, tk), lambda b, h, i: (0, 0))],
        out_specs=pl.BlockSpec((tq, LANES), lambda b, h, i: (b * nq + i, h)),
        out_shape=jax.ShapeDtypeStruct((n, pairs * LANES), BF16),
        compiler_params=_cparams(("arbitrary", "arbitrary", "arbitrary")),
        name="stick_attention",
    )(proj, proj, proj, ntri)


def _out_router_kernel(*refs, n_parts):
    x_ref = refs[0]
    o_refs = refs[1:1 + n_parts]
    w_refs = refs[1 + n_parts:1 + 2 * n_parts]
    (g_ref, wr_hi_ref, wr_lo_ref, br_ref, ltri_ref, xo_ref, h_ref, route_ref, count_ref,
     seen_ref) = refs[1 + 2 * n_parts:]
    mix = _dot(o_refs[0][...], w_refs[0][...])
    for p in range(1, n_parts):
        mix = mix + _dot(o_refs[p][...], w_refs[p][...])
    x = x_ref[...] + mix
    xo_ref[...] = x
    h = _rms_rows(x, g_ref[...])
    h_ref[...] = h

    h_hi, h_lo = _split2(h)
    logits = (_dot(h_hi, wr_hi_ref[...]) + _dot(h_hi, wr_lo_ref[...]) + _dot(h_lo, wr_hi_ref[...])
              + br_ref[...])
    lane = lax.broadcasted_iota(jnp.int32, logits.shape, 1)

    def first_max(vals):
        top = jnp.max(vals, axis=-1, keepdims=True)
        idx = jnp.min(jnp.where(vals == top, lane, LANES), axis=-1, keepdims=True)
        return top, idx

    g_logits = jnp.where(lane < N_GROUPS, logits, NEG)
    g_top, g_idx = first_max(g_logits)
    g_w = 1.0 / jnp.sum(jnp.exp(g_logits - g_top), axis=-1, keepdims=True)
    lo_lane = N_GROUPS + EXPERTS_PER_GROUP * g_idx
    in_group = (lane >= lo_lane) & (lane < lo_lane + EXPERTS_PER_GROUP)
    e_logits = jnp.where(in_group, logits, NEG)
    v1, i1 = first_max(e_logits)
    v2, i2 = first_max(jnp.where(lane == i1, NEG, e_logits))
    e2 = jnp.exp(v2 - v1)
    w1 = g_w / (1.0 + e2)
    w2 = g_w * e2 / (1.0 + e2)

    @pl.when(pl.program_id(0) == 0)
    def _():
        seen_ref[...] = jnp.zeros_like(seen_ref)

    uses = ((lane == i1) | (lane == i2)).astype(BF16)
    earlier = _dot(ltri_ref[...], uses) + seen_ref[...]
    r1 = jnp.sum(jnp.where(lane == i1, earlier, 0.0), axis=-1, keepdims=True)
    r2 = jnp.sum(jnp.where(lane == i2, earlier, 0.0), axis=-1, keepdims=True)
    seen = earlier[-1:, :] + uses[-1:, :].astype(F32)
    seen_ref[...] = seen
    count_ref[...] = jnp.broadcast_to(seen, count_ref.shape)

    cols = ((i1 - N_GROUPS).astype(F32), (i2 - N_GROUPS).astype(F32), w1, w2, r1, r2)
    route = jnp.zeros(logits.shape, F32)
    for c, val in enumerate(cols):
        route = jnp.where(lane == c, val, route)
    route_ref[...] = route


def _out_router(x, o_parts, w_out, g_ffn, w_group, b_group, w_router, b_router):
    n, d = x.shape
    n_parts = len(o_parts)
    w_bf = w_out.astype(BF16)
    w_parts, r0 = [], 0
    for o in o_parts:
        w_parts.append(w_bf[r0:r0 + o.shape[1]])
        r0 += o.shape[1]
    wr = jnp.zeros((d, LANES), F32).at[:, :N_GROUPS].set(w_group)
    wr = wr.at[:, N_GROUPS:N_GROUPS + N_EXPERTS].set(w_router)
    wr_hi = wr.astype(BF16)
    wr_lo = (wr - wr_hi.astype(F32)).astype(BF16)
    br = jnp.zeros((1, LANES), F32).at[0, :N_GROUPS].set(b_group.astype(F32))
    br = br.at[0, N_GROUPS:N_GROUPS + N_EXPERTS].set(b_router.astype(F32))
    ltri = (jnp.arange(ROW_TILE)[:, None] > jnp.arange(ROW_TILE)[None, :]).astype(BF16)
    rows = lambda width: pl.BlockSpec((ROW_TILE, width), lambda i: (i, 0))
    const = lambda shape: pl.BlockSpec(shape, lambda i: (0, 0))
    return pl.pallas_call(
        functools.partial(_out_router_kernel, n_parts=n_parts),
        grid=(n // ROW_TILE,),
        in_specs=([rows(d)] + [rows(o.shape[1]) for o in o_parts]
                  + [const(w.shape) for w in w_parts]
                  + [const((1, d)), const((d, LANES)), const((d, LANES)), const((1, LANES)),
                     const((ROW_TILE, ROW_TILE))]),
        out_specs=[rows(d), rows(d), rows(LANES), const((8, LANES))],
        out_shape=[jax.ShapeDtypeStruct((n, d), F32), jax.ShapeDtypeStruct((n, d), F32),
                   jax.ShapeDtypeStruct((n, LANES), F32), jax.ShapeDtypeStruct((8, LANES), F32)],
        scratch_shapes=[pltpu.VMEM((1, LANES), F32)],
        compiler_params=_cparams(("arbitrary",)),
        name="out_router",
    )(x, *o_parts, *w_parts, g_ffn.reshape(1, d), wr_hi, wr_lo, br, ltri)


def _dispatch_plan(route, counts_row, n_blocks):
    n = route.shape[0]
    e_flat = route[:, 0:2].astype(jnp.int32).reshape(-1)
    rank = route[:, 4:6].astype(jnp.int32).reshape(-1)
    counts = counts_row[N_GROUPS:N_GROUPS + N_EXPERTS].astype(jnp.int32)
    padded = (counts + MOE_TILE - 1) // MOE_TILE * MOE_TILE
    pad_ends = jnp.cumsum(padded)
    dest = ((pad_ends - padded)[e_flat] + rank).astype(jnp.int32)
    p = n_blocks * MOE_TILE
    tok_buf = jnp.zeros((p,), jnp.int32).at[dest].set(jnp.arange(2 * n, dtype=jnp.int32) // 2)
    blk_e = jnp.searchsorted(pad_ends, jnp.arange(n_blocks) * MOE_TILE, side='right')
    blk_e = jnp.minimum(blk_e, N_EXPERTS - 1).astype(jnp.int32)
    return dest, tok_buf, blk_e


def _moe_kernel(blk_e_ref, tok_ref, h_hbm, wg_ref, wu_ref, wd_ref, y_ref, xbuf, wg_bf, wu_bf, wd_bf, sem):
    i = pl.program_id(0)
    last = pl.num_programs(0) - 1
    slot = i % 2

    def row_copy(blk, slot_, r):
        tok = tok_ref[blk * MOE_TILE + r]
        return pltpu.make_async_copy(h_hbm.at[pl.ds(tok, 1), :], xbuf.at[slot_, pl.ds(r, 1), :],
                                     sem.at[slot_])

    def wait_rows(slot_):
        def body(r, c):
            pltpu.make_async_copy(h_hbm.at[pl.ds(0, 1), :], xbuf.at[slot_, pl.ds(r, 1), :],
                                  sem.at[slot_]).wait()
            return c
        lax.fori_loop(0, MOE_TILE, body, 0, unroll=8)

    @pl.when(i == 0)
    def _():
        for r in range(MOE_TILE):
            row_copy(0, 0, r).start()

    @pl.when((i == 0) | (blk_e_ref[i] != blk_e_ref[jnp.maximum(i - 1, 0)]))
    def _():
        wg_bf[...] = wg_ref[...].astype(BF16)
        wu_bf[...] = wu_ref[...].astype(BF16)
        wd_bf[...] = wd_ref[...].astype(BF16)

    wait_rows(slot)
    xb = xbuf[slot].astype(BF16)
    nxt = jnp.minimum(i + 1, last)
    for r in range(MOE_TILE):
        row_copy(nxt, 1 - slot, r).start()
    gate = _dot(xb, wg_bf[...])
    up = _dot(xb, wu_bf[...])
    act = (gate * jax.nn.sigmoid(gate) * up).astype(BF16)
    y_ref[...] = _dot(act, wd_bf[...])

    @pl.when(i == last)
    def _():
        wait_rows(1 - slot)


def _moe_experts(h, tok_buf, blk_e, w_gate, w_up, w_down, layer, n_blocks):
    n, d = h.shape
    f = w_gate.shape[-1]
    p = n_blocks * MOE_TILE
    return pl.pallas_call(
        _moe_kernel,
        grid_spec=pltpu.PrefetchScalarGridSpec(
            num_scalar_prefetch=2,
            grid=(n_blocks,),
            in_specs=[pl.BlockSpec(memory_space=pl.ANY),
                      pl.BlockSpec((None, None, d, f), lambda i, be, tk: (layer, be[i], 0, 0)),
                      pl.BlockSpec((None, None, d, f), lambda i, be, tk: (layer, be[i], 0, 0)),
                      pl.BlockSpec((None, None, f, d), lambda i, be, tk: (layer, be[i], 0, 0))],
            out_specs=pl.BlockSpec((MOE_TILE, d), lambda i, be, tk: (i, 0)),
            scratch_shapes=[pltpu.VMEM((2, MOE_TILE, d), F32), pltpu.VMEM((d, f), BF16),
                            pltpu.VMEM((d, f), BF16), pltpu.VMEM((f, d), BF16),
                            pltpu.SemaphoreType.DMA((2,))]),
        out_shape=jax.ShapeDtypeStruct((p, d), F32),
        compiler_params=_cparams(("arbitrary",)),
        name="moe_experts",
    )(blk_e, tok_buf, h, w_gate, w_up, w_down)


def _combine_kernel(dest_ref, x_ref, route_ref, y_hbm, o_ref, ybuf, sem):
    i = pl.program_id(0)
    nsteps = pl.num_programs(0)
    slot = i % 2

    def start_gather(step, slot_):
        def body(r, c):
            for k in range(2):
                src = dest_ref[(step * CMB_TILE + r) * 2 + k]
                pltpu.make_async_copy(y_hbm.at[pl.ds(src, 1), :], ybuf.at[slot_, k, pl.ds(r, 1), :],
                                      sem.at[slot_]).start()
            return c
        lax.fori_loop(0, CMB_TILE, body, 0, unroll=4)

    @pl.when(i == 0)
    def _():
        start_gather(0, 0)

    @pl.when(i + 1 < nsteps)
    def _():
        start_gather(i + 1, 1 - slot)

    def wait_body(r, c):
        for k in range(2):
            pltpu.make_async_copy(y_hbm.at[pl.ds(0, 1), :], ybuf.at[slot, k, pl.ds(r, 1), :],
                                  sem.at[slot]).wait()
        return c
    lax.fori_loop(0, CMB_TILE, wait_body, 0, unroll=4)
    route = route_ref[...]
    o_ref[...] = x_ref[...] + (ybuf[slot, 0] * route[:, 2:3] + ybuf[slot, 1] * route[:, 3:4])


def _combine(x, y_buf, dest, route):
    n, d = x.shape
    return pl.pallas_call(
        _combine_kernel,
        grid_spec=pltpu.PrefetchScalarGridSpec(
            num_scalar_prefetch=1,
            grid=(n // CMB_TILE,),
            in_specs=[pl.BlockSpec((CMB_TILE, d), lambda i, ds: (i, 0)),
                      pl.BlockSpec((CMB_TILE, LANES), lambda i, ds: (i, 0)),
                      pl.BlockSpec(memory_space=pl.ANY)],
            out_specs=pl.BlockSpec((CMB_TILE, d), lambda i, ds: (i, 0)),
            scratch_shapes=[pltpu.VMEM((2, 2, CMB_TILE, d), F32), pltpu.SemaphoreType.DMA((2,))]),
        out_shape=jax.ShapeDtypeStruct((n, d), F32),
        compiler_params=_cparams(("arbitrary",)),
        name="moe_combine",
    )(dest, x, route, y_buf)


def kernel(x, norm_mix_g, norm_ffn_g, ab_w_in, a_q_gain, a_k_gain, a_lam_q1, a_lam_k1, a_lam_q2,
           a_lam_k2, a_subln_g, ab_w_out, c_w_in, c_b_f, c_q_gain, c_k_gain, c_w_out, moe_w_group,
           moe_b_group, moe_w_router, moe_b_router, moe_w_gate, moe_w_up, moe_w_down):
    batch, seq, d = x.shape
    n = batch * seq
    depth = norm_mix_g.shape[0]
    heads = d // HEAD_DIM
    n_blocks = (2 * n) // MOE_TILE + N_EXPERTS
    xs = x.reshape(n, d).astype(F32)
    for layer in range(depth):
        i = layer // 2
        if layer % 2 == 0:
            proj = _even_in_proj(xs, norm_mix_g[layer], ab_w_in[i], a_q_gain[i], a_k_gain[i])
            lam_params = jnp.stack([a_lam_q1[i], a_lam_k1[i], a_lam_q2[i], a_lam_k2[i]])
            o_a = _diff_attention(proj, lam_params, a_subln_g[i], _zero_cutoff(a_q_gain[i], a_k_gain[i]),
                                  batch, seq, layer)
            o_b = _stick_attention(proj, batch, seq, 3 * A_HEADS)
            o_parts, w_out = [o_a, o_b], ab_w_out[i]
        else:
            qkv, cum = _odd_in_proj(xs, norm_mix_g[layer], c_w_in[i], c_b_f[i], c_q_gain[i],
                                    c_k_gain[i], seq)
            cum_t = cum[:, :heads].reshape(batch, seq, heads // 2, 2).transpose(0, 2, 3, 1)
            cutoff = _zero_cutoff(c_q_gain[i], c_k_gain[i])
            o_parts, w_out = [_fox_attention(qkv, cum_t, cutoff, batch, seq)], c_w_out[i]
        xs, h, route, counts = _out_router(xs, o_parts, w_out, norm_ffn_g[layer], moe_w_group[layer],
                                           moe_b_group[layer], moe_w_router[layer], moe_b_router[layer])
        dest, tok_buf, blk_e = _dispatch_plan(route, counts[0], n_blocks)
        y_buf = _moe_experts(h, tok_buf, blk_e, moe_w_gate, moe_w_up, moe_w_down, layer, n_blocks)
        xs = _combine(xs, y_buf, dest, route)
    return xs.reshape(batch, seq, d).astype(x.dtype)
```

```python
import functools
import math

import jax
import jax.numpy as jnp
from jax import lax
from jax.experimental import pallas as pl
from jax.experimental.pallas import tpu as pltpu

F32 = jnp.float32
BF16 = jnp.bfloat16

HEAD_DIM = 64
LANES = 128
EPS = 1e-6
SCALE = HEAD_DIM ** -0.5
N_GROUPS = 4
EXPERTS_PER_GROUP = 4
N_EXPERTS = N_GROUPS * EXPERTS_PER_GROUP
A_HEADS = 4
NEG = -1e30
LOG2E = math.log2(math.e)

ROW_TILE = 512
ATT_TILE = 512
SB_Q_TILE = 512
SB_K_TILE = 256
EXP_ZERO_BELOW = -104.0
MOE_TILE = 256
MOE_SLOTS = 3
CMB_TILE = 256
SM_ROWS = 32
SEC = 512
VMEM_LIMIT = 48 * 1024 * 1024


def _cparams(sem):
    return pltpu.CompilerParams(dimension_semantics=sem, vmem_limit_bytes=VMEM_LIMIT)


def _dot(a, b):
    return jnp.dot(a, b, preferred_element_type=F32)


def _dot_nt(a, b):
    return lax.dot_general(a, b, (((1,), (1,)), ((), ())), preferred_element_type=F32)


def _split2(x):
    hi = x.astype(BF16)
    lo = (x - hi.astype(F32)).astype(BF16)
    return hi, lo


def _rms_rows(x, g):
    return x * lax.rsqrt(jnp.mean(x * x, axis=-1, keepdims=True) + EPS) * g


def _group_rsqrt(acc, e_ref, et_ref):
    hi, lo = _split2(acc * acc)
    ss = _dot(hi, e_ref[...]) + _dot(lo, e_ref[...])
    rhi, rlo = _split2(lax.rsqrt(ss * (1.0 / HEAD_DIM) + EPS))
    return _dot(rhi, et_ref[...]) + _dot(rlo, et_ref[...])


def _softplus(z):
    return jnp.maximum(z, 0.0) + jnp.log(1.0 + jnp.exp(-jnp.abs(z)))


def _even_in_kernel(x_ref, g_ref, w_ref, gq_ref, gk_ref, e_ref, et_ref, o_ref):
    xn = _rms_rows(x_ref[...], g_ref[...]).astype(BF16)
    for sec in range(6):
        cols = slice(sec * SEC, (sec + 1) * SEC)
        acc = _dot(xn, w_ref[:, cols])
        if sec == 0:
            acc = acc * _group_rsqrt(acc, e_ref, et_ref) * gq_ref[...]
        elif sec == 1:
            acc = acc * _group_rsqrt(acc, e_ref, et_ref) * gk_ref[...]
        elif sec == 3:
            acc = acc * SCALE
        o_ref[:, cols] = acc.astype(BF16)


def _odd_in_kernel(x_ref, g_ref, w_ref, wf_ref, bf_ref, gq_ref, gk_ref, e_ref, et_ref, tri_ref,
                   o_ref, cum_ref, carry_ref, *, tiles_per_seq):
    i = pl.program_id(0)
    xn = _rms_rows(x_ref[...], g_ref[...]).astype(BF16)
    for sec in range(6):
        cols = slice(sec * SEC, (sec + 1) * SEC)
        acc = _dot(xn, w_ref[:, cols])
        if sec < 2:
            acc = acc * _group_rsqrt(acc, e_ref, et_ref) * gq_ref[...]
        elif sec < 4:
            acc = acc * _group_rsqrt(acc, e_ref, et_ref) * gk_ref[...]
        o_ref[:, cols] = acc.astype(BF16)

    log_f = -_softplus(-(_dot(xn, wf_ref[...]) + bf_ref[...]))
    p1 = log_f.astype(BF16)
    r1 = log_f - p1.astype(F32)
    p2 = r1.astype(BF16)
    p3 = (r1 - p2.astype(F32)).astype(BF16)
    tri = tri_ref[...]
    within = _dot(tri, p1) + _dot(tri, p2) + _dot(tri, p3)

    @pl.when(i % tiles_per_seq == 0)
    def _():
        carry_ref[...] = jnp.zeros_like(carry_ref)

    cum = within + carry_ref[...]
    cum_ref[...] = cum
    carry_ref[...] = cum[-1:, :]


def _group_maps():
    lane_grp = jnp.arange(SEC)[:, None] // HEAD_DIM
    e = (lane_grp == jnp.arange(LANES)[None, :]).astype(BF16)
    return e, e.T


def _even_in_proj(x, g, w_in, q_gain, k_gain):
    n, d = x.shape
    e, et = _group_maps()
    reps = SEC // HEAD_DIM
    gq = jnp.tile(q_gain.astype(F32) * (SCALE * LOG2E), reps).reshape(1, SEC)
    gk = jnp.tile(k_gain.astype(F32), reps).reshape(1, SEC)
    const = lambda shape: pl.BlockSpec(shape, lambda i: (0, 0))
    return pl.pallas_call(
        _even_in_kernel,
        grid=(n // ROW_TILE,),
        in_specs=[pl.BlockSpec((ROW_TILE, d), lambda i: (i, 0)), const((1, d)), const((d, 3 * d)),
                  const((1, SEC)), const((1, SEC)), const((SEC, LANES)), const((LANES, SEC))],
        out_specs=pl.BlockSpec((ROW_TILE, 3 * d), lambda i: (i, 0)),
        out_shape=jax.ShapeDtypeStruct((n, 3 * d), BF16),
        compiler_params=_cparams(("arbitrary",)),
        name="even_in_proj",
    )(x, g.reshape(1, d), w_in.astype(BF16), gq, gk, e, et)


def _odd_in_proj(x, g, w_in, b_f, q_gain, k_gain, seq):
    n, d = x.shape
    heads = d // HEAD_DIM
    e, et = _group_maps()
    reps = SEC // HEAD_DIM
    gq = jnp.tile(q_gain.astype(F32) * (SCALE * LOG2E), reps).reshape(1, SEC)
    gk = jnp.tile(k_gain.astype(F32), reps).reshape(1, SEC)
    wf = jnp.zeros((d, LANES), F32).at[:, :heads].set(w_in[:, 3 * d:]).astype(BF16)
    bf = jnp.zeros((1, LANES), F32).at[0, :heads].set(b_f.astype(F32))
    tri = (jnp.arange(ROW_TILE)[:, None] >= jnp.arange(ROW_TILE)[None, :]).astype(BF16)
    const = lambda shape: pl.BlockSpec(shape, lambda i: (0, 0))
    return pl.pallas_call(
        functools.partial(_odd_in_kernel, tiles_per_seq=seq // ROW_TILE),
        grid=(n // ROW_TILE,),
        in_specs=[pl.BlockSpec((ROW_TILE, d), lambda i: (i, 0)), const((1, d)), const((d, 3 * d)),
                  const((d, LANES)), const((1, LANES)), const((1, SEC)), const((1, SEC)),
                  const((SEC, LANES)), const((LANES, SEC)), const((ROW_TILE, ROW_TILE))],
        out_specs=[pl.BlockSpec((ROW_TILE, 3 * d), lambda i: (i, 0)),
                   pl.BlockSpec((ROW_TILE, LANES), lambda i: (i, 0))],
        out_shape=[jax.ShapeDtypeStruct((n, 3 * d), BF16), jax.ShapeDtypeStruct((n, LANES), F32)],
        scratch_shapes=[pltpu.VMEM((1, LANES), F32)],
        compiler_params=_cparams(("arbitrary",)),
        name="odd_in_proj",
    )(x, g.reshape(1, d), w_in[:, :3 * d].astype(BF16), wf, bf, gq, gk, e, et, tri)


def _flash_parts(q_parts, row0s, k_ref, vaug_ref, bias_fn, qi, kstart, tile, m_refs, acc_refs):
    n_parts = len(q_parts)
    for p in range(n_parts):
        m_refs[p][...] = jnp.full(m_refs[p].shape, NEG, F32)
        acc_refs[p][...] = jnp.zeros(acc_refs[p].shape, F32)
    rel = (lax.broadcasted_iota(jnp.int32, (SM_ROWS, tile), 0)
           - lax.broadcasted_iota(jnp.int32, (SM_ROWS, tile), 1))

    def softmax_rows(s, brow, m_all, diag_row0):
        rows_p, width = s.shape
        m_news, chunks = [], []
        for c in range(rows_p // SM_ROWS):
            rows = slice(c * SM_ROWS, (c + 1) * SM_ROWS)
            sc = s[rows] + brow
            if diag_row0 is not None:
                sc = jnp.where(rel >= -(diag_row0 + c * SM_ROWS), sc, NEG)
            blocks = [sc[:, j * LANES:(j + 1) * LANES] for j in range(width // LANES)]
            top = functools.reduce(jnp.maximum, blocks)
            m_new = jnp.maximum(m_all[rows], jnp.max(top, axis=-1, keepdims=True))
            m_news.append(m_new)
            chunks.append(jnp.concatenate([jnp.exp2((b - m_new).astype(BF16)) for b in blocks], axis=1))
        return jnp.concatenate(m_news, axis=0), jnp.concatenate(chunks, axis=0)

    def step(key_start, width, masked):
        key_start = pl.multiple_of(key_start, tile)
        k = k_ref[pl.ds(key_start, width), :]
        va = vaug_ref[pl.ds(key_start, width), :]
        brow = bias_fn(key_start, width)
        s = [_dot_nt(q_parts[p], k) for p in range(n_parts)]
        m_old = [m_refs[p][...] for p in range(n_parts)]
        acc_old = [acc_refs[p][...] for p in range(n_parts)]
        m_new, acc_new = [], []
        for p in range(n_parts):
            m_p, w = softmax_rows(s[p], brow, m_old[p], row0s[p] if masked else None)
            alpha = jnp.exp2(m_old[p] - m_p)
            pv = _dot(w, va)
            m_new.append(m_p)
            acc_new.append(jnp.concatenate([alpha * acc_old[p][:, :LANES] + pv[:, :LANES],
                                            alpha * acc_old[p][:, LANES:] + pv[:, LANES:]], axis=1))
        for p in range(n_parts):
            m_refs[p][...] = m_new[p]
            acc_refs[p][...] = acc_new[p]

    n_plain = qi - kstart

    def pair_body(j, c):
        step((kstart + 2 * j) * tile, 2 * tile, False)
        return c

    lax.fori_loop(0, n_plain // 2, pair_body, 0)

    @pl.when(n_plain % 2 == 1)
    def _():
        step((qi - 1) * tile, tile, False)

    step(qi * tile, tile, True)


def _fill_vaug(qi, v_ref, vaug_ref):
    @pl.when(qi == 0)
    def _():
        vaug_ref[:, :LANES] = v_ref[...]
        vaug_ref[:, LANES:] = jnp.ones(v_ref.shape, BF16)


def _first_live_tile(qi, decay_to_tile, cutoff):
    def cond(kt):
        return (kt < qi) & (decay_to_tile(jnp.minimum(kt, qi - 1)) > cutoff)
    return lax.while_loop(cond, lambda kt: kt + 1, jnp.int32(0))


def _flash_scratch(seq, rows):
    return [pltpu.VMEM((seq, 2 * LANES), BF16), pltpu.VMEM((rows, LANES), F32), pltpu.VMEM((rows, LANES), F32),
            pltpu.VMEM((rows, 2 * LANES), F32), pltpu.VMEM((rows, 2 * LANES), F32)]


def _fox_kernel(ftab_ref, cut_ref, q_ref, k_ref, v_ref, f_ref, o_ref, vaug_ref, m0_ref, m1_ref, acc0_ref,
                acc1_ref, *, tile, nq):
    b = pl.program_id(0)
    hp = pl.program_id(1)
    qi = pl.program_id(2)
    qstart = pl.multiple_of(qi * tile, tile)
    half = tile // 2
    _fill_vaug(qi, v_ref, vaug_ref)
    q = q_ref[...]
    lane = lax.broadcasted_iota(jnp.int32, q.shape, 1)
    cutoff = cut_ref[0]

    def one_head(hh, carry):
        in_head = (lane >= hh * HEAD_DIM) & (lane < (hh + 1) * HEAD_DIM)
        qh = jnp.where(in_head, q, jnp.zeros_like(q))
        lane_half = lax.broadcasted_iota(jnp.int32, (half, LANES), 1)
        in_head_half = (lane_half >= hh * HEAD_DIM) & (lane_half < (hh + 1) * HEAD_DIM)

        def f_row(start, width):
            first = jnp.full((1, width), hh, jnp.int32) == 0
            return jnp.where(first, f_ref[0:1, pl.ds(start, width)], f_ref[1:2, pl.ds(start, width)])

        f_q0 = f_row(qstart, LANES)[:, 0:1]

        def bias(start, width):
            return LOG2E * (f_q0 - f_row(start, width))

        base = ((b * pl.num_programs(1) + hp) * 2 + hh) * nq

        def decay(kt):
            return (ftab_ref[base + kt + 1] - ftab_ref[base + qi]) * LOG2E

        kstart = _first_live_tile(qi, decay, cutoff)
        _flash_parts((qh[:half], qh[half:]), (0, half), k_ref, vaug_ref, bias, qi, kstart, tile,
                     (m0_ref, m1_ref), (acc0_ref, acc1_ref))
        for p, acc_ref in enumerate((acc0_ref, acc1_ref)):
            rows = slice(p * half, (p + 1) * half)
            acc = acc_ref[...]
            o_h = acc[:, :LANES] / acc[:, LANES:]
            o_ref[rows, :] = jnp.where(in_head_half, o_h, o_ref[rows, :].astype(F32)).astype(BF16)
        return carry

    o_ref[...] = jnp.zeros_like(o_ref)
    lax.fori_loop(0, 2, one_head, 0)


def _zero_cutoff(q_gain, k_gain):
    bound = (HEAD_DIM * SCALE * LOG2E * 1.02) * jnp.max(jnp.abs(q_gain)) * jnp.max(jnp.abs(k_gain))
    return (150.0 + 2.0 * bound).astype(F32).reshape(1)


def _fox_attention(qkv, cum_t, cutoff, batch, seq):
    n, d3 = qkv.shape
    d = d3 // 3
    pairs = d // LANES
    tile = ATT_TILE
    nq = seq // tile
    ftab = cum_t[:, :, :, ::tile].reshape(-1)
    return pl.pallas_call(
        functools.partial(_fox_kernel, tile=tile, nq=nq),
        grid_spec=pltpu.PrefetchScalarGridSpec(
            num_scalar_prefetch=2,
            grid=(batch, pairs, nq),
            in_specs=[pl.BlockSpec((tile, LANES), lambda b, h, i, ft, ct: (b * nq + i, h)),
                      pl.BlockSpec((seq, LANES), lambda b, h, i, ft, ct: (b, pairs + h)),
                      pl.BlockSpec((seq, LANES), lambda b, h, i, ft, ct: (b, 2 * pairs + h)),
                      pl.BlockSpec((None, None, 2, seq), lambda b, h, i, ft, ct: (b, h, 0, 0))],
            out_specs=pl.BlockSpec((tile, LANES), lambda b, h, i, ft, ct: (b * nq + i, h)),
            scratch_shapes=_flash_scratch(seq, tile // 2)),
        out_shape=jax.ShapeDtypeStruct((n, d), BF16),
        compiler_params=_cparams(("arbitrary", "arbitrary", "arbitrary")),
        name="fox_attention",
    )(ftab, cutoff, qkv, qkv, qkv, cum_t)


def _diff_kernel(slope_ref, cut_ref, q_ref, k_ref, v_ref, lam_ref, sub_ref, o_ref, vaug_ref, m0_ref, m1_ref,
                 acc0_ref, acc1_ref, *, tile, lam_init):
    h = pl.program_id(1)
    qi = pl.program_id(2)
    slope = slope_ref[h] * LOG2E
    qstart = qi * tile

    def bias(start, width):
        key_off = lax.broadcasted_iota(jnp.int32, (1, width), 1)
        return slope * (start - qstart + key_off).astype(F32)

    def decay(kt):
        return slope * (qstart - (kt + 1) * tile + 1).astype(F32)

    kstart = _first_live_tile(qi, decay, cut_ref[0])
    _fill_vaug(qi, v_ref, vaug_ref)
    q = q_ref[...]
    lane = lax.broadcasted_iota(jnp.int32, q.shape, 1)
    zero = jnp.zeros_like(q)
    q_parts = (jnp.where(lane < HEAD_DIM, q, zero), jnp.where(lane >= HEAD_DIM, q, zero))
    _flash_parts(q_parts, (0, 0), k_ref, vaug_ref, bias, qi, kstart, tile, (m0_ref, m1_ref),
                 (acc0_ref, acc1_ref))
    acc0, acc1 = acc0_ref[...], acc1_ref[...]
    lp = lam_ref[...]
    lam = (jnp.exp(jnp.sum(lp[0:1] * lp[1:2], axis=-1, keepdims=True))
           - jnp.exp(jnp.sum(lp[2:3] * lp[3:4], axis=-1, keepdims=True)) + lam_init)
    o = acc0[:, :LANES] / acc0[:, LANES:] - lam * (acc1[:, :LANES] / acc1[:, LANES:])
    o_ref[...] = (_rms_rows(o, sub_ref[...]) * (1.0 - lam_init)).astype(BF16)


def _diff_attention(proj, lam_params, subln_g, cutoff, batch, seq, layer):
    n = proj.shape[0]
    tile = ATT_TILE
    nq = seq // tile
    lam_init = 0.8 - 0.6 * math.exp(-0.3 * layer)
    slopes = jnp.asarray([2.0 ** (-8.0 * (i + 1) / A_HEADS) for i in range(A_HEADS)], F32)
    imap = lambda f: (lambda b, h, i, s, c: f(b, h, i))
    return pl.pallas_call(
        functools.partial(_diff_kernel, tile=tile, lam_init=lam_init),
        grid_spec=pltpu.PrefetchScalarGridSpec(
            num_scalar_prefetch=2,
            grid=(batch, A_HEADS, nq),
            in_specs=[pl.BlockSpec((tile, LANES), imap(lambda b, h, i: (b * nq + i, h))),
                      pl.BlockSpec((seq, LANES), imap(lambda b, h, i: (b, A_HEADS + h))),
                      pl.BlockSpec((seq, LANES), imap(lambda b, h, i: (b, 2 * A_HEADS + h))),
                      pl.BlockSpec((4, HEAD_DIM), imap(lambda b, h, i: (0, 0))),
                      pl.BlockSpec((1, LANES), imap(lambda b, h, i: (0, 0)))],
            out_specs=pl.BlockSpec((tile, LANES), imap(lambda b, h, i: (b * nq + i, h))),
            scratch_shapes=_flash_scratch(seq, tile)),
        out_shape=jax.ShapeDtypeStruct((n, A_HEADS * LANES), BF16),
        compiler_params=_cparams(("arbitrary", "arbitrary", "arbitrary")),
        name="diff_attention",
    )(slopes, cutoff, proj, proj, proj, lam_params.astype(F32), subln_g.astype(F32).reshape(1, LANES))


def _stick_kernel(q_ref, k_ref, v_ref, ntri_ref, o_ref, *, tq, tk):
    qi = pl.program_id(2)
    ratio = tq // tk
    q = q_ref[...]
    lane = lax.broadcasted_iota(jnp.int32, q.shape, 1)
    zero = jnp.zeros_like(q)
    qs = (jnp.where(lane < HEAD_DIM, q, zero), jnp.where(lane >= HEAD_DIM, q, zero))
    rel = (lax.broadcasted_iota(jnp.int32, (tq, tk), 0)
           - lax.broadcasted_iota(jnp.int32, (tq, tk), 1))
    ntri = ntri_ref[...]

    def step(kt, carry, masked):
        start = pl.multiple_of(kt * tk, tk)
        k = k_ref[pl.ds(start, tk), :]
        v = v_ref[pl.ds(start, tk), :]
        if masked:
            before = rel > kt * tk - qi * tq
        new = []
        for hh in range(2):
            run, acc = carry[hh]
            z = _dot_nt(qs[hh], k)
            sp = _softplus(z)
            drop = jnp.where(before, sp, 0.0) if masked else sp
            hi, lo = _split2(drop)
            later = _dot(hi, ntri) + _dot(lo, ntri)
            w = jnp.exp((z - sp) + later + run)
            if masked:
                w = jnp.where(before, w, 0.0)
            acc = acc + _dot(w.astype(BF16), v)
            run = run - jnp.sum(drop, axis=-1, keepdims=True)
            new.append((run, acc))
        return tuple(new)

    init_one = (jnp.zeros((tq, 1), F32), jnp.zeros((tq, LANES), F32))
    carry = (init_one, init_one)
    for d in range(ratio - 1, -1, -1):
        carry = step(qi * ratio + d, carry, True)

    def live(c):
        top = jnp.maximum(jnp.max(c[0][0]), jnp.max(c[1][0]))
        return (top > EXP_ZERO_BELOW).astype(jnp.int32)

    def cond(state):
        kt, alive, _ = state
        return (kt >= 0) & (alive > 0)

    def body(state):
        kt, _, c = state
        c = step(kt, c, False)
        return kt - 1, live(c), c

    _, _, carry = lax.while_loop(cond, body, (qi * ratio - 1, live(carry), carry))
    o_ref[...] = jnp.where(lane < HEAD_DIM, carry[0][1], carry[1][1]).astype(BF16)


def _stick_attention(proj, batch, seq, col0):
    n = proj.shape[0]
    pairs = 4
    tq, tk = SB_Q_TILE, SB_K_TILE
    nq = seq // tq
    ntri = -(jnp.arange(tk)[:, None] > jnp.arange(tk)[None, :]).astype(BF16)
    return pl.pallas_call(
        functools.partial(_stick_kernel, tq=tq, tk=tk),
        grid=(batch, pairs, nq),
        in_specs=[pl.BlockSpec((tq, LANES), lambda b, h, i: (b * nq + i, col0 + h)),
                  pl.BlockSpec((seq, LANES), lambda b, h, i: (b, col0 + pairs + h)),
                  pl.BlockSpec((seq, LANES), lambda b, h, i: (b, col0 + 2 * pairs + h)),
                  pl.BlockSpec((tk, tk), lambda b, h, i: (0, 0))],
        out_specs=pl.BlockSpec((tq, LANES), lambda b, h, i: (b * nq + i, h)),
        out_shape=jax.ShapeDtypeStruct((n, pairs * LANES), BF16),
        compiler_params=_cparams(("arbitrary", "arbitrary", "arbitrary")),
        name="stick_attention",
    )(proj, proj, proj, ntri)


def _out_router_kernel(*refs, n_parts):
    x_ref = refs[0]
    o_refs = refs[1:1 + n_parts]
    w_refs = refs[1 + n_parts:1 + 2 * n_parts]
    (g_ref, wr_hi_ref, wr_lo_ref, br_ref, ltri_ref, xo_ref, h_ref, route_ref, count_ref,
     seen_ref) = refs[1 + 2 * n_parts:]
    mix = _dot(o_refs[0][...], w_refs[0][...])
    for p in range(1, n_parts):
        mix = mix + _dot(o_refs[p][...], w_refs[p][...])
    x = x_ref[...] + mix
    xo_ref[...] = x
    h = _rms_rows(x, g_ref[...])
    h_ref[...] = h

    h_hi, h_lo = _split2(h)
    logits = (_dot(h_hi, wr_hi_ref[...]) + _dot(h_hi, wr_lo_ref[...]) + _dot(h_lo, wr_hi_ref[...])
              + br_ref[...])
    lane = lax.broadcasted_iota(jnp.int32, logits.shape, 1)

    def first_max(vals):
        top = jnp.max(vals, axis=-1, keepdims=True)
        idx = jnp.min(jnp.where(vals == top, lane, LANES), axis=-1, keepdims=True)
        return top, idx

    g_logits = jnp.where(lane < N_GROUPS, logits, NEG)
    g_top, g_idx = first_max(g_logits)
    g_w = 1.0 / jnp.sum(jnp.exp(g_logits - g_top), axis=-1, keepdims=True)
    lo_lane = N_GROUPS + EXPERTS_PER_GROUP * g_idx
    in_group = (lane >= lo_lane) & (lane < lo_lane + EXPERTS_PER_GROUP)
    e_logits = jnp.where(in_group, logits, NEG)
    v1, i1 = first_max(e_logits)
    v2, i2 = first_max(jnp.where(lane == i1, NEG, e_logits))
    e2 = jnp.exp(v2 - v1)
    w1 = g_w / (1.0 + e2)
    w2 = g_w * e2 / (1.0 + e2)

    @pl.when(pl.program_id(0) == 0)
    def _():
        seen_ref[...] = jnp.zeros_like(seen_ref)

    uses = ((lane == i1) | (lane == i2)).astype(BF16)
    earlier = _dot(ltri_ref[...], uses) + seen_ref[...]
    r1 = jnp.sum(jnp.where(lane == i1, earlier, 0.0), axis=-1, keepdims=True)
    r2 = jnp.sum(jnp.where(lane == i2, earlier, 0.0), axis=-1, keepdims=True)
    seen = earlier[-1:, :] + uses[-1:, :].astype(F32)
    seen_ref[...] = seen
    count_ref[...] = jnp.broadcast_to(seen, count_ref.shape)

    cols = ((i1 - N_GROUPS).astype(F32), (i2 - N_GROUPS).astype(F32), w1, w2, r1, r2)
    route = jnp.zeros(logits.shape, F32)
    for c, val in enumerate(cols):
        route = jnp.where(lane == c, val, route)
    route_ref[...] = route


def _out_router(x, o_parts, w_out, g_ffn, w_group, b_group, w_router, b_router):
    n, d = x.shape
    n_parts = len(o_parts)
    w_bf = w_out.astype(BF16)
    w_parts, r0 = [], 0
    for o in o_parts:
        w_parts.append(w_bf[r0:r0 + o.shape[1]])
        r0 += o.shape[1]
    wr = jnp.zeros((d, LANES), F32).at[:, :N_GROUPS].set(w_group)
    wr = wr.at[:, N_GROUPS:N_GROUPS + N_EXPERTS].set(w_router)
    wr_hi = wr.astype(BF16)
    wr_lo = (wr - wr_hi.astype(F32)).astype(BF16)
    br = jnp.zeros((1, LANES), F32).at[0, :N_GROUPS].set(b_group.astype(F32))
    br = br.at[0, N_GROUPS:N_GROUPS + N_EXPERTS].set(b_router.astype(F32))
    ltri = (jnp.arange(ROW_TILE)[:, None] > jnp.arange(ROW_TILE)[None, :]).astype(BF16)
    rows = lambda width: pl.BlockSpec((ROW_TILE, width), lambda i: (i, 0))
    const = lambda shape: pl.BlockSpec(shape, lambda i: (0, 0))
    return pl.pallas_call(
        functools.partial(_out_router_kernel, n_parts=n_parts),
        grid=(n // ROW_TILE,),
        in_specs=([rows(d)] + [rows(o.shape[1]) for o in o_parts]
                  + [const(w.shape) for w in w_parts]
                  + [const((1, d)), const((d, LANES)), const((d, LANES)), const((1, LANES)),
                     const((ROW_TILE, ROW_TILE))]),
        out_specs=[rows(d), rows(d), rows(LANES), const((8, LANES))],
        out_shape=[jax.ShapeDtypeStruct((n, d), F32), jax.ShapeDtypeStruct((n, d), F32),
                   jax.ShapeDtypeStruct((n, LANES), F32), jax.ShapeDtypeStruct((8, LANES), F32)],
        scratch_shapes=[pltpu.VMEM((1, LANES), F32)],
        compiler_params=_cparams(("arbitrary",)),
        name="out_router",
    )(x, *o_parts, *w_parts, g_ffn.reshape(1, d), wr_hi, wr_lo, br, ltri)


def _dispatch_plan(route, counts_row, n_blocks):
    n = route.shape[0]
    e_flat = route[:, 0:2].astype(jnp.int32).reshape(-1)
    rank = route[:, 4:6].astype(jnp.int32).reshape(-1)
    counts = counts_row[N_GROUPS:N_GROUPS + N_EXPERTS].astype(jnp.int32)
    padded = (counts + MOE_TILE - 1) // MOE_TILE * MOE_TILE
    pad_ends = jnp.cumsum(padded)
    dest = ((pad_ends - padded)[e_flat] + rank).astype(jnp.int32)
    p = n_blocks * MOE_TILE
    tok_buf = jnp.zeros((p,), jnp.int32).at[dest].set(jnp.arange(2 * n, dtype=jnp.int32) // 2)
    blk_start = jnp.arange(n_blocks, dtype=jnp.int32) * MOE_TILE
    blk_e = jnp.sum((pad_ends[None, :] <= blk_start[:, None]).astype(jnp.int32), axis=1)
    blk_e = jnp.minimum(blk_e, N_EXPERTS - 1).astype(jnp.int32)
    return dest, tok_buf, blk_e


def _moe_kernel(blk_e_ref, tok_ref, h_hbm, wg_ref, wu_ref, wd_ref, y_ref, xbuf, wg_bf, wu_bf, wd_bf, sem):
    i = pl.program_id(0)
    last = pl.num_programs(0) - 1
    slot = i % MOE_SLOTS

    def start_rows(blk, slot_):
        for r in range(MOE_TILE):
            tok = tok_ref[blk * MOE_TILE + r]
            pltpu.make_async_copy(h_hbm.at[pl.ds(tok, 1), :], xbuf.at[slot_, pl.ds(r, 1), :],
                                  sem.at[slot_]).start()

    def wait_rows(slot_):
        def body(r, c):
            pltpu.make_async_copy(h_hbm.at[pl.ds(0, 1), :], xbuf.at[slot_, pl.ds(r, 1), :],
                                  sem.at[slot_]).wait()
            return c
        lax.fori_loop(0, MOE_TILE, body, 0, unroll=8)

    @pl.when(i == 0)
    def _():
        for ahead in range(MOE_SLOTS - 1):
            start_rows(jnp.minimum(ahead, last), ahead)

    @pl.when((i == 0) | (blk_e_ref[i] != blk_e_ref[jnp.maximum(i - 1, 0)]))
    def _():
        wg_bf[...] = wg_ref[...].astype(BF16)
        wu_bf[...] = wu_ref[...].astype(BF16)
        wd_bf[...] = wd_ref[...].astype(BF16)

    wait_rows(slot)
    xb = xbuf[slot].astype(BF16)
    start_rows(jnp.minimum(i + MOE_SLOTS - 1, last), (i + MOE_SLOTS - 1) % MOE_SLOTS)
    gate = _dot(xb, wg_bf[...])
    up = _dot(xb, wu_bf[...])
    act = (gate * jax.nn.sigmoid(gate) * up).astype(BF16)
    y_ref[...] = _dot(act, wd_bf[...])

    @pl.when(i == last)
    def _():
        for ahead in range(1, MOE_SLOTS):
            wait_rows((i + ahead) % MOE_SLOTS)


def _moe_experts(h, tok_buf, blk_e, w_gate, w_up, w_down, layer, n_blocks):
    n, d = h.shape
    f = w_gate.shape[-1]
    p = n_blocks * MOE_TILE
    return pl.pallas_call(
        _moe_kernel,
        grid_spec=pltpu.PrefetchScalarGridSpec(
            num_scalar_prefetch=2,
            grid=(n_blocks,),
            in_specs=[pl.BlockSpec(memory_space=pl.ANY),
                      pl.BlockSpec((None, None, d, f), lambda i, be, tk: (layer, be[i], 0, 0)),
                      pl.BlockSpec((None, None, d, f), lambda i, be, tk: (layer, be[i], 0, 0)),
                      pl.BlockSpec((None, None, f, d), lambda i, be, tk: (layer, be[i], 0, 0))],
            out_specs=pl.BlockSpec((MOE_TILE, d), lambda i, be, tk: (i, 0)),
            scratch_shapes=[pltpu.VMEM((MOE_SLOTS, MOE_TILE, d), F32), pltpu.VMEM((d, f), BF16),
                            pltpu.VMEM((d, f), BF16), pltpu.VMEM((f, d), BF16),
                            pltpu.SemaphoreType.DMA((MOE_SLOTS,))]),
        out_shape=jax.ShapeDtypeStruct((p, d), F32),
        compiler_params=_cparams(("arbitrary",)),
        name="moe_experts",
    )(blk_e, tok_buf, h, w_gate, w_up, w_down)


def _combine_kernel(dest_ref, x_ref, route_ref, y_hbm, o_ref, ybuf, sem):
    i = pl.program_id(0)
    nsteps = pl.num_programs(0)
    slot = i % 2

    def start_gather(step, slot_):
        def body(r, c):
            for k in range(2):
                src = dest_ref[(step * CMB_TILE + r) * 2 + k]
                pltpu.make_async_copy(y_hbm.at[pl.ds(src, 1), :], ybuf.at[slot_, k, pl.ds(r, 1), :],
                                      sem.at[slot_]).start()
            return c
        lax.fori_loop(0, CMB_TILE, body, 0, unroll=4)

    @pl.when(i == 0)
    def _():
        start_gather(0, 0)

    @pl.when(i + 1 < nsteps)
    def _():
        start_gather(i + 1, 1 - slot)

    def wait_body(r, c):
        for k in range(2):
            pltpu.make_async_copy(y_hbm.at[pl.ds(0, 1), :], ybuf.at[slot, k, pl.ds(r, 1), :],
                                  sem.at[slot]).wait()
        return c
    lax.fori_loop(0, CMB_TILE, wait_body, 0, unroll=4)
    route = route_ref[...]
    o_ref[...] = x_ref[...] + (ybuf[slot, 0] * route[:, 2:3] + ybuf[slot, 1] * route[:, 3:4])


def _combine(x, y_buf, dest, route):
    n, d = x.shape
    return pl.pallas_call(
        _combine_kernel,
        grid_spec=pltpu.PrefetchScalarGridSpec(
            num_scalar_prefetch=1,
            grid=(n // CMB_TILE,),
            in_specs=[pl.BlockSpec((CMB_TILE, d), lambda i, ds: (i, 0)),
                      pl.BlockSpec((CMB_TILE, LANES), lambda i, ds: (i, 0)),
                      pl.BlockSpec(memory_space=pl.ANY)],
            out_specs=pl.BlockSpec((CMB_TILE, d), lambda i, ds: (i, 0)),
            scratch_shapes=[pltpu.VMEM((2, 2, CMB_TILE, d), F32), pltpu.SemaphoreType.DMA((2,))]),
        out_shape=jax.ShapeDtypeStruct((n, d), F32),
        compiler_params=_cparams(("arbitrary",)),
        name="moe_combine",
    )(dest, x, route, y_buf)


def kernel(x, norm_mix_g, norm_ffn_g, ab_w_in, a_q_gain, a_k_gain, a_lam_q1, a_lam_k1, a_lam_q2,
           a_lam_k2, a_subln_g, ab_w_out, c_w_in, c_b_f, c_q_gain, c_k_gain, c_w_out, moe_w_group,
           moe_b_group, moe_w_router, moe_b_router, moe_w_gate, moe_w_up, moe_w_down):
    batch, seq, d = x.shape
    n = batch * seq
    depth = norm_mix_g.shape[0]
    heads = d // HEAD_DIM
    n_blocks = (2 * n) // MOE_TILE + N_EXPERTS
    xs = x.reshape(n, d).astype(F32)
    for layer in range(depth):
        i = layer // 2
        if layer % 2 == 0:
            proj = _even_in_proj(xs, norm_mix_g[layer], ab_w_in[i], a_q_gain[i], a_k_gain[i])
            lam_params = jnp.stack([a_lam_q1[i], a_lam_k1[i], a_lam_q2[i], a_lam_k2[i]])
            o_a = _diff_attention(proj, lam_params, a_subln_g[i], _zero_cutoff(a_q_gain[i], a_k_gain[i]),
                                  batch, seq, layer)
            o_b = _stick_attention(proj, batch, seq, 3 * A_HEADS)
            o_parts, w_out = [o_a, o_b], ab_w_out[i]
        else:
            qkv, cum = _odd_in_proj(xs, norm_mix_g[layer], c_w_in[i], c_b_f[i], c_q_gain[i],
                                    c_k_gain[i], seq)
            cum_t = cum[:, :heads].reshape(batch, seq, heads // 2, 2).transpose(0, 2, 3, 1)
            cutoff = _zero_cutoff(c_q_gain[i], c_k_gain[i])
            o_parts, w_out = [_fox_attention(qkv, cum_t, cutoff, batch, seq)], c_w_out[i]
        xs, h, route, counts = _out_router(xs, o_parts, w_out, norm_ffn_g[layer], moe_w_group[layer],
                                           moe_b_group[layer], moe_w_router[layer], moe_b_router[layer])
        dest, tok_buf, blk_e = _dispatch_plan(route, counts[0], n_blocks)
        y_buf = _moe_experts(h, tok_buf, blk_e, moe_w_gate, moe_w_up, moe_w_down, layer, n_blocks)
        xs = _combine(xs, y_buf, dest, route)
    return xs.reshape(batch, seq, d).astype(x.dtype)
```

```python
import functools
import math

import jax
import jax.numpy as jnp
from jax import lax
from jax.experimental import pallas as pl
from jax.experimental.pallas import tpu as pltpu

F32 = jnp.float32
BF16 = jnp.bfloat16

HEAD_DIM = 64
LANES = 128
EPS = 1e-6
SCALE = HEAD_DIM ** -0.5
N_GROUPS = 4
EXPERTS_PER_GROUP = 4
N_EXPERTS = N_GROUPS * EXPERTS_PER_GROUP
A_HEADS = 4
NEG = -1e30
LOG2E = math.log2(math.e)

ROW_TILE = 512
ATT_TILE = 512
FOX_Q_TILE = 1024
SB_Q_TILE = 512
SB_K_TILE = 256
EXP_ZERO_BELOW = -104.0
MOE_TILE = 256
MOE_SLOTS = 3
CMB_TILE = 256
SM_ROWS = 32
SEC = 512
IN_SECTIONS = 6
VMEM_LIMIT = 48 * 1024 * 1024


def _cparams(sem):
    return pltpu.CompilerParams(dimension_semantics=sem, vmem_limit_bytes=VMEM_LIMIT)


def _dot(a, b):
    return jnp.dot(a, b, preferred_element_type=F32)


def _dot_nt(a, b):
    return lax.dot_general(a, b, (((1,), (1,)), ((), ())), preferred_element_type=F32)


def _split2(x):
    hi = x.astype(BF16)
    lo = (x - hi.astype(F32)).astype(BF16)
    return hi, lo


def _rms_rows(x, g):
    return x * lax.rsqrt(jnp.mean(x * x, axis=-1, keepdims=True) + EPS) * g


def _group_rsqrt(acc, e_ref, et_ref):
    hi, lo = _split2(acc * acc)
    ss = _dot(hi, e_ref[...]) + _dot(lo, e_ref[...])
    rhi, rlo = _split2(lax.rsqrt(ss * (1.0 / HEAD_DIM) + EPS))
    return _dot(rhi, et_ref[...]) + _dot(rlo, et_ref[...])


def _softplus(z):
    return jnp.maximum(z, 0.0) + jnp.log(1.0 + jnp.exp(-jnp.abs(z)))


def _add_expert_rows(dest_ref, x_ref, route_ref, y_hbm, xo_ref, ybuf, sem):
    i = pl.program_id(0)
    last = pl.num_programs(0) - 1
    slot = i % 2

    def start_rows(step, slot_, row_lo=0, row_hi=ROW_TILE):
        for r in range(row_lo, row_hi):
            for k in range(2):
                src = dest_ref[(step * ROW_TILE + r) * 2 + k]
                pltpu.make_async_copy(y_hbm.at[pl.ds(src, 1), :], ybuf.at[slot_, k, pl.ds(r, 1), :],
                                      sem.at[slot_]).start(priority=k)

    def wait_rows(slot_):
        def body(r, c):
            for k in range(2):
                pltpu.make_async_copy(y_hbm.at[pl.ds(0, 1), :], ybuf.at[slot_, k, pl.ds(r, 1), :],
                                      sem.at[slot_]).wait()
            return c
        lax.fori_loop(0, ROW_TILE, body, 0, unroll=4)

    @pl.when(i == 0)
    def _():
        start_rows(0, 0)

    wait_rows(slot)
    route = route_ref[...]
    x = x_ref[...] + (ybuf[slot, 0] * route[:, 2:3] + ybuf[slot, 1] * route[:, 3:4])
    xo_ref[...] = x
    def issue(j):
        start_rows(jnp.minimum(i + 1, last), 1 - slot, j * ROW_TILE // IN_SECTIONS,
                   (j + 1) * ROW_TILE // IN_SECTIONS)

    def drain():
        @pl.when(i == last)
        def _():
            wait_rows(1 - slot)

    return x, issue, drain


def _even_in_kernel(*refs, fused):
    if fused:
        (dest_ref, x_ref, route_ref, y_hbm, g_ref, w_ref, gq_ref, gk_ref, e_ref, et_ref, o_ref, xo_ref,
         ybuf, sem) = refs
        x, issue, drain = _add_expert_rows(dest_ref, x_ref, route_ref, y_hbm, xo_ref, ybuf, sem)
    else:
        x_ref, g_ref, w_ref, gq_ref, gk_ref, e_ref, et_ref, o_ref = refs
        x = x_ref[...]
    xn = _rms_rows(x, g_ref[...]).astype(BF16)
    for sec in range(IN_SECTIONS):
        if fused:
            issue(sec)
        cols = slice(sec * SEC, (sec + 1) * SEC)
        acc = _dot(xn, w_ref[:, cols])
        if sec == 0:
            acc = acc * _group_rsqrt(acc, e_ref, et_ref) * gq_ref[...]
        elif sec == 1:
            acc = acc * _group_rsqrt(acc, e_ref, et_ref) * gk_ref[...]
        elif sec == 3:
            acc = acc * SCALE
        o_ref[:, cols] = acc.astype(BF16)
    if fused:
        drain()


def _odd_in_kernel(*refs, tiles_per_seq, fused):
    if fused:
        (dest_ref, x_ref, route_ref, y_hbm, g_ref, w_ref, wf_ref, bf_ref, gq_ref, gk_ref, e_ref, et_ref,
         tri_ref, o_ref, cum_ref, xo_ref, carry_ref, ybuf, sem) = refs
        x, issue, drain = _add_expert_rows(dest_ref, x_ref, route_ref, y_hbm, xo_ref, ybuf, sem)
    else:
        (x_ref, g_ref, w_ref, wf_ref, bf_ref, gq_ref, gk_ref, e_ref, et_ref, tri_ref, o_ref, cum_ref,
         carry_ref) = refs
        x = x_ref[...]
    i = pl.program_id(0)
    xn = _rms_rows(x, g_ref[...]).astype(BF16)
    for sec in range(IN_SECTIONS):
        if fused:
            issue(sec)
        cols = slice(sec * SEC, (sec + 1) * SEC)
        acc = _dot(xn, w_ref[:, cols])
        if sec < 2:
            acc = acc * _group_rsqrt(acc, e_ref, et_ref) * gq_ref[...]
        elif sec < 4:
            acc = acc * _group_rsqrt(acc, e_ref, et_ref) * gk_ref[...]
        o_ref[:, cols] = acc.astype(BF16)

    log_f = -_softplus(-(_dot(xn, wf_ref[...]) + bf_ref[...]))
    p1 = log_f.astype(BF16)
    r1 = log_f - p1.astype(F32)
    p2 = r1.astype(BF16)
    p3 = (r1 - p2.astype(F32)).astype(BF16)
    tri = tri_ref[...]
    within = _dot(tri, p1) + _dot(tri, p2) + _dot(tri, p3)

    @pl.when(i % tiles_per_seq == 0)
    def _():
        carry_ref[...] = jnp.zeros_like(carry_ref)

    cum = within + carry_ref[...]
    cum_ref[...] = cum
    carry_ref[...] = cum[-1:, :]
    if fused:
        drain()


def _group_maps():
    lane_grp = jnp.arange(SEC)[:, None] // HEAD_DIM
    e = (lane_grp == jnp.arange(LANES)[None, :]).astype(BF16)
    return e, e.T


def _in_proj_call(kernel_fn, name, x, moe, in_arrays, in_shapes, out_specs_fn, out_shapes, scratch):
    n, d = x.shape
    rows = lambda width: pl.BlockSpec((ROW_TILE, width), lambda i, *_: (i, 0))
    const = lambda shape: pl.BlockSpec(shape, lambda i, *_: (0,) * len(shape))
    in_specs = [rows(d)] + [const(s) for s in in_shapes]
    out_specs = out_specs_fn(rows)
    args = [x] + list(in_arrays)
    prefetch = []
    if moe is not None:
        y_buf, dest, route = moe
        in_specs = [rows(d), rows(LANES), pl.BlockSpec(memory_space=pl.ANY)] + in_specs[1:]
        args = [x, route, y_buf] + list(in_arrays)
        prefetch = [dest]
        out_specs = out_specs + [rows(d)]
        out_shapes = out_shapes + [jax.ShapeDtypeStruct((n, d), F32)]
        scratch = scratch + [pltpu.VMEM((2, 2, ROW_TILE, d), F32), pltpu.SemaphoreType.DMA((2,))]
    return pl.pallas_call(
        functools.partial(kernel_fn, fused=moe is not None),
        grid_spec=pltpu.PrefetchScalarGridSpec(
            num_scalar_prefetch=len(prefetch), grid=(n // ROW_TILE,), in_specs=in_specs,
            out_specs=out_specs, scratch_shapes=scratch),
        out_shape=out_shapes,
        compiler_params=_cparams(("arbitrary",)),
        name=name,
    )(*prefetch, *args)


def _even_in_proj(x, g, w_in, q_gain, k_gain, moe=None):
    n, d = x.shape
    e, et = _group_maps()
    reps = SEC // HEAD_DIM
    gq = jnp.tile(q_gain.astype(F32) * (SCALE * LOG2E), reps).reshape(1, SEC)
    gk = jnp.tile(k_gain.astype(F32), reps).reshape(1, SEC)
    return _in_proj_call(
        _even_in_kernel, "even_in_proj", x, moe,
        [g.reshape(1, d), w_in.astype(BF16), gq, gk, e, et],
        [(1, d), (d, 3 * d), (1, SEC), (1, SEC), (SEC, LANES), (LANES, SEC)],
        lambda rows: [rows(3 * d)], [jax.ShapeDtypeStruct((n, 3 * d), BF16)], [])


def _odd_in_proj(x, g, w_in, b_f, q_gain, k_gain, seq, moe=None):
    n, d = x.shape
    heads = d // HEAD_DIM
    e, et = _group_maps()
    reps = SEC // HEAD_DIM
    gq = jnp.tile(q_gain.astype(F32) * (SCALE * LOG2E), reps).reshape(1, SEC)
    gk = jnp.tile(k_gain.astype(F32), reps).reshape(1, SEC)
    wf = jnp.zeros((d, LANES), F32).at[:, :heads].set(w_in[:, 3 * d:]).astype(BF16)
    bf = jnp.zeros((1, LANES), F32).at[0, :heads].set(b_f.astype(F32))
    tri = (jnp.arange(ROW_TILE)[:, None] >= jnp.arange(ROW_TILE)[None, :]).astype(BF16)
    return _in_proj_call(
        functools.partial(_odd_in_kernel, tiles_per_seq=seq // ROW_TILE), "odd_in_proj", x, moe,
        [g.reshape(1, d), w_in[:, :3 * d].astype(BF16), wf, bf, gq, gk, e, et, tri],
        [(1, d), (d, 3 * d), (d, LANES), (1, LANES), (1, SEC), (1, SEC), (SEC, LANES), (LANES, SEC),
         (ROW_TILE, ROW_TILE)],
        lambda rows: [rows(3 * d), rows(LANES)],
        [jax.ShapeDtypeStruct((n, 3 * d), BF16), jax.ShapeDtypeStruct((n, LANES), F32)],
        [pltpu.VMEM((1, LANES), F32)])


def _flash_parts(q_parts, row0s, k_ref, vaug_ref, bias_fn, qi, kstart, tq, tile, m_refs, acc_refs):
    n_parts = len(q_parts)
    for p in range(n_parts):
        m_refs[p][...] = jnp.full(m_refs[p].shape, NEG, F32)
        acc_refs[p][...] = jnp.zeros(acc_refs[p].shape, F32)
    rel = (lax.broadcasted_iota(jnp.int32, (SM_ROWS, tq), 0)
           - lax.broadcasted_iota(jnp.int32, (SM_ROWS, tq), 1))

    def softmax_rows(s, brow, m_all, diag_row0):
        rows_p, width = s.shape
        m_news, chunks = [], []
        for c in range(rows_p // SM_ROWS):
            rows = slice(c * SM_ROWS, (c + 1) * SM_ROWS)
            sc = s[rows] + brow
            if diag_row0 is not None:
                sc = jnp.where(rel >= -(diag_row0 + c * SM_ROWS), sc, NEG)
            blocks = [sc[:, j * LANES:(j + 1) * LANES] for j in range(width // LANES)]
            top = functools.reduce(jnp.maximum, blocks)
            m_new = jnp.maximum(m_all[rows], jnp.max(top, axis=-1, keepdims=True))
            m_news.append(m_new)
            chunks.append(jnp.concatenate([jnp.exp2((b - m_new).astype(BF16)) for b in blocks], axis=1))
        return jnp.concatenate(m_news, axis=0), jnp.concatenate(chunks, axis=0)

    def step(key_start, width, masked):
        key_start = pl.multiple_of(key_start, tile)
        k = k_ref[pl.ds(key_start, width), :]
        va = vaug_ref[pl.ds(key_start, width), :]
        brow = bias_fn(key_start, width)
        s = [_dot_nt(q_parts[p], k) for p in range(n_parts)]
        m_old = [m_refs[p][...] for p in range(n_parts)]
        acc_old = [acc_refs[p][...] for p in range(n_parts)]
        m_new, acc_new = [], []
        for p in range(n_parts):
            m_p, w = softmax_rows(s[p], brow, m_old[p], row0s[p] if masked else None)
            alpha = jnp.exp2(m_old[p] - m_p)
            pv = _dot(w, va)
            m_new.append(m_p)
            acc_new.append(jnp.concatenate([alpha * acc_old[p][:, :LANES] + pv[:, :LANES],
                                            alpha * acc_old[p][:, LANES:] + pv[:, LANES:]], axis=1))
        for p in range(n_parts):
            m_refs[p][...] = m_new[p]
            acc_refs[p][...] = acc_new[p]

    first_own = qi * (tq // tile)
    n_plain = first_own - kstart

    def pair_body(j, c):
        step((kstart + 2 * j) * tile, 2 * tile, False)
        return c

    lax.fori_loop(0, n_plain // 2, pair_body, 0)

    @pl.when(n_plain % 2 == 1)
    def _():
        step((first_own - 1) * tile, tile, False)

    step(qi * tq, tq, True)


def _fill_vaug(qi, v_ref, vaug_ref):
    @pl.when(qi == 0)
    def _():
        vaug_ref[:, :LANES] = v_ref[...]
        vaug_ref[:, LANES:] = jnp.ones(v_ref.shape, BF16)


def _first_live_tile(first_own, decay_to_tile, cutoff):
    def cond(kt):
        return (kt < first_own) & (decay_to_tile(jnp.minimum(kt, first_own - 1)) > cutoff)
    return lax.while_loop(cond, lambda kt: kt + 1, jnp.int32(0))


def _flash_scratch(seq, rows):
    return [pltpu.VMEM((seq, 2 * LANES), BF16), pltpu.VMEM((rows, LANES), F32), pltpu.VMEM((rows, LANES), F32),
            pltpu.VMEM((rows, 2 * LANES), F32), pltpu.VMEM((rows, 2 * LANES), F32)]


def _fox_kernel(ftab_ref, cut_ref, q_ref, k_ref, v_ref, f_ref, o_ref, vaug_ref, m0_ref, m1_ref, acc0_ref,
                acc1_ref, *, tq, tile, nk):
    b = pl.program_id(0)
    hp = pl.program_id(1)
    qi = pl.program_id(2)
    qstart = pl.multiple_of(qi * tq, tq)
    first_own = qi * (tq // tile)
    half = tq // 2
    _fill_vaug(qi, v_ref, vaug_ref)
    q = q_ref[...]
    lane = lax.broadcasted_iota(jnp.int32, q.shape, 1)
    cutoff = cut_ref[0]

    def one_head(hh, carry):
        in_head = (lane >= hh * HEAD_DIM) & (lane < (hh + 1) * HEAD_DIM)
        qh = jnp.where(in_head, q, jnp.zeros_like(q))
        lane_half = lax.broadcasted_iota(jnp.int32, (half, LANES), 1)
        in_head_half = (lane_half >= hh * HEAD_DIM) & (lane_half < (hh + 1) * HEAD_DIM)

        def f_row(start, width):
            first = jnp.full((1, width), hh, jnp.int32) == 0
            return jnp.where(first, f_ref[0:1, pl.ds(start, width)], f_ref[1:2, pl.ds(start, width)])

        f_q0 = f_row(qstart, LANES)[:, 0:1]

        def bias(start, width):
            return LOG2E * (f_q0 - f_row(start, width))

        base = ((b * pl.num_programs(1) + hp) * 2 + hh) * nk

        def decay(kt):
            return (ftab_ref[base + kt + 1] - ftab_ref[base + first_own]) * LOG2E

        kstart = _first_live_tile(first_own, decay, cutoff)
        _flash_parts((qh[:half], qh[half:]), (0, half), k_ref, vaug_ref, bias, qi, kstart, tq, tile,
                     (m0_ref, m1_ref), (acc0_ref, acc1_ref))
        for p, acc_ref in enumerate((acc0_ref, acc1_ref)):
            rows = slice(p * half, (p + 1) * half)
            acc = acc_ref[...]
            o_h = acc[:, :LANES] / acc[:, LANES:]
            o_ref[rows, :] = jnp.where(in_head_half, o_h, o_ref[rows, :].astype(F32)).astype(BF16)
        return carry

    o_ref[...] = jnp.zeros_like(o_ref)
    lax.fori_loop(0, 2, one_head, 0)


def _zero_cutoff(q_gain, k_gain):
    bound = (HEAD_DIM * SCALE * LOG2E * 1.02) * jnp.max(jnp.abs(q_gain)) * jnp.max(jnp.abs(k_gain))
    return (150.0 + 2.0 * bound).astype(F32).reshape(1)


def _fox_attention(qkv, cum_t, cutoff, batch, seq):
    n, d3 = qkv.shape
    d = d3 // 3
    pairs = d // LANES
    tq, tile = FOX_Q_TILE, ATT_TILE
    nq = seq // tq
    ftab = cum_t[:, :, :, ::tile].reshape(-1)
    return pl.pallas_call(
        functools.partial(_fox_kernel, tq=tq, tile=tile, nk=seq // tile),
        grid_spec=pltpu.PrefetchScalarGridSpec(
            num_scalar_prefetch=2,
            grid=(batch, pairs, nq),
            in_specs=[pl.BlockSpec((tq, LANES), lambda b, h, i, ft, ct: (b * nq + i, h)),
                      pl.BlockSpec((seq, LANES), lambda b, h, i, ft, ct: (b, pairs + h)),
                      pl.BlockSpec((seq, LANES), lambda b, h, i, ft, ct: (b, 2 * pairs + h)),
                      pl.BlockSpec((None, None, 2, seq), lambda b, h, i, ft, ct: (b, h, 0, 0))],
            out_specs=pl.BlockSpec((tq, LANES), lambda b, h, i, ft, ct: (b * nq + i, h)),
            scratch_shapes=_flash_scratch(seq, tq // 2)),
        out_shape=jax.ShapeDtypeStruct((n, d), BF16),
        compiler_params=_cparams(("arbitrary", "arbitrary", "arbitrary")),
        name="fox_attention",
    )(ftab, cutoff, qkv, qkv, qkv, cum_t)


def _diff_kernel(slope_ref, cut_ref, q_ref, k_ref, v_ref, lam_ref, sub_ref, o_ref, vaug_ref, m0_ref, m1_ref,
                 acc0_ref, acc1_ref, *, tile, lam_init):
    h = pl.program_id(1)
    qi = pl.program_id(2)
    slope = slope_ref[h] * LOG2E
    qstart = qi * tile

    def bias(start, width):
        key_off = lax.broadcasted_iota(jnp.int32, (1, width), 1)
        return slope * (start - qstart + key_off).astype(F32)

    def decay(kt):
        return slope * (qstart - (kt + 1) * tile + 1).astype(F32)

    kstart = _first_live_tile(qi, decay, cut_ref[0])
    _fill_vaug(qi, v_ref, vaug_ref)
    q = q_ref[...]
    lane = lax.broadcasted_iota(jnp.int32, q.shape, 1)
    zero = jnp.zeros_like(q)
    q_parts = (jnp.where(lane < HEAD_DIM, q, zero), jnp.where(lane >= HEAD_DIM, q, zero))
    _flash_parts(q_parts, (0, 0), k_ref, vaug_ref, bias, qi, kstart, tile, tile, (m0_ref, m1_ref),
                 (acc0_ref, acc1_ref))
    acc0, acc1 = acc0_ref[...], acc1_ref[...]
    lp = lam_ref[...]
    lam = (jnp.exp(jnp.sum(lp[0:1] * lp[1:2], axis=-1, keepdims=True))
           - jnp.exp(jnp.sum(lp[2:3] * lp[3:4], axis=-1, keepdims=True)) + lam_init)
    o = acc0[:, :LANES] / acc0[:, LANES:] - lam * (acc1[:, :LANES] / acc1[:, LANES:])
    o_ref[...] = (_rms_rows(o, sub_ref[...]) * (1.0 - lam_init)).astype(BF16)


def _diff_attention(proj, lam_params, subln_g, cutoff, batch, seq, layer):
    n = proj.shape[0]
    tile = ATT_TILE
    nq = seq // tile
    lam_init = 0.8 - 0.6 * math.exp(-0.3 * layer)
    slopes = jnp.asarray([2.0 ** (-8.0 * (i + 1) / A_HEADS) for i in range(A_HEADS)], F32)
    imap = lambda f: (lambda b, h, i, s, c: f(b, h, i))
    return pl.pallas_call(
        functools.partial(_diff_kernel, tile=tile, lam_init=lam_init),
        grid_spec=pltpu.PrefetchScalarGridSpec(
            num_scalar_prefetch=2,
            grid=(batch, A_HEADS, nq),
            in_specs=[pl.BlockSpec((tile, LANES), imap(lambda b, h, i: (b * nq + i, h))),
                      pl.BlockSpec((seq, LANES), imap(lambda b, h, i: (b, A_HEADS + h))),
                      pl.BlockSpec((seq, LANES), imap(lambda b, h, i: (b, 2 * A_HEADS + h))),
                      pl.BlockSpec((4, HEAD_DIM), imap(lambda b, h, i: (0, 0))),
                      pl.BlockSpec((1, LANES), imap(lambda b, h, i: (0, 0)))],
            out_specs=pl.BlockSpec((tile, LANES), imap(lambda b, h, i: (b * nq + i, h))),
            scratch_shapes=_flash_scratch(seq, tile)),
        out_shape=jax.ShapeDtypeStruct((n, A_HEADS * LANES), BF16),
        compiler_params=_cparams(("arbitrary", "arbitrary", "arbitrary")),
        name="diff_attention",
    )(slopes, cutoff, proj, proj, proj, lam_params.astype(F32), subln_g.astype(F32).reshape(1, LANES))


def _stick_kernel(q_ref, k_ref, v_ref, ntri_ref, o_ref, *, tq, tk):
    qi = pl.program_id(2)
    ratio = tq // tk
    q = q_ref[...]
    lane = lax.broadcasted_iota(jnp.int32, q.shape, 1)
    zero = jnp.zeros_like(q)
    qs = (jnp.where(lane < HEAD_DIM, q, zero), jnp.where(lane >= HEAD_DIM, q, zero))
    rel = (lax.broadcasted_iota(jnp.int32, (tq, tk), 0)
           - lax.broadcasted_iota(jnp.int32, (tq, tk), 1))
    ntri = ntri_ref[...]

    def step(kt, carry, masked):
        start = pl.multiple_of(kt * tk, tk)
        k = k_ref[pl.ds(start, tk), :]
        v = v_ref[pl.ds(start, tk), :]
        if masked:
            before = rel > kt * tk - qi * tq
        new = []
        for hh in range(2):
            run, acc = carry[hh]
            z = _dot_nt(qs[hh], k)
            sp = _softplus(z)
            drop = jnp.where(before, sp, 0.0) if masked else sp
            hi, lo = _split2(drop)
            later = _dot(hi, ntri) + _dot(lo, ntri)
            w = jnp.exp((z - sp) + later + run)
            if masked:
                w = jnp.where(before, w, 0.0)
            acc = acc + _dot(w.astype(BF16), v)
            run = run - jnp.sum(drop, axis=-1, keepdims=True)
            new.append((run, acc))
        return tuple(new)

    init_one = (jnp.zeros((tq, 1), F32), jnp.zeros((tq, LANES), F32))
    carry = (init_one, init_one)
    for d in range(ratio - 1, -1, -1):
        carry = step(qi * ratio + d, carry, True)

    def live(c):
        top = jnp.maximum(jnp.max(c[0][0]), jnp.max(c[1][0]))
        return (top > EXP_ZERO_BELOW).astype(jnp.int32)

    def cond(state):
        kt, alive, _ = state
        return (kt >= 0) & (alive > 0)

    def body(state):
        kt, _, c = state
        c = step(kt, c, False)
        return kt - 1, live(c), c

    _, _, carry = lax.while_loop(cond, body, (qi * ratio - 1, live(carry), carry))
    o_ref[...] = jnp.where(lane < HEAD_DIM, carry[0][1], carry[1][1]).astype(BF16)


def _stick_attention(proj, batch, seq, col0):
    n = proj.shape[0]
    pairs = 4
    tq, tk = SB_Q_TILE, SB_K_TILE
    nq = seq // tq
    ntri = -(jnp.arange(tk)[:, None] > jnp.arange(tk)[None, :]).astype(BF16)
    return pl.pallas_call(
        functools.partial(_stick_kernel, tq=tq, tk=tk),
        grid=(batch, pairs, nq),
        in_specs=[pl.BlockSpec((tq, LANES), lambda b, h, i: (b * nq + i, col0 + h)),
                  pl.BlockSpec((seq, LANES), lambda b, h, i: (b, col0 + pairs + h)),
                  pl.BlockSpec((seq, LANES), lambda b, h, i: (b, col0 + 2 * pairs + h)),
                  pl.BlockSpec((tk, tk), lambda b, h, i: (0, 0))],
        out_specs=pl.BlockSpec((tq, LANES), lambda b, h, i: (b * nq + i, h)),
        out_shape=jax.ShapeDtypeStruct((n, pairs * LANES), BF16),
        compiler_params=_cparams(("arbitrary", "arbitrary", "arbitrary")),
        name="stick_attention",
    )(proj, proj, proj, ntri)


def _out_router_kernel(*refs, n_parts):
    x_ref = refs[0]
    o_refs = refs[1:1 + n_parts]
    w_refs = refs[1 + n_parts:1 + 2 * n_parts]
    (g_ref, wr_hi_ref, wr_lo_ref, br_ref, ltri_ref, xo_ref, h_ref, route_ref, count_ref,
     seen_ref) = refs[1 + 2 * n_parts:]
    mix = _dot(o_refs[0][...], w_refs[0][...])
    for p in range(1, n_parts):
        mix = mix + _dot(o_refs[p][...], w_refs[p][...])
    x = x_ref[...] + mix
    xo_ref[...] = x
    h = _rms_rows(x, g_ref[...])
    h_ref[...] = h

    h_hi, h_lo = _split2(h)
    logits = (_dot(h_hi, wr_hi_ref[...]) + _dot(h_hi, wr_lo_ref[...]) + _dot(h_lo, wr_hi_ref[...])
              + br_ref[...])
    lane = lax.broadcasted_iota(jnp.int32, logits.shape, 1)

    def first_max(vals):
        top = jnp.max(vals, axis=-1, keepdims=True)
        idx = jnp.min(jnp.where(vals == top, lane, LANES), axis=-1, keepdims=True)
        return top, idx

    g_logits = jnp.where(lane < N_GROUPS, logits, NEG)
    g_top, g_idx = first_max(g_logits)
    g_w = 1.0 / jnp.sum(jnp.exp(g_logits - g_top), axis=-1, keepdims=True)
    lo_lane = N_GROUPS + EXPERTS_PER_GROUP * g_idx
    in_group = (lane >= lo_lane) & (lane < lo_lane + EXPERTS_PER_GROUP)
    e_logits = jnp.where(in_group, logits, NEG)
    v1, i1 = first_max(e_logits)
    v2, i2 = first_max(jnp.where(lane == i1, NEG, e_logits))
    e2 = jnp.exp(v2 - v1)
    w1 = g_w / (1.0 + e2)
    w2 = g_w * e2 / (1.0 + e2)

    @pl.when(pl.program_id(0) == 0)
    def _():
        seen_ref[...] = jnp.zeros_like(seen_ref)

    uses = ((lane == i1) | (lane == i2)).astype(BF16)
    earlier = _dot(ltri_ref[...], uses) + seen_ref[...]
    r1 = jnp.sum(jnp.where(lane == i1, earlier, 0.0), axis=-1, keepdims=True)
    r2 = jnp.sum(jnp.where(lane == i2, earlier, 0.0), axis=-1, keepdims=True)
    seen = earlier[-1:, :] + uses[-1:, :].astype(F32)
    seen_ref[...] = seen
    count_ref[...] = jnp.broadcast_to(seen, count_ref.shape)

    cols = ((i1 - N_GROUPS).astype(F32), (i2 - N_GROUPS).astype(F32), w1, w2, r1, r2)
    route = jnp.zeros(logits.shape, F32)
    for c, val in enumerate(cols):
        route = jnp.where(lane == c, val, route)
    route_ref[...] = route


def _out_router(x, o_parts, w_out, g_ffn, w_group, b_group, w_router, b_router):
    n, d = x.shape
    n_parts = len(o_parts)
    w_bf = w_out.astype(BF16)
    w_parts, r0 = [], 0
    for o in o_parts:
        w_parts.append(w_bf[r0:r0 + o.shape[1]])
        r0 += o.shape[1]
    wr = jnp.zeros((d, LANES), F32).at[:, :N_GROUPS].set(w_group)
    wr = wr.at[:, N_GROUPS:N_GROUPS + N_EXPERTS].set(w_router)
    wr_hi = wr.astype(BF16)
    wr_lo = (wr - wr_hi.astype(F32)).astype(BF16)
    br = jnp.zeros((1, LANES), F32).at[0, :N_GROUPS].set(b_group.astype(F32))
    br = br.at[0, N_GROUPS:N_GROUPS + N_EXPERTS].set(b_router.astype(F32))
    ltri = (jnp.arange(ROW_TILE)[:, None] > jnp.arange(ROW_TILE)[None, :]).astype(BF16)
    rows = lambda width: pl.BlockSpec((ROW_TILE, width), lambda i: (i, 0))
    const = lambda shape: pl.BlockSpec(shape, lambda i: (0, 0))
    return pl.pallas_call(
        functools.partial(_out_router_kernel, n_parts=n_parts),
        grid=(n // ROW_TILE,),
        in_specs=([rows(d)] + [rows(o.shape[1]) for o in o_parts]
                  + [const(w.shape) for w in w_parts]
                  + [const((1, d)), const((d, LANES)), const((d, LANES)), const((1, LANES)),
                     const((ROW_TILE, ROW_TILE))]),
        out_specs=[rows(d), rows(d), rows(LANES), const((8, LANES))],
        out_shape=[jax.ShapeDtypeStruct((n, d), F32), jax.ShapeDtypeStruct((n, d), F32),
                   jax.ShapeDtypeStruct((n, LANES), F32), jax.ShapeDtypeStruct((8, LANES), F32)],
        scratch_shapes=[pltpu.VMEM((1, LANES), F32)],
        compiler_params=_cparams(("arbitrary",)),
        name="out_router",
    )(x, *o_parts, *w_parts, g_ffn.reshape(1, d), wr_hi, wr_lo, br, ltri)


def _dispatch_plan(route, counts_row, n_blocks):
    n = route.shape[0]
    e_flat = route[:, 0:2].astype(jnp.int32).reshape(-1)
    rank = route[:, 4:6].astype(jnp.int32).reshape(-1)
    counts = counts_row[N_GROUPS:N_GROUPS + N_EXPERTS].astype(jnp.int32)
    padded = (counts + MOE_TILE - 1) // MOE_TILE * MOE_TILE
    pad_ends = jnp.cumsum(padded)
    dest = ((pad_ends - padded)[e_flat] + rank).astype(jnp.int32)
    p = n_blocks * MOE_TILE
    tok_buf = jnp.zeros((p,), jnp.int32).at[dest].set(jnp.arange(2 * n, dtype=jnp.int32) // 2)
    blk_start = jnp.arange(n_blocks, dtype=jnp.int32) * MOE_TILE
    blk_e = jnp.sum((pad_ends[None, :] <= blk_start[:, None]).astype(jnp.int32), axis=1)
    blk_e = jnp.minimum(blk_e, N_EXPERTS - 1).astype(jnp.int32)
    return dest, tok_buf, blk_e


def _moe_kernel(blk_e_ref, tok_ref, h_hbm, wg_ref, wu_ref, wd_ref, y_ref, xbuf, wg_bf, wu_bf, wd_bf, sem):
    i = pl.program_id(0)
    last = pl.num_programs(0) - 1
    slot = i % MOE_SLOTS

    def start_rows(blk, slot_, row_lo=0, row_hi=MOE_TILE):
        for r in range(row_lo, row_hi):
            tok = tok_ref[blk * MOE_TILE + r]
            pltpu.make_async_copy(h_hbm.at[pl.ds(tok, 1), :], xbuf.at[slot_, pl.ds(r, 1), :],
                                  sem.at[slot_]).start(priority=r % 2)

    def wait_rows(slot_):
        def body(r, c):
            pltpu.make_async_copy(h_hbm.at[pl.ds(0, 1), :], xbuf.at[slot_, pl.ds(r, 1), :],
                                  sem.at[slot_]).wait()
            return c
        lax.fori_loop(0, MOE_TILE, body, 0, unroll=8)

    @pl.when(i == 0)
    def _():
        for ahead in range(MOE_SLOTS - 1):
            start_rows(jnp.minimum(ahead, last), ahead)

    @pl.when((i == 0) | (blk_e_ref[i] != blk_e_ref[jnp.maximum(i - 1, 0)]))
    def _():
        wg_bf[...] = wg_ref[...].astype(BF16)
        wu_bf[...] = wu_ref[...].astype(BF16)
        wd_bf[...] = wd_ref[...].astype(BF16)

    wait_rows(slot)
    xb = xbuf[slot].astype(BF16)
    nxt, nxt_slot = jnp.minimum(i + MOE_SLOTS - 1, last), (i + MOE_SLOTS - 1) % MOE_SLOTS
    cut1, cut2 = MOE_TILE // 3, 2 * MOE_TILE // 3
    start_rows(nxt, nxt_slot, 0, cut1)
    gate = _dot(xb, wg_bf[...])
    start_rows(nxt, nxt_slot, cut1, cut2)
    up = _dot(xb, wu_bf[...])
    act = (gate * jax.nn.sigmoid(gate) * up).astype(BF16)
    start_rows(nxt, nxt_slot, cut2, MOE_TILE)
    y_ref[...] = _dot(act, wd_bf[...])

    @pl.when(i == last)
    def _():
        for ahead in range(1, MOE_SLOTS):
            wait_rows((i + ahead) % MOE_SLOTS)


def _moe_experts(h, tok_buf, blk_e, w_gate, w_up, w_down, layer, n_blocks):
    n, d = h.shape
    f = w_gate.shape[-1]
    p = n_blocks * MOE_TILE
    return pl.pallas_call(
        _moe_kernel,
        grid_spec=pltpu.PrefetchScalarGridSpec(
            num_scalar_prefetch=2,
            grid=(n_blocks,),
            in_specs=[pl.BlockSpec(memory_space=pl.ANY),
                      pl.BlockSpec((None, None, d, f), lambda i, be, tk: (layer, be[i], 0, 0)),
                      pl.BlockSpec((None, None, d, f), lambda i, be, tk: (layer, be[i], 0, 0)),
                      pl.BlockSpec((None, None, f, d), lambda i, be, tk: (layer, be[i], 0, 0))],
            out_specs=pl.BlockSpec((MOE_TILE, d), lambda i, be, tk: (i, 0)),
            scratch_shapes=[pltpu.VMEM((MOE_SLOTS, MOE_TILE, d), F32), pltpu.VMEM((d, f), BF16),
                            pltpu.VMEM((d, f), BF16), pltpu.VMEM((f, d), BF16),
                            pltpu.SemaphoreType.DMA((MOE_SLOTS,))]),
        out_shape=jax.ShapeDtypeStruct((p, d), F32),
        compiler_params=_cparams(("arbitrary",)),
        name="moe_experts",
    )(blk_e, tok_buf, h, w_gate, w_up, w_down)


def _combine_kernel(dest_ref, x_ref, route_ref, y_hbm, o_ref, ybuf, sem):
    i = pl.program_id(0)
    nsteps = pl.num_programs(0)
    slot = i % 2

    def start_gather(step, slot_):
        def body(r, c):
            for k in range(2):
                src = dest_ref[(step * CMB_TILE + r) * 2 + k]
                pltpu.make_async_copy(y_hbm.at[pl.ds(src, 1), :], ybuf.at[slot_, k, pl.ds(r, 1), :],
                                      sem.at[slot_]).start(priority=k)
            return c
        lax.fori_loop(0, CMB_TILE, body, 0, unroll=4)

    @pl.when(i == 0)
    def _():
        start_gather(0, 0)

    @pl.when(i + 1 < nsteps)
    def _():
        start_gather(i + 1, 1 - slot)

    def wait_body(r, c):
        for k in range(2):
            pltpu.make_async_copy(y_hbm.at[pl.ds(0, 1), :], ybuf.at[slot, k, pl.ds(r, 1), :],
                                  sem.at[slot]).wait()
        return c
    lax.fori_loop(0, CMB_TILE, wait_body, 0, unroll=4)
    route = route_ref[...]
    o_ref[...] = x_ref[...] + (ybuf[slot, 0] * route[:, 2:3] + ybuf[slot, 1] * route[:, 3:4])


def _combine(x, y_buf, dest, route):
    n, d = x.shape
    return pl.pallas_call(
        _combine_kernel,
        grid_spec=pltpu.PrefetchScalarGridSpec(
            num_scalar_prefetch=1,
            grid=(n // CMB_TILE,),
            in_specs=[pl.BlockSpec((CMB_TILE, d), lambda i, ds: (i, 0)),
                      pl.BlockSpec((CMB_TILE, LANES), lambda i, ds: (i, 0)),
                      pl.BlockSpec(memory_space=pl.ANY)],
            out_specs=pl.BlockSpec((CMB_TILE, d), lambda i, ds: (i, 0)),
            scratch_shapes=[pltpu.VMEM((2, 2, CMB_TILE, d), F32), pltpu.SemaphoreType.DMA((2,))]),
        out_shape=jax.ShapeDtypeStruct((n, d), F32),
        compiler_params=_cparams(("arbitrary",)),
        name="moe_combine",
    )(dest, x, route, y_buf)


def kernel(x, norm_mix_g, norm_ffn_g, ab_w_in, a_q_gain, a_k_gain, a_lam_q1, a_lam_k1, a_lam_q2,
           a_lam_k2, a_subln_g, ab_w_out, c_w_in, c_b_f, c_q_gain, c_k_gain, c_w_out, moe_w_group,
           moe_b_group, moe_w_router, moe_b_router, moe_w_gate, moe_w_up, moe_w_down):
    batch, seq, d = x.shape
    n = batch * seq
    depth = norm_mix_g.shape[0]
    heads = d // HEAD_DIM
    n_blocks = (2 * n) // MOE_TILE + N_EXPERTS
    xs = x.reshape(n, d).astype(F32)
    moe = None
    for layer in range(depth):
        i = layer // 2
        if layer % 2 == 0:
            outs = _even_in_proj(xs, norm_mix_g[layer], ab_w_in[i], a_q_gain[i], a_k_gain[i], moe)
            proj = outs[0]
            xs = outs[-1] if moe is not None else xs
            lam_params = jnp.stack([a_lam_q1[i], a_lam_k1[i], a_lam_q2[i], a_lam_k2[i]])
            o_a = _diff_attention(proj, lam_params, a_subln_g[i], _zero_cutoff(a_q_gain[i], a_k_gain[i]),
                                  batch, seq, layer)
            o_b = _stick_attention(proj, batch, seq, 3 * A_HEADS)
            o_parts, w_out = [o_a, o_b], ab_w_out[i]
        else:
            outs = _odd_in_proj(xs, norm_mix_g[layer], c_w_in[i], c_b_f[i], c_q_gain[i], c_k_gain[i], seq,
                                moe)
            qkv, cum = outs[0], outs[1]
            xs = outs[-1] if moe is not None else xs
            cum_t = cum[:, :heads].reshape(batch, seq, heads // 2, 2).transpose(0, 2, 3, 1)
            cutoff = _zero_cutoff(c_q_gain[i], c_k_gain[i])
            o_parts, w_out = [_fox_attention(qkv, cum_t, cutoff, batch, seq)], c_w_out[i]
        xs, h, route, counts = _out_router(xs, o_parts, w_out, norm_ffn_g[layer], moe_w_group[layer],
                                           moe_b_group[layer], moe_w_router[layer], moe_b_router[layer])
        dest, tok_buf, blk_e = _dispatch_plan(route, counts[0], n_blocks)
        y_buf = _moe_experts(h, tok_buf, blk_e, moe_w_gate, moe_w_up, moe_w_down, layer, n_blocks)
        moe = (y_buf, dest, route)
    xs = _combine(xs, *moe)
    return xs.reshape(batch, seq, d).astype(x.dtype)
```

```python
import functools
import math

import jax
import jax.numpy as jnp
from jax import lax
from jax.experimental import pallas as pl
from jax.experimental.pallas import tpu as pltpu

F32 = jnp.float32
BF16 = jnp.bfloat16

HEAD_DIM = 64
LANES = 128
EPS = 1e-6
SCALE = HEAD_DIM ** -0.5
N_GROUPS = 4
EXPERTS_PER_GROUP = 4
N_EXPERTS = N_GROUPS * EXPERTS_PER_GROUP
A_HEADS = 4
NEG = -1e30
LOG2E = math.log2(math.e)

ROW_TILE = 512
ATT_TILE = 512
FOX_Q_TILE = 1024
SB_Q_TILE = 512
SB_K_TILE = 256
EXP_ZERO_BELOW = -104.0
MOE_TILE = 256
MOE_SLOTS = 3
CMB_TILE = 256
SM_ROWS = 32
SEC = 512
IN_SECTIONS = 6
VMEM_LIMIT = 48 * 1024 * 1024


def _cparams(sem):
    return pltpu.CompilerParams(dimension_semantics=sem, vmem_limit_bytes=VMEM_LIMIT)


def _dot(a, b):
    return jnp.dot(a, b, preferred_element_type=F32)


def _dot_nt(a, b):
    return lax.dot_general(a, b, (((1,), (1,)), ((), ())), preferred_element_type=F32)


def _split2(x):
    hi = x.astype(BF16)
    lo = (x - hi.astype(F32)).astype(BF16)
    return hi, lo


def _rms_rows(x, g):
    return x * lax.rsqrt(jnp.mean(x * x, axis=-1, keepdims=True) + EPS) * g


def _group_rsqrt(acc, e_ref, et_ref):
    hi, lo = _split2(acc * acc)
    ss = _dot(hi, e_ref[...]) + _dot(lo, e_ref[...])
    rhi, rlo = _split2(lax.rsqrt(ss * (1.0 / HEAD_DIM) + EPS))
    return _dot(rhi, et_ref[...]) + _dot(rlo, et_ref[...])


def _softplus(z):
    return jnp.maximum(z, 0.0) + jnp.log(1.0 + jnp.exp(-jnp.abs(z)))


def _add_expert_rows(dest_ref, x_ref, route_ref, y_hbm, xo_ref, ybuf, sem):
    i = pl.program_id(0)
    last = pl.num_programs(0) - 1
    slot = i % 2

    def start_rows(step, slot_, row_lo=0, row_hi=ROW_TILE):
        for r in range(row_lo, row_hi):
            for k in range(2):
                src = dest_ref[(step * ROW_TILE + r) * 2 + k]
                pltpu.make_async_copy(y_hbm.at[pl.ds(src, 1), :], ybuf.at[slot_, k, pl.ds(r, 1), :],
                                      sem.at[slot_]).start(priority=k)

    def wait_rows(slot_):
        for k in range(2):
            pltpu.make_async_copy(y_hbm.at[pl.ds(0, ROW_TILE), :], ybuf.at[slot_, k], sem.at[slot_]).wait()

    @pl.when(i == 0)
    def _():
        start_rows(0, 0)

    wait_rows(slot)
    route = route_ref[...]
    x = x_ref[...] + (ybuf[slot, 0] * route[:, 2:3] + ybuf[slot, 1] * route[:, 3:4])
    xo_ref[...] = x
    def issue(j):
        start_rows(jnp.minimum(i + 1, last), 1 - slot, j * ROW_TILE // IN_SECTIONS,
                   (j + 1) * ROW_TILE // IN_SECTIONS)

    def drain():
        @pl.when(i == last)
        def _():
            wait_rows(1 - slot)

    return x, issue, drain


def _even_in_kernel(*refs, fused):
    if fused:
        (dest_ref, x_ref, route_ref, y_hbm, g_ref, w_ref, gq_ref, gk_ref, e_ref, et_ref, o_ref, xo_ref,
         ybuf, sem) = refs
        x, issue, drain = _add_expert_rows(dest_ref, x_ref, route_ref, y_hbm, xo_ref, ybuf, sem)
    else:
        x_ref, g_ref, w_ref, gq_ref, gk_ref, e_ref, et_ref, o_ref = refs
        x = x_ref[...]
    xn = _rms_rows(x, g_ref[...]).astype(BF16)
    for sec in range(IN_SECTIONS):
        if fused:
            issue(sec)
        cols = slice(sec * SEC, (sec + 1) * SEC)
        acc = _dot(xn, w_ref[:, cols])
        if sec == 0:
            acc = acc * _group_rsqrt(acc, e_ref, et_ref) * gq_ref[...]
        elif sec == 1:
            acc = acc * _group_rsqrt(acc, e_ref, et_ref) * gk_ref[...]
        elif sec == 3:
            acc = acc * SCALE
        o_ref[:, cols] = acc.astype(BF16)
    if fused:
        drain()


def _odd_in_kernel(*refs, tiles_per_seq, fused):
    if fused:
        (dest_ref, x_ref, route_ref, y_hbm, g_ref, w_ref, wf_ref, bf_ref, gq_ref, gk_ref, e_ref, et_ref,
         tri_ref, o_ref, cum_ref, xo_ref, carry_ref, ybuf, sem) = refs
        x, issue, drain = _add_expert_rows(dest_ref, x_ref, route_ref, y_hbm, xo_ref, ybuf, sem)
    else:
        (x_ref, g_ref, w_ref, wf_ref, bf_ref, gq_ref, gk_ref, e_ref, et_ref, tri_ref, o_ref, cum_ref,
         carry_ref) = refs
        x = x_ref[...]
    i = pl.program_id(0)
    xn = _rms_rows(x, g_ref[...]).astype(BF16)
    for sec in range(IN_SECTIONS):
        if fused:
            issue(sec)
        cols = slice(sec * SEC, (sec + 1) * SEC)
        acc = _dot(xn, w_ref[:, cols])
        if sec < 2:
            acc = acc * _group_rsqrt(acc, e_ref, et_ref) * gq_ref[...]
        elif sec < 4:
            acc = acc * _group_rsqrt(acc, e_ref, et_ref) * gk_ref[...]
        o_ref[:, cols] = acc.astype(BF16)

    log_f = -_softplus(-(_dot(xn, wf_ref[...]) + bf_ref[...]))
    p1 = log_f.astype(BF16)
    r1 = log_f - p1.astype(F32)
    p2 = r1.astype(BF16)
    p3 = (r1 - p2.astype(F32)).astype(BF16)
    tri = tri_ref[...]
    within = _dot(tri, p1) + _dot(tri, p2) + _dot(tri, p3)

    @pl.when(i % tiles_per_seq == 0)
    def _():
        carry_ref[...] = jnp.zeros_like(carry_ref)

    cum = within + carry_ref[...]
    cum_ref[...] = cum
    carry_ref[...] = cum[-1:, :]
    if fused:
        drain()


def _group_maps():
    lane_grp = jnp.arange(SEC)[:, None] // HEAD_DIM
    e = (lane_grp == jnp.arange(LANES)[None, :]).astype(BF16)
    return e, e.T


def _in_proj_call(kernel_fn, name, x, moe, in_arrays, in_shapes, out_specs_fn, out_shapes, scratch):
    n, d = x.shape
    rows = lambda width: pl.BlockSpec((ROW_TILE, width), lambda i, *_: (i, 0))
    const = lambda shape: pl.BlockSpec(shape, lambda i, *_: (0,) * len(shape))
    in_specs = [rows(d)] + [const(s) for s in in_shapes]
    out_specs = out_specs_fn(rows)
    args = [x] + list(in_arrays)
    prefetch = []
    if moe is not None:
        y_buf, dest, route = moe
        in_specs = [rows(d), rows(LANES), pl.BlockSpec(memory_space=pl.ANY)] + in_specs[1:]
        args = [x, route, y_buf] + list(in_arrays)
        prefetch = [dest]
        out_specs = out_specs + [rows(d)]
        out_shapes = out_shapes + [jax.ShapeDtypeStruct((n, d), F32)]
        scratch = scratch + [pltpu.VMEM((2, 2, ROW_TILE, d), F32), pltpu.SemaphoreType.DMA((2,))]
    return pl.pallas_call(
        functools.partial(kernel_fn, fused=moe is not None),
        grid_spec=pltpu.PrefetchScalarGridSpec(
            num_scalar_prefetch=len(prefetch), grid=(n // ROW_TILE,), in_specs=in_specs,
            out_specs=out_specs, scratch_shapes=scratch),
        out_shape=out_shapes,
        compiler_params=_cparams(("arbitrary",)),
        name=name,
    )(*prefetch, *args)


def _even_in_proj(x, g, w_in, q_gain, k_gain, moe=None):
    n, d = x.shape
    e, et = _group_maps()
    reps = SEC // HEAD_DIM
    gq = jnp.tile(q_gain.astype(F32) * (SCALE * LOG2E), reps).reshape(1, SEC)
    gk = jnp.tile(k_gain.astype(F32), reps).reshape(1, SEC)
    return _in_proj_call(
        _even_in_kernel, "even_in_proj", x, moe,
        [g.reshape(1, d), w_in.astype(BF16), gq, gk, e, et],
        [(1, d), (d, 3 * d), (1, SEC), (1, SEC), (SEC, LANES), (LANES, SEC)],
        lambda rows: [rows(3 * d)], [jax.ShapeDtypeStruct((n, 3 * d), BF16)], [])


def _odd_in_proj(x, g, w_in, b_f, q_gain, k_gain, seq, moe=None):
    n, d = x.shape
    heads = d // HEAD_DIM
    e, et = _group_maps()
    reps = SEC // HEAD_DIM
    gq = jnp.tile(q_gain.astype(F32) * (SCALE * LOG2E), reps).reshape(1, SEC)
    gk = jnp.tile(k_gain.astype(F32), reps).reshape(1, SEC)
    wf = jnp.zeros((d, LANES), F32).at[:, :heads].set(w_in[:, 3 * d:]).astype(BF16)
    bf = jnp.zeros((1, LANES), F32).at[0, :heads].set(b_f.astype(F32))
    tri = (jnp.arange(ROW_TILE)[:, None] >= jnp.arange(ROW_TILE)[None, :]).astype(BF16)
    return _in_proj_call(
        functools.partial(_odd_in_kernel, tiles_per_seq=seq // ROW_TILE), "odd_in_proj", x, moe,
        [g.reshape(1, d), w_in[:, :3 * d].astype(BF16), wf, bf, gq, gk, e, et, tri],
        [(1, d), (d, 3 * d), (d, LANES), (1, LANES), (1, SEC), (1, SEC), (SEC, LANES), (LANES, SEC),
         (ROW_TILE, ROW_TILE)],
        lambda rows: [rows(3 * d), rows(LANES)],
        [jax.ShapeDtypeStruct((n, 3 * d), BF16), jax.ShapeDtypeStruct((n, LANES), F32)],
        [pltpu.VMEM((1, LANES), F32)])


def _flash_parts(q_parts, row0s, k_ref, vaug_ref, bias_fn, qi, kstart, tq, tile, m_refs, acc_refs):
    n_parts = len(q_parts)
    for p in range(n_parts):
        m_refs[p][...] = jnp.full(m_refs[p].shape, NEG, F32)
        acc_refs[p][...] = jnp.zeros(acc_refs[p].shape, F32)
    rel = (lax.broadcasted_iota(jnp.int32, (SM_ROWS, tq), 0)
           - lax.broadcasted_iota(jnp.int32, (SM_ROWS, tq), 1))

    def softmax_rows(s, brow, m_all, diag_row0):
        rows_p, width = s.shape
        m_news, chunks = [], []
        for c in range(rows_p // SM_ROWS):
            rows = slice(c * SM_ROWS, (c + 1) * SM_ROWS)
            sc = s[rows] + brow
            if diag_row0 is not None:
                sc = jnp.where(rel >= -(diag_row0 + c * SM_ROWS), sc, NEG)
            blocks = [sc[:, j * LANES:(j + 1) * LANES] for j in range(width // LANES)]
            top = functools.reduce(jnp.maximum, blocks)
            m_new = jnp.maximum(m_all[rows], jnp.max(top, axis=-1, keepdims=True))
            m_news.append(m_new)
            chunks.append(jnp.concatenate([jnp.exp2((b - m_new).astype(BF16)) for b in blocks], axis=1))
        return jnp.concatenate(m_news, axis=0), jnp.concatenate(chunks, axis=0)

    def step(key_start, width, masked):
        key_start = pl.multiple_of(key_start, tile)
        k = k_ref[pl.ds(key_start, width), :]
        va = vaug_ref[pl.ds(key_start, width), :]
        brow = bias_fn(key_start, width)
        s = [_dot_nt(q_parts[p], k) for p in range(n_parts)]
        m_old = [m_refs[p][...] for p in range(n_parts)]
        acc_old = [acc_refs[p][...] for p in range(n_parts)]
        m_new, acc_new = [], []
        for p in range(n_parts):
            m_p, w = softmax_rows(s[p], brow, m_old[p], row0s[p] if masked else None)
            alpha = jnp.exp2(m_old[p] - m_p)
            pv = _dot(w, va)
            m_new.append(m_p)
            acc_new.append(jnp.concatenate([alpha * acc_old[p][:, :LANES] + pv[:, :LANES],
                                            alpha * acc_old[p][:, LANES:] + pv[:, LANES:]], axis=1))
        for p in range(n_parts):
            m_refs[p][...] = m_new[p]
            acc_refs[p][...] = acc_new[p]

    first_own = qi * (tq // tile)
    n_plain = first_own - kstart

    def pair_body(j, c):
        step((kstart + 2 * j) * tile, 2 * tile, False)
        return c

    lax.fori_loop(0, n_plain // 2, pair_body, 0)

    @pl.when(n_plain % 2 == 1)
    def _():
        step((first_own - 1) * tile, tile, False)

    step(qi * tq, tq, True)


def _fill_vaug(qi, v_ref, vaug_ref):
    @pl.when(qi == 0)
    def _():
        vaug_ref[:, :LANES] = v_ref[...]
        vaug_ref[:, LANES:] = jnp.ones(v_ref.shape, BF16)


def _first_live_tile(first_own, decay_to_tile, cutoff):
    def cond(kt):
        return (kt < first_own) & (decay_to_tile(jnp.minimum(kt, first_own - 1)) > cutoff)
    return lax.while_loop(cond, lambda kt: kt + 1, jnp.int32(0))


def _flash_scratch(seq, rows):
    return [pltpu.VMEM((seq, 2 * LANES), BF16), pltpu.VMEM((rows, LANES), F32), pltpu.VMEM((rows, LANES), F32),
            pltpu.VMEM((rows, 2 * LANES), F32), pltpu.VMEM((rows, 2 * LANES), F32)]


def _fox_kernel(ftab_ref, cut_ref, q_ref, k_ref, v_ref, f_ref, o_ref, vaug_ref, m0_ref, m1_ref, acc0_ref,
                acc1_ref, *, tq, tile, nk):
    b = pl.program_id(0)
    hp = pl.program_id(1)
    qi = pl.program_id(2)
    qstart = pl.multiple_of(qi * tq, tq)
    first_own = qi * (tq // tile)
    half = tq // 2
    _fill_vaug(qi, v_ref, vaug_ref)
    q = q_ref[...]
    lane = lax.broadcasted_iota(jnp.int32, q.shape, 1)
    cutoff = cut_ref[0]

    def one_head(hh, carry):
        in_head = (lane >= hh * HEAD_DIM) & (lane < (hh + 1) * HEAD_DIM)
        qh = jnp.where(in_head, q, jnp.zeros_like(q))
        lane_half = lax.broadcasted_iota(jnp.int32, (half, LANES), 1)
        in_head_half = (lane_half >= hh * HEAD_DIM) & (lane_half < (hh + 1) * HEAD_DIM)

        def f_row(start, width):
            first = jnp.full((1, width), hh, jnp.int32) == 0
            return jnp.where(first, f_ref[0:1, pl.ds(start, width)], f_ref[1:2, pl.ds(start, width)])

        f_q0 = f_row(qstart, LANES)[:, 0:1]

        def bias(start, width):
            return LOG2E * (f_q0 - f_row(start, width))

        base = ((b * pl.num_programs(1) + hp) * 2 + hh) * nk

        def decay(kt):
            return (ftab_ref[base + kt + 1] - ftab_ref[base + first_own]) * LOG2E

        kstart = _first_live_tile(first_own, decay, cutoff)
        _flash_parts((qh[:half], qh[half:]), (0, half), k_ref, vaug_ref, bias, qi, kstart, tq, tile,
                     (m0_ref, m1_ref), (acc0_ref, acc1_ref))
        for p, acc_ref in enumerate((acc0_ref, acc1_ref)):
            rows = slice(p * half, (p + 1) * half)
            acc = acc_ref[...]
            o_h = acc[:, :LANES] / acc[:, LANES:]
            o_ref[rows, :] = jnp.where(in_head_half, o_h, o_ref[rows, :].astype(F32)).astype(BF16)
        return carry

    o_ref[...] = jnp.zeros_like(o_ref)
    lax.fori_loop(0, 2, one_head, 0)


def _zero_cutoff(q_gain, k_gain):
    bound = (HEAD_DIM * SCALE * LOG2E * 1.02) * jnp.max(jnp.abs(q_gain)) * jnp.max(jnp.abs(k_gain))
    return (150.0 + 2.0 * bound).astype(F32).reshape(1)


def _fox_attention(qkv, cum_t, cutoff, batch, seq):
    n, d3 = qkv.shape
    d = d3 // 3
    pairs = d // LANES
    tq, tile = FOX_Q_TILE, ATT_TILE
    nq = seq // tq
    ftab = cum_t[:, :, :, ::tile].reshape(-1)
    return pl.pallas_call(
        functools.partial(_fox_kernel, tq=tq, tile=tile, nk=seq // tile),
        grid_spec=pltpu.PrefetchScalarGridSpec(
            num_scalar_prefetch=2,
            grid=(batch, pairs, nq),
            in_specs=[pl.BlockSpec((tq, LANES), lambda b, h, i, ft, ct: (b * nq + i, h)),
                      pl.BlockSpec((seq, LANES), lambda b, h, i, ft, ct: (b, pairs + h)),
                      pl.BlockSpec((seq, LANES), lambda b, h, i, ft, ct: (b, 2 * pairs + h)),
                      pl.BlockSpec((None, None, 2, seq), lambda b, h, i, ft, ct: (b, h, 0, 0))],
            out_specs=pl.BlockSpec((tq, LANES), lambda b, h, i, ft, ct: (b * nq + i, h)),
            scratch_shapes=_flash_scratch(seq, tq // 2)),
        out_shape=jax.ShapeDtypeStruct((n, d), BF16),
        compiler_params=_cparams(("arbitrary", "arbitrary", "arbitrary")),
        name="fox_attention",
    )(ftab, cutoff, qkv, qkv, qkv, cum_t)


def _diff_kernel(slope_ref, cut_ref, q_ref, k_ref, v_ref, lam_ref, sub_ref, o_ref, vaug_ref, m0_ref, m1_ref,
                 acc0_ref, acc1_ref, *, tile, lam_init):
    h = pl.program_id(1)
    qi = pl.program_id(2)
    slope = slope_ref[h] * LOG2E
    qstart = qi * tile

    def bias(start, width):
        key_off = lax.broadcasted_iota(jnp.int32, (1, width), 1)
        return slope * (start - qstart + key_off).astype(F32)

    def decay(kt):
        return slope * (qstart - (kt + 1) * tile + 1).astype(F32)

    kstart = _first_live_tile(qi, decay, cut_ref[0])
    _fill_vaug(qi, v_ref, vaug_ref)
    q = q_ref[...]
    lane = lax.broadcasted_iota(jnp.int32, q.shape, 1)
    zero = jnp.zeros_like(q)
    q_parts = (jnp.where(lane < HEAD_DIM, q, zero), jnp.where(lane >= HEAD_DIM, q, zero))
    _flash_parts(q_parts, (0, 0), k_ref, vaug_ref, bias, qi, kstart, tile, tile, (m0_ref, m1_ref),
                 (acc0_ref, acc1_ref))
    acc0, acc1 = acc0_ref[...], acc1_ref[...]
    lp = lam_ref[...]
    lam = (jnp.exp(jnp.sum(lp[0:1] * lp[1:2], axis=-1, keepdims=True))
           - jnp.exp(jnp.sum(lp[2:3] * lp[3:4], axis=-1, keepdims=True)) + lam_init)
    o = acc0[:, :LANES] / acc0[:, LANES:] - lam * (acc1[:, :LANES] / acc1[:, LANES:])
    o_ref[...] = (_rms_rows(o, sub_ref[...]) * (1.0 - lam_init)).astype(BF16)


def _diff_attention(proj, lam_params, subln_g, cutoff, batch, seq, layer):
    n = proj.shape[0]
    tile = ATT_TILE
    nq = seq // tile
    lam_init = 0.8 - 0.6 * math.exp(-0.3 * layer)
    slopes = jnp.asarray([2.0 ** (-8.0 * (i + 1) / A_HEADS) for i in range(A_HEADS)], F32)
    imap = lambda f: (lambda b, h, i, s, c: f(b, h, i))
    return pl.pallas_call(
        functools.partial(_diff_kernel, tile=tile, lam_init=lam_init),
        grid_spec=pltpu.PrefetchScalarGridSpec(
            num_scalar_prefetch=2,
            grid=(batch, A_HEADS, nq),
            in_specs=[pl.BlockSpec((tile, LANES), imap(lambda b, h, i: (b * nq + i, h))),
                      pl.BlockSpec((seq, LANES), imap(lambda b, h, i: (b, A_HEADS + h))),
                      pl.BlockSpec((seq, LANES), imap(lambda b, h, i: (b, 2 * A_HEADS + h))),
                      pl.BlockSpec((4, HEAD_DIM), imap(lambda b, h, i: (0, 0))),
                      pl.BlockSpec((1, LANES), imap(lambda b, h, i: (0, 0)))],
            out_specs=pl.BlockSpec((tile, LANES), imap(lambda b, h, i: (b * nq + i, h))),
            scratch_shapes=_flash_scratch(seq, tile)),
        out_shape=jax.ShapeDtypeStruct((n, A_HEADS * LANES), BF16),
        compiler_params=_cparams(("arbitrary", "arbitrary", "arbitrary")),
        name="diff_attention",
    )(slopes, cutoff, proj, proj, proj, lam_params.astype(F32), subln_g.astype(F32).reshape(1, LANES))


def _stick_kernel(q_ref, k_ref, v_ref, ntri_ref, o_ref, *, tq, tk):
    qi = pl.program_id(2)
    ratio = tq // tk
    q = q_ref[...]
    lane = lax.broadcasted_iota(jnp.int32, q.shape, 1)
    zero = jnp.zeros_like(q)
    qs = (jnp.where(lane < HEAD_DIM, q, zero), jnp.where(lane >= HEAD_DIM, q, zero))
    rel = (lax.broadcasted_iota(jnp.int32, (tq, tk), 0)
           - lax.broadcasted_iota(jnp.int32, (tq, tk), 1))
    ntri = ntri_ref[...]

    def step(kt, carry, masked):
        start = pl.multiple_of(kt * tk, tk)
        k = k_ref[pl.ds(start, tk), :]
        v = v_ref[pl.ds(start, tk), :]
        if masked:
            before = rel > kt * tk - qi * tq
        new = []
        for hh in range(2):
            run, acc = carry[hh]
            z = _dot_nt(qs[hh], k)
            sp = _softplus(z)
            drop = jnp.where(before, sp, 0.0) if masked else sp
            hi, lo = _split2(drop)
            later = _dot(hi, ntri) + _dot(lo, ntri)
            w = jnp.exp((z - sp) + later + run)
            if masked:
                w = jnp.where(before, w, 0.0)
            acc = acc + _dot(w.astype(BF16), v)
            run = run - jnp.sum(drop, axis=-1, keepdims=True)
            new.append((run, acc))
        return tuple(new)

    init_one = (jnp.zeros((tq, 1), F32), jnp.zeros((tq, LANES), F32))
    carry = (init_one, init_one)
    for d in range(ratio - 1, -1, -1):
        carry = step(qi * ratio + d, carry, True)

    def live(c):
        top = jnp.maximum(jnp.max(c[0][0]), jnp.max(c[1][0]))
        return (top > EXP_ZERO_BELOW).astype(jnp.int32)

    def cond(state):
        kt, alive, _ = state
        return (kt >= 0) & (alive > 0)

    def body(state):
        kt, _, c = state
        c = step(kt, c, False)
        return kt - 1, live(c), c

    _, _, carry = lax.while_loop(cond, body, (qi * ratio - 1, live(carry), carry))
    o_ref[...] = jnp.where(lane < HEAD_DIM, carry[0][1], carry[1][1]).astype(BF16)


def _stick_attention(proj, batch, seq, col0):
    n = proj.shape[0]
    pairs = 4
    tq, tk = SB_Q_TILE, SB_K_TILE
    nq = seq // tq
    ntri = -(jnp.arange(tk)[:, None] > jnp.arange(tk)[None, :]).astype(BF16)
    return pl.pallas_call(
        functools.partial(_stick_kernel, tq=tq, tk=tk),
        grid=(batch, pairs, nq),
        in_specs=[pl.BlockSpec((tq, LANES), lambda b, h, i: (b * nq + i, col0 + h)),
                  pl.BlockSpec((seq, LANES), lambda b, h, i: (b, col0 + pairs + h)),
                  pl.BlockSpec((seq, LANES), lambda b, h, i: (b, col0 + 2 * pairs + h)),
                  pl.BlockSpec((tk, tk), lambda b, h, i: (0, 0))],
        out_specs=pl.BlockSpec((tq, LANES), lambda b, h, i: (b * nq + i, h)),
        out_shape=jax.ShapeDtypeStruct((n, pairs * LANES), BF16),
        compiler_params=_cparams(("arbitrary", "arbitrary", "arbitrary")),
        name="stick_attention",
    )(proj, proj, proj, ntri)


def _out_router_kernel(*refs, n_parts):
    x_ref = refs[0]
    o_refs = refs[1:1 + n_parts]
    w_refs = refs[1 + n_parts:1 + 2 * n_parts]
    (g_ref, wr_hi_ref, wr_lo_ref, br_ref, ltri_ref, xo_ref, h_ref, route_ref, count_ref,
     seen_ref) = refs[1 + 2 * n_parts:]
    mix = _dot(o_refs[0][...], w_refs[0][...])
    for p in range(1, n_parts):
        mix = mix + _dot(o_refs[p][...], w_refs[p][...])
    x = x_ref[...] + mix
    xo_ref[...] = x
    h = _rms_rows(x, g_ref[...])
    h_ref[...] = h

    h_hi, h_lo = _split2(h)
    logits = (_dot(h_hi, wr_hi_ref[...]) + _dot(h_hi, wr_lo_ref[...]) + _dot(h_lo, wr_hi_ref[...])
              + br_ref[...])
    lane = lax.broadcasted_iota(jnp.int32, logits.shape, 1)

    def first_max(vals):
        top = jnp.max(vals, axis=-1, keepdims=True)
        idx = jnp.min(jnp.where(vals == top, lane, LANES), axis=-1, keepdims=True)
        return top, idx

    g_logits = jnp.where(lane < N_GROUPS, logits, NEG)
    g_top, g_idx = first_max(g_logits)
    g_w = 1.0 / jnp.sum(jnp.exp(g_logits - g_top), axis=-1, keepdims=True)
    lo_lane = N_GROUPS + EXPERTS_PER_GROUP * g_idx
    in_group = (lane >= lo_lane) & (lane < lo_lane + EXPERTS_PER_GROUP)
    e_logits = jnp.where(in_group, logits, NEG)
    v1, i1 = first_max(e_logits)
    v2, i2 = first_max(jnp.where(lane == i1, NEG, e_logits))
    e2 = jnp.exp(v2 - v1)
    w1 = g_w / (1.0 + e2)
    w2 = g_w * e2 / (1.0 + e2)

    @pl.when(pl.program_id(0) == 0)
    def _():
        seen_ref[...] = jnp.zeros_like(seen_ref)

    uses = ((lane == i1) | (lane == i2)).astype(BF16)
    earlier = _dot(ltri_ref[...], uses) + seen_ref[...]
    r1 = jnp.sum(jnp.where(lane == i1, earlier, 0.0), axis=-1, keepdims=True)
    r2 = jnp.sum(jnp.where(lane == i2, earlier, 0.0), axis=-1, keepdims=True)
    seen = earlier[-1:, :] + uses[-1:, :].astype(F32)
    seen_ref[...] = seen
    count_ref[...] = jnp.broadcast_to(seen, count_ref.shape)

    cols = ((i1 - N_GROUPS).astype(F32), (i2 - N_GROUPS).astype(F32), w1, w2, r1, r2)
    route = jnp.zeros(logits.shape, F32)
    for c, val in enumerate(cols):
        route = jnp.where(lane == c, val, route)
    route_ref[...] = route


def _out_router(x, o_parts, w_out, g_ffn, w_group, b_group, w_router, b_router):
    n, d = x.shape
    n_parts = len(o_parts)
    w_bf = w_out.astype(BF16)
    w_parts, r0 = [], 0
    for o in o_parts:
        w_parts.append(w_bf[r0:r0 + o.shape[1]])
        r0 += o.shape[1]
    wr = jnp.zeros((d, LANES), F32).at[:, :N_GROUPS].set(w_group)
    wr = wr.at[:, N_GROUPS:N_GROUPS + N_EXPERTS].set(w_router)
    wr_hi = wr.astype(BF16)
    wr_lo = (wr - wr_hi.astype(F32)).astype(BF16)
    br = jnp.zeros((1, LANES), F32).at[0, :N_GROUPS].set(b_group.astype(F32))
    br = br.at[0, N_GROUPS:N_GROUPS + N_EXPERTS].set(b_router.astype(F32))
    ltri = (jnp.arange(ROW_TILE)[:, None] > jnp.arange(ROW_TILE)[None, :]).astype(BF16)
    rows = lambda width: pl.BlockSpec((ROW_TILE, width), lambda i: (i, 0))
    const = lambda shape: pl.BlockSpec(shape, lambda i: (0, 0))
    return pl.pallas_call(
        functools.partial(_out_router_kernel, n_parts=n_parts),
        grid=(n // ROW_TILE,),
        in_specs=([rows(d)] + [rows(o.shape[1]) for o in o_parts]
                  + [const(w.shape) for w in w_parts]
                  + [const((1, d)), const((d, LANES)), const((d, LANES)), const((1, LANES)),
                     const((ROW_TILE, ROW_TILE))]),
        out_specs=[rows(d), rows(d), rows(LANES), const((8, LANES))],
        out_shape=[jax.ShapeDtypeStruct((n, d), F32), jax.ShapeDtypeStruct((n, d), F32),
                   jax.ShapeDtypeStruct((n, LANES), F32), jax.ShapeDtypeStruct((8, LANES), F32)],
        scratch_shapes=[pltpu.VMEM((1, LANES), F32)],
        compiler_params=_cparams(("arbitrary",)),
        name="out_router",
    )(x, *o_parts, *w_parts, g_ffn.reshape(1, d), wr_hi, wr_lo, br, ltri)


def _dispatch_plan(route, counts_row, n_blocks):
    n = route.shape[0]
    e_flat = route[:, 0:2].astype(jnp.int32).reshape(-1)
    rank = route[:, 4:6].astype(jnp.int32).reshape(-1)
    counts = counts_row[N_GROUPS:N_GROUPS + N_EXPERTS].astype(jnp.int32)
    padded = (counts + MOE_TILE - 1) // MOE_TILE * MOE_TILE
    pad_ends = jnp.cumsum(padded)
    dest = ((pad_ends - padded)[e_flat] + rank).astype(jnp.int32)
    p = n_blocks * MOE_TILE
    tok_buf = jnp.zeros((p,), jnp.int32).at[dest].set(jnp.arange(2 * n, dtype=jnp.int32) // 2)
    blk_start = jnp.arange(n_blocks, dtype=jnp.int32) * MOE_TILE
    blk_e = jnp.sum((pad_ends[None, :] <= blk_start[:, None]).astype(jnp.int32), axis=1)
    blk_e = jnp.minimum(blk_e, N_EXPERTS - 1).astype(jnp.int32)
    return dest, tok_buf, blk_e


def _moe_kernel(blk_e_ref, tok_ref, h_hbm, wg_ref, wu_ref, wd_ref, y_ref, xbuf, wg_bf, wu_bf, wd_bf, sem):
    i = pl.program_id(0)
    last = pl.num_programs(0) - 1
    slot = i % MOE_SLOTS

    def start_rows(blk, slot_, row_lo=0, row_hi=MOE_TILE):
        for r in range(row_lo, row_hi):
            tok = tok_ref[blk * MOE_TILE + r]
            pltpu.make_async_copy(h_hbm.at[pl.ds(tok, 1), :], xbuf.at[slot_, pl.ds(r, 1), :],
                                  sem.at[slot_]).start(priority=r % 2)

    def wait_rows(slot_):
        pltpu.make_async_copy(h_hbm.at[pl.ds(0, MOE_TILE), :], xbuf.at[slot_], sem.at[slot_]).wait()

    @pl.when(i == 0)
    def _():
        for ahead in range(MOE_SLOTS - 1):
            start_rows(jnp.minimum(ahead, last), ahead)

    @pl.when((i == 0) | (blk_e_ref[i] != blk_e_ref[jnp.maximum(i - 1, 0)]))
    def _():
        wg_bf[...] = wg_ref[...].astype(BF16)
        wu_bf[...] = wu_ref[...].astype(BF16)
        wd_bf[...] = wd_ref[...].astype(BF16)

    wait_rows(slot)
    xb = xbuf[slot].astype(BF16)
    nxt, nxt_slot = jnp.minimum(i + MOE_SLOTS - 1, last), (i + MOE_SLOTS - 1) % MOE_SLOTS
    cut1, cut2 = MOE_TILE // 3, 2 * MOE_TILE // 3
    start_rows(nxt, nxt_slot, 0, cut1)
    gate = _dot(xb, wg_bf[...])
    start_rows(nxt, nxt_slot, cut1, cut2)
    up = _dot(xb, wu_bf[...])
    act = (gate * jax.nn.sigmoid(gate) * up).astype(BF16)
    start_rows(nxt, nxt_slot, cut2, MOE_TILE)
    y_ref[...] = _dot(act, wd_bf[...])

    @pl.when(i == last)
    def _():
        for ahead in range(1, MOE_SLOTS):
            wait_rows((i + ahead) % MOE_SLOTS)


def _moe_experts(h, tok_buf, blk_e, w_gate, w_up, w_down, layer, n_blocks):
    n, d = h.shape
    f = w_gate.shape[-1]
    p = n_blocks * MOE_TILE
    return pl.pallas_call(
        _moe_kernel,
        grid_spec=pltpu.PrefetchScalarGridSpec(
            num_scalar_prefetch=2,
            grid=(n_blocks,),
            in_specs=[pl.BlockSpec(memory_space=pl.ANY),
                      pl.BlockSpec((None, None, d, f), lambda i, be, tk: (layer, be[i], 0, 0)),
                      pl.BlockSpec((None, None, d, f), lambda i, be, tk: (layer, be[i], 0, 0)),
                      pl.BlockSpec((None, None, f, d), lambda i, be, tk: (layer, be[i], 0, 0))],
            out_specs=pl.BlockSpec((MOE_TILE, d), lambda i, be, tk: (i, 0)),
            scratch_shapes=[pltpu.VMEM((MOE_SLOTS, MOE_TILE, d), F32), pltpu.VMEM((d, f), BF16),
                            pltpu.VMEM((d, f), BF16), pltpu.VMEM((f, d), BF16),
                            pltpu.SemaphoreType.DMA((MOE_SLOTS,))]),
        out_shape=jax.ShapeDtypeStruct((p, d), F32),
        compiler_params=_cparams(("arbitrary",)),
        name="moe_experts",
    )(blk_e, tok_buf, h, w_gate, w_up, w_down)


def _combine_kernel(dest_ref, x_ref, route_ref, y_hbm, o_ref, ybuf, sem):
    i = pl.program_id(0)
    nsteps = pl.num_programs(0)
    slot = i % 2

    def start_gather(step, slot_):
        def body(r, c):
            for k in range(2):
                src = dest_ref[(step * CMB_TILE + r) * 2 + k]
                pltpu.make_async_copy(y_hbm.at[pl.ds(src, 1), :], ybuf.at[slot_, k, pl.ds(r, 1), :],
                                      sem.at[slot_]).start(priority=k)
            return c
        lax.fori_loop(0, CMB_TILE, body, 0, unroll=4)

    @pl.when(i == 0)
    def _():
        start_gather(0, 0)

    @pl.when(i + 1 < nsteps)
    def _():
        start_gather(i + 1, 1 - slot)

    for k in range(2):
        pltpu.make_async_copy(y_hbm.at[pl.ds(0, CMB_TILE), :], ybuf.at[slot, k], sem.at[slot]).wait()
    route = route_ref[...]
    o_ref[...] = x_ref[...] + (ybuf[slot, 0] * route[:, 2:3] + ybuf[slot, 1] * route[:, 3:4])


def _combine(x, y_buf, dest, route):
    n, d = x.shape
    return pl.pallas_call(
        _combine_kernel,
        grid_spec=pltpu.PrefetchScalarGridSpec(
            num_scalar_prefetch=1,
            grid=(n // CMB_TILE,),
            in_specs=[pl.BlockSpec((CMB_TILE, d), lambda i, ds: (i, 0)),
                      pl.BlockSpec((CMB_TILE, LANES), lambda i, ds: (i, 0)),
                      pl.BlockSpec(memory_space=pl.ANY)],
            out_specs=pl.BlockSpec((CMB_TILE, d), lambda i, ds: (i, 0)),
            scratch_shapes=[pltpu.VMEM((2, 2, CMB_TILE, d), F32), pltpu.SemaphoreType.DMA((2,))]),
        out_shape=jax.ShapeDtypeStruct((n, d), F32),
        compiler_params=_cparams(("arbitrary",)),
        name="moe_combine",
    )(dest, x, route, y_buf)


def kernel(x, norm_mix_g, norm_ffn_g, ab_w_in, a_q_gain, a_k_gain, a_lam_q1, a_lam_k1, a_lam_q2,
           a_lam_k2, a_subln_g, ab_w_out, c_w_in, c_b_f, c_q_gain, c_k_gain, c_w_out, moe_w_group,
           moe_b_group, moe_w_router, moe_b_router, moe_w_gate, moe_w_up, moe_w_down):
    batch, seq, d = x.shape
    n = batch * seq
    depth = norm_mix_g.shape[0]
    heads = d // HEAD_DIM
    n_blocks = (2 * n) // MOE_TILE + N_EXPERTS
    xs = x.reshape(n, d).astype(F32)
    moe = None
    for layer in range(depth):
        i = layer // 2
        if layer % 2 == 0:
            outs = _even_in_proj(xs, norm_mix_g[layer], ab_w_in[i], a_q_gain[i], a_k_gain[i], moe)
            proj = outs[0]
            xs = outs[-1] if moe is not None else xs
            lam_params = jnp.stack([a_lam_q1[i], a_lam_k1[i], a_lam_q2[i], a_lam_k2[i]])
            o_a = _diff_attention(proj, lam_params, a_subln_g[i], _zero_cutoff(a_q_gain[i], a_k_gain[i]),
                                  batch, seq, layer)
            o_b = _stick_attention(proj, batch, seq, 3 * A_HEADS)
            o_parts, w_out = [o_a, o_b], ab_w_out[i]
        else:
            outs = _odd_in_proj(xs, norm_mix_g[layer], c_w_in[i], c_b_f[i], c_q_gain[i], c_k_gain[i], seq,
                                moe)
            qkv, cum = outs[0], outs[1]
            xs = outs[-1] if moe is not None else xs
            cum_t = cum[:, :heads].reshape(batch, seq, heads // 2, 2).transpose(0, 2, 3, 1)
            cutoff = _zero_cutoff(c_q_gain[i], c_k_gain[i])
            o_parts, w_out = [_fox_attention(qkv, cum_t, cutoff, batch, seq)], c_w_out[i]
        xs, h, route, counts = _out_router(xs, o_parts, w_out, norm_ffn_g[layer], moe_w_group[layer],
                                           moe_b_group[layer], moe_w_router[layer], moe_b_router[layer])
        dest, tok_buf, blk_e = _dispatch_plan(route, counts[0], n_blocks)
        y_buf = _moe_experts(h, tok_buf, blk_e, moe_w_gate, moe_w_up, moe_w_down, layer, n_blocks)
        moe = (y_buf, dest, route)
    xs = _combine(xs, *moe)
    return xs.reshape(batch, seq, d).astype(x.dtype)
```

```python
import functools
import math

import jax
import jax.numpy as jnp
from jax import lax
from jax.experimental import pallas as pl
from jax.experimental.pallas import tpu as pltpu

F32 = jnp.float32
BF16 = jnp.bfloat16

HEAD_DIM = 64
LANES = 128
EPS = 1e-6
SCALE = HEAD_DIM ** -0.5
N_GROUPS = 4
EXPERTS_PER_GROUP = 4
N_EXPERTS = N_GROUPS * EXPERTS_PER_GROUP
A_HEADS = 4
NEG = -1e30
LOG2E = math.log2(math.e)

ROW_TILE = 512
ATT_TILE = 512
FOX_Q_TILE = 1024
SB_Q_TILE = 512
SB_K_TILE = 256
EXP_ZERO_BELOW = -104.0
MOE_TILE = 256
MOE_SLOTS = 3
CMB_TILE = 256
SM_ROWS = 32
SM_WIDTH = 1024
SEC = 512
IN_SECTIONS = 6
VMEM_LIMIT = 48 * 1024 * 1024


def _cparams(sem):
    return pltpu.CompilerParams(dimension_semantics=sem, vmem_limit_bytes=VMEM_LIMIT)


def _dot(a, b):
    return jnp.dot(a, b, preferred_element_type=F32)


def _dot_nt(a, b):
    return lax.dot_general(a, b, (((1,), (1,)), ((), ())), preferred_element_type=F32)


def _split2(x):
    hi = x.astype(BF16)
    lo = (x - hi.astype(F32)).astype(BF16)
    return hi, lo


def _rms_rows(x, g):
    return x * lax.rsqrt(jnp.mean(x * x, axis=-1, keepdims=True) + EPS) * g


def _group_rsqrt(acc, e_ref, et_ref):
    hi, lo = _split2(acc * acc)
    ss = _dot(hi, e_ref[...]) + _dot(lo, e_ref[...])
    rhi, rlo = _split2(lax.rsqrt(ss * (1.0 / HEAD_DIM) + EPS))
    return _dot(rhi, et_ref[...]) + _dot(rlo, et_ref[...])


def _softplus(z):
    return jnp.maximum(z, 0.0) + jnp.log(1.0 + jnp.exp(-jnp.abs(z)))


def _add_expert_rows(dest_ref, x_ref, route_ref, y_hbm, xo_ref, ybuf, sem):
    i = pl.program_id(0)
    last = pl.num_programs(0) - 1
    slot = i % 2

    def start_rows(step, slot_, row_lo=0, row_hi=ROW_TILE):
        for r in range(row_lo, row_hi):
            for k in range(2):
                src = dest_ref[(step * ROW_TILE + r) * 2 + k]
                pltpu.make_async_copy(y_hbm.at[pl.ds(src, 1), :], ybuf.at[slot_, k, pl.ds(r, 1), :],
                                      sem.at[slot_]).start(priority=k)

    def wait_rows(slot_):
        for k in range(2):
            pltpu.make_async_copy(y_hbm.at[pl.ds(0, ROW_TILE), :], ybuf.at[slot_, k], sem.at[slot_]).wait()

    @pl.when(i == 0)
    def _():
        start_rows(0, 0)

    wait_rows(slot)
    route = route_ref[...]
    x = x_ref[...] + (ybuf[slot, 0] * route[:, 2:3] + ybuf[slot, 1] * route[:, 3:4])
    xo_ref[...] = x
    def issue(j):
        start_rows(jnp.minimum(i + 1, last), 1 - slot, j * ROW_TILE // IN_SECTIONS,
                   (j + 1) * ROW_TILE // IN_SECTIONS)

    def drain():
        @pl.when(i == last)
        def _():
            wait_rows(1 - slot)

    return x, issue, drain


def _even_in_kernel(*refs, fused):
    if fused:
        (dest_ref, x_ref, route_ref, y_hbm, g_ref, w_ref, gq_ref, gk_ref, e_ref, et_ref, o_ref, xo_ref,
         ybuf, sem) = refs
        x, issue, drain = _add_expert_rows(dest_ref, x_ref, route_ref, y_hbm, xo_ref, ybuf, sem)
    else:
        x_ref, g_ref, w_ref, gq_ref, gk_ref, e_ref, et_ref, o_ref = refs
        x = x_ref[...]
    xn = _rms_rows(x, g_ref[...]).astype(BF16)
    for sec in range(IN_SECTIONS):
        if fused:
            issue(sec)
        cols = slice(sec * SEC, (sec + 1) * SEC)
        acc = _dot(xn, w_ref[:, cols])
        if sec == 0:
            acc = acc * _group_rsqrt(acc, e_ref, et_ref) * gq_ref[...]
        elif sec == 1:
            acc = acc * _group_rsqrt(acc, e_ref, et_ref) * gk_ref[...]
        elif sec == 3:
            acc = acc * SCALE
        o_ref[:, cols] = acc.astype(BF16)
    if fused:
        drain()


def _odd_in_kernel(*refs, tiles_per_seq, fused):
    if fused:
        (dest_ref, x_ref, route_ref, y_hbm, g_ref, w_ref, wf_ref, bf_ref, gq_ref, gk_ref, e_ref, et_ref,
         tri_ref, o_ref, cum_ref, xo_ref, carry_ref, ybuf, sem) = refs
        x, issue, drain = _add_expert_rows(dest_ref, x_ref, route_ref, y_hbm, xo_ref, ybuf, sem)
    else:
        (x_ref, g_ref, w_ref, wf_ref, bf_ref, gq_ref, gk_ref, e_ref, et_ref, tri_ref, o_ref, cum_ref,
         carry_ref) = refs
        x = x_ref[...]
    i = pl.program_id(0)
    xn = _rms_rows(x, g_ref[...]).astype(BF16)
    for sec in range(IN_SECTIONS):
        if fused:
            issue(sec)
        cols = slice(sec * SEC, (sec + 1) * SEC)
        acc = _dot(xn, w_ref[:, cols])
        if sec < 2:
            acc = acc * _group_rsqrt(acc, e_ref, et_ref) * gq_ref[...]
        elif sec < 4:
            acc = acc * _group_rsqrt(acc, e_ref, et_ref) * gk_ref[...]
        o_ref[:, cols] = acc.astype(BF16)

    log_f = -_softplus(-(_dot(xn, wf_ref[...]) + bf_ref[...]))
    p1 = log_f.astype(BF16)
    r1 = log_f - p1.astype(F32)
    p2 = r1.astype(BF16)
    p3 = (r1 - p2.astype(F32)).astype(BF16)
    tri = tri_ref[...]
    within = _dot(tri, p1) + _dot(tri, p2) + _dot(tri, p3)

    @pl.when(i % tiles_per_seq == 0)
    def _():
        carry_ref[...] = jnp.zeros_like(carry_ref)

    cum = within + carry_ref[...]
    cum_ref[...] = cum
    carry_ref[...] = cum[-1:, :]
    if fused:
        drain()


def _group_maps():
    lane_grp = jnp.arange(SEC)[:, None] // HEAD_DIM
    e = (lane_grp == jnp.arange(LANES)[None, :]).astype(BF16)
    return e, e.T


def _in_proj_call(kernel_fn, name, x, moe, in_arrays, in_shapes, out_specs_fn, out_shapes, scratch):
    n, d = x.shape
    rows = lambda width: pl.BlockSpec((ROW_TILE, width), lambda i, *_: (i, 0))
    const = lambda shape: pl.BlockSpec(shape, lambda i, *_: (0,) * len(shape))
    in_specs = [rows(d)] + [const(s) for s in in_shapes]
    out_specs = out_specs_fn(rows)
    args = [x] + list(in_arrays)
    prefetch = []
    if moe is not None:
        y_buf, dest, route = moe
        in_specs = [rows(d), rows(LANES), pl.BlockSpec(memory_space=pl.ANY)] + in_specs[1:]
        args = [x, route, y_buf] + list(in_arrays)
        prefetch = [dest]
        out_specs = out_specs + [rows(d)]
        out_shapes = out_shapes + [jax.ShapeDtypeStruct((n, d), F32)]
        scratch = scratch + [pltpu.VMEM((2, 2, ROW_TILE, d), F32), pltpu.SemaphoreType.DMA((2,))]
    return pl.pallas_call(
        functools.partial(kernel_fn, fused=moe is not None),
        grid_spec=pltpu.PrefetchScalarGridSpec(
            num_scalar_prefetch=len(prefetch), grid=(n // ROW_TILE,), in_specs=in_specs,
            out_specs=out_specs, scratch_shapes=scratch),
        out_shape=out_shapes,
        compiler_params=_cparams(("arbitrary",)),
        name=name,
    )(*prefetch, *args)


def _even_in_proj(x, g, w_in, q_gain, k_gain, moe=None):
    n, d = x.shape
    e, et = _group_maps()
    reps = SEC // HEAD_DIM
    gq = jnp.tile(q_gain.astype(F32) * (SCALE * LOG2E), reps).reshape(1, SEC)
    gk = jnp.tile(k_gain.astype(F32), reps).reshape(1, SEC)
    return _in_proj_call(
        _even_in_kernel, "even_in_proj", x, moe,
        [g.reshape(1, d), w_in.astype(BF16), gq, gk, e, et],
        [(1, d), (d, 3 * d), (1, SEC), (1, SEC), (SEC, LANES), (LANES, SEC)],
        lambda rows: [rows(3 * d)], [jax.ShapeDtypeStruct((n, 3 * d), BF16)], [])


def _odd_in_proj(x, g, w_in, b_f, q_gain, k_gain, seq, moe=None):
    n, d = x.shape
    heads = d // HEAD_DIM
    e, et = _group_maps()
    reps = SEC // HEAD_DIM
    gq = jnp.tile(q_gain.astype(F32) * (SCALE * LOG2E), reps).reshape(1, SEC)
    gk = jnp.tile(k_gain.astype(F32), reps).reshape(1, SEC)
    wf = jnp.zeros((d, LANES), F32).at[:, :heads].set(w_in[:, 3 * d:]).astype(BF16)
    bf = jnp.zeros((1, LANES), F32).at[0, :heads].set(b_f.astype(F32))
    tri = (jnp.arange(ROW_TILE)[:, None] >= jnp.arange(ROW_TILE)[None, :]).astype(BF16)
    return _in_proj_call(
        functools.partial(_odd_in_kernel, tiles_per_seq=seq // ROW_TILE), "odd_in_proj", x, moe,
        [g.reshape(1, d), w_in[:, :3 * d].astype(BF16), wf, bf, gq, gk, e, et, tri],
        [(1, d), (d, 3 * d), (d, LANES), (1, LANES), (1, SEC), (1, SEC), (SEC, LANES), (LANES, SEC),
         (ROW_TILE, ROW_TILE)],
        lambda rows: [rows(3 * d), rows(LANES)],
        [jax.ShapeDtypeStruct((n, 3 * d), BF16), jax.ShapeDtypeStruct((n, LANES), F32)],
        [pltpu.VMEM((1, LANES), F32)])


def _flash_parts(q_parts, row0s, k_ref, vaug_ref, bias_fn, qi, kstart, tq, tile, m_refs, acc_refs):
    n_parts = len(q_parts)
    for p in range(n_parts):
        m_refs[p][...] = jnp.full(m_refs[p].shape, NEG, F32)
        acc_refs[p][...] = jnp.zeros(acc_refs[p].shape, F32)

    def softmax_rows(s, brow, m_all, diag_off):
        rows_p, width = s.shape
        chunk = SM_ROWS if width <= SM_WIDTH else SM_ROWS // 2
        if diag_off is not None:
            rel = (lax.broadcasted_iota(jnp.int32, (chunk, width), 0)
                   - lax.broadcasted_iota(jnp.int32, (chunk, width), 1))
        m_news, chunks = [], []
        for c in range(rows_p // chunk):
            rows = slice(c * chunk, (c + 1) * chunk)
            sc = s[rows] + brow
            if diag_off is not None:
                sc = jnp.where(rel >= -(diag_off + c * chunk), sc, NEG)
            blocks = [sc[:, j * LANES:(j + 1) * LANES] for j in range(width // LANES)]
            top = functools.reduce(jnp.maximum, blocks)
            m_new = jnp.maximum(m_all[rows], jnp.max(top, axis=-1, keepdims=True))
            m_news.append(m_new)
            chunks.append(jnp.concatenate([jnp.exp2((b - m_new).astype(BF16)) for b in blocks], axis=1))
        return jnp.concatenate(m_news, axis=0), jnp.concatenate(chunks, axis=0)

    def step(key_start, width, modes):
        key_start = pl.multiple_of(key_start, tile)
        k = k_ref[pl.ds(key_start, width), :]
        va = vaug_ref[pl.ds(key_start, width), :]
        brow = bias_fn(key_start, width)
        live = [p for p in range(n_parts) if modes[p] != "skip"]
        s = {p: _dot_nt(q_parts[p], k) for p in live}
        m_old = {p: m_refs[p][...] for p in live}
        acc_old = {p: acc_refs[p][...] for p in live}
        m_new, acc_new = {}, {}
        for p in live:
            m_new[p], w = softmax_rows(s[p], brow, m_old[p], modes[p])
            alpha = jnp.exp2(m_old[p] - m_new[p])
            pv = _dot(w, va)
            acc_new[p] = jnp.concatenate([alpha * acc_old[p][:, :LANES] + pv[:, :LANES],
                                          alpha * acc_old[p][:, LANES:] + pv[:, LANES:]], axis=1)
        for p in live:
            m_refs[p][...] = m_new[p]
            acc_refs[p][...] = acc_new[p]

    plain = (None,) * n_parts
    first_own = qi * (tq // tile)
    n_plain = first_own - kstart
    n_quads = n_plain // 4
    rest = n_plain - 4 * n_quads

    def quad_body(j, c):
        step((kstart + 4 * j) * tile, 4 * tile, plain)
        return c

    lax.fori_loop(0, n_quads, quad_body, 0)

    @pl.when(rest >= 2)
    def _():
        step((kstart + 4 * n_quads) * tile, 2 * tile, plain)

    @pl.when(rest % 2 == 1)
    def _():
        step((first_own - 1) * tile, tile, plain)

    for j in range(tq // tile):
        modes = []
        for p in range(n_parts):
            rows_p = q_parts[p].shape[0]
            if row0s[p] + rows_p - 1 < j * tile:
                modes.append("skip")
            elif row0s[p] >= (j + 1) * tile - 1:
                modes.append(None)
            else:
                modes.append(row0s[p] - j * tile)
        step(qi * tq + j * tile, tile, tuple(modes))


def _fill_vaug(qi, v_ref, vaug_ref):
    @pl.when(qi == 0)
    def _():
        vaug_ref[:, :LANES] = v_ref[...]
        vaug_ref[:, LANES:] = jnp.ones(v_ref.shape, BF16)


def _first_live_tile(first_own, decay_to_tile, cutoff):
    def cond(kt):
        return (kt < first_own) & (decay_to_tile(jnp.minimum(kt, first_own - 1)) > cutoff)
    return lax.while_loop(cond, lambda kt: kt + 1, jnp.int32(0))


def _flash_scratch(seq, rows):
    return [pltpu.VMEM((seq, 2 * LANES), BF16), pltpu.VMEM((rows, LANES), F32), pltpu.VMEM((rows, LANES), F32),
            pltpu.VMEM((rows, 2 * LANES), F32), pltpu.VMEM((rows, 2 * LANES), F32)]


def _fox_kernel(ftab_ref, cut_ref, q_ref, k_ref, v_ref, f_ref, o_ref, vaug_ref, m0_ref, m1_ref, acc0_ref,
                acc1_ref, *, tq, tile, nk):
    b = pl.program_id(0)
    hp = pl.program_id(1)
    qi = pl.program_id(2)
    qstart = pl.multiple_of(qi * tq, tq)
    first_own = qi * (tq // tile)
    half = tq // 2
    _fill_vaug(qi, v_ref, vaug_ref)
    q = q_ref[...]
    lane = lax.broadcasted_iota(jnp.int32, q.shape, 1)
    cutoff = cut_ref[0]

    def one_head(hh, carry):
        in_head = (lane >= hh * HEAD_DIM) & (lane < (hh + 1) * HEAD_DIM)
        qh = jnp.where(in_head, q, jnp.zeros_like(q))
        lane_half = lax.broadcasted_iota(jnp.int32, (half, LANES), 1)
        in_head_half = (lane_half >= hh * HEAD_DIM) & (lane_half < (hh + 1) * HEAD_DIM)

        def f_row(start, width):
            first = jnp.full((1, width), hh, jnp.int32) == 0
            return jnp.where(first, f_ref[0:1, pl.ds(start, width)], f_ref[1:2, pl.ds(start, width)])

        f_q0 = f_row(qstart, LANES)[:, 0:1]

        def bias(start, width):
            return LOG2E * (f_q0 - f_row(start, width))

        base = ((b * pl.num_programs(1) + hp) * 2 + hh) * nk

        def decay(kt):
            return (ftab_ref[base + kt + 1] - ftab_ref[base + first_own]) * LOG2E

        kstart = _first_live_tile(first_own, decay, cutoff)
        _flash_parts((qh[:half], qh[half:]), (0, half), k_ref, vaug_ref, bias, qi, kstart, tq, tile,
                     (m0_ref, m1_ref), (acc0_ref, acc1_ref))
        for p, acc_ref in enumerate((acc0_ref, acc1_ref)):
            rows = slice(p * half, (p + 1) * half)
            acc = acc_ref[...]
            o_h = acc[:, :LANES] / acc[:, LANES:]
            o_ref[rows, :] = jnp.where(in_head_half, o_h, o_ref[rows, :].astype(F32)).astype(BF16)
        return carry

    o_ref[...] = jnp.zeros_like(o_ref)
    lax.fori_loop(0, 2, one_head, 0)


def _zero_cutoff(q_gain, k_gain):
    bound = (HEAD_DIM * SCALE * LOG2E * 1.02) * jnp.max(jnp.abs(q_gain)) * jnp.max(jnp.abs(k_gain))
    return (150.0 + 2.0 * bound).astype(F32).reshape(1)


def _fox_attention(qkv, cum_t, cutoff, batch, seq):
    n, d3 = qkv.shape
    d = d3 // 3
    pairs = d // LANES
    tq, tile = FOX_Q_TILE, ATT_TILE
    nq = seq // tq
    ftab = cum_t[:, :, :, ::tile].reshape(-1)
    return pl.pallas_call(
        functools.partial(_fox_kernel, tq=tq, tile=tile, nk=seq // tile),
        grid_spec=pltpu.PrefetchScalarGridSpec(
            num_scalar_prefetch=2,
            grid=(batch, pairs, nq),
            in_specs=[pl.BlockSpec((tq, LANES), lambda b, h, i, ft, ct: (b * nq + i, h)),
                      pl.BlockSpec((seq, LANES), lambda b, h, i, ft, ct: (b, pairs + h)),
                      pl.BlockSpec((seq, LANES), lambda b, h, i, ft, ct: (b, 2 * pairs + h)),
                      pl.BlockSpec((None, None, 2, seq), lambda b, h, i, ft, ct: (b, h, 0, 0))],
            out_specs=pl.BlockSpec((tq, LANES), lambda b, h, i, ft, ct: (b * nq + i, h)),
            scratch_shapes=_flash_scratch(seq, tq // 2)),
        out_shape=jax.ShapeDtypeStruct((n, d), BF16),
        compiler_params=_cparams(("arbitrary", "arbitrary", "arbitrary")),
        name="fox_attention",
    )(ftab, cutoff, qkv, qkv, qkv, cum_t)


def _diff_kernel(slope_ref, cut_ref, q_ref, k_ref, v_ref, lam_ref, sub_ref, o_ref, vaug_ref, m0_ref, m1_ref,
                 acc0_ref, acc1_ref, *, tile, lam_init):
    h = pl.program_id(1)
    qi = pl.program_id(2)
    slope = slope_ref[h] * LOG2E
    qstart = qi * tile

    def bias(start, width):
        key_off = lax.broadcasted_iota(jnp.int32, (1, width), 1)
        return slope * (start - qstart + key_off).astype(F32)

    def decay(kt):
        return slope * (qstart - (kt + 1) * tile + 1).astype(F32)

    kstart = _first_live_tile(qi, decay, cut_ref[0])
    _fill_vaug(qi, v_ref, vaug_ref)
    q = q_ref[...]
    lane = lax.broadcasted_iota(jnp.int32, q.shape, 1)
    zero = jnp.zeros_like(q)
    q_parts = (jnp.where(lane < HEAD_DIM, q, zero), jnp.where(lane >= HEAD_DIM, q, zero))
    _flash_parts(q_parts, (0, 0), k_ref, vaug_ref, bias, qi, kstart, tile, tile, (m0_ref, m1_ref),
                 (acc0_ref, acc1_ref))
    acc0, acc1 = acc0_ref[...], acc1_ref[...]
    lp = lam_ref[...]
    lam = (jnp.exp(jnp.sum(lp[0:1] * lp[1:2], axis=-1, keepdims=True))
           - jnp.exp(jnp.sum(lp[2:3] * lp[3:4], axis=-1, keepdims=True)) + lam_init)
    o = acc0[:, :LANES] / acc0[:, LANES:] - lam * (acc1[:, :LANES] / acc1[:, LANES:])
    o_ref[...] = (_rms_rows(o, sub_ref[...]) * (1.0 - lam_init)).astype(BF16)


def _diff_attention(proj, lam_params, subln_g, cutoff, batch, seq, layer):
    n = proj.shape[0]
    tile = ATT_TILE
    nq = seq // tile
    lam_init = 0.8 - 0.6 * math.exp(-0.3 * layer)
    slopes = jnp.asarray([2.0 ** (-8.0 * (i + 1) / A_HEADS) for i in range(A_HEADS)], F32)
    imap = lambda f: (lambda b, h, i, s, c: f(b, h, i))
    return pl.pallas_call(
        functools.partial(_diff_kernel, tile=tile, lam_init=lam_init),
        grid_spec=pltpu.PrefetchScalarGridSpec(
            num_scalar_prefetch=2,
            grid=(batch, A_HEADS, nq),
            in_specs=[pl.BlockSpec((tile, LANES), imap(lambda b, h, i: (b * nq + i, h))),
                      pl.BlockSpec((seq, LANES), imap(lambda b, h, i: (b, A_HEADS + h))),
                      pl.BlockSpec((seq, LANES), imap(lambda b, h, i: (b, 2 * A_HEADS + h))),
                      pl.BlockSpec((4, HEAD_DIM), imap(lambda b, h, i: (0, 0))),
                      pl.BlockSpec((1, LANES), imap(lambda b, h, i: (0, 0)))],
            out_specs=pl.BlockSpec((tile, LANES), imap(lambda b, h, i: (b * nq + i, h))),
            scratch_shapes=_flash_scratch(seq, tile)),
        out_shape=jax.ShapeDtypeStruct((n, A_HEADS * LANES), BF16),
        compiler_params=_cparams(("arbitrary", "arbitrary", "arbitrary")),
        name="diff_attention",
    )(slopes, cutoff, proj, proj, proj, lam_params.astype(F32), subln_g.astype(F32).reshape(1, LANES))


def _stick_kernel(q_ref, k_ref, v_ref, ntri_ref, o_ref, *, tq, tk):
    qi = pl.program_id(2)
    ratio = tq // tk
    q = q_ref[...]
    lane = lax.broadcasted_iota(jnp.int32, q.shape, 1)
    zero = jnp.zeros_like(q)
    qs = (jnp.where(lane < HEAD_DIM, q, zero), jnp.where(lane >= HEAD_DIM, q, zero))
    rel = (lax.broadcasted_iota(jnp.int32, (tq, tk), 0)
           - lax.broadcasted_iota(jnp.int32, (tq, tk), 1))
    ntri = ntri_ref[...]

    def step(kt, carry, masked):
        start = pl.multiple_of(kt * tk, tk)
        k = k_ref[pl.ds(start, tk), :]
        v = v_ref[pl.ds(start, tk), :]
        if masked:
            before = rel > kt * tk - qi * tq
        new = []
        for hh in range(2):
            run, acc = carry[hh]
            z = _dot_nt(qs[hh], k)
            sp = _softplus(z)
            drop = jnp.where(before, sp, 0.0) if masked else sp
            hi, lo = _split2(drop)
            later = _dot(hi, ntri) + _dot(lo, ntri)
            w = jnp.exp((z - sp) + later + run)
            if masked:
                w = jnp.where(before, w, 0.0)
            acc = acc + _dot(w.astype(BF16), v)
            run = run - jnp.sum(drop, axis=-1, keepdims=True)
            new.append((run, acc))
        return tuple(new)

    init_one = (jnp.zeros((tq, 1), F32), jnp.zeros((tq, LANES), F32))
    carry = (init_one, init_one)
    for d in range(ratio - 1, -1, -1):
        carry = step(qi * ratio + d, carry, True)

    def live(c):
        top = jnp.maximum(jnp.max(c[0][0]), jnp.max(c[1][0]))
        return (top > EXP_ZERO_BELOW).astype(jnp.int32)

    def cond(state):
        kt, alive, _ = state
        return (kt >= 0) & (alive > 0)

    def body(state):
        kt, _, c = state
        c = step(kt, c, False)
        return kt - 1, live(c), c

    _, _, carry = lax.while_loop(cond, body, (qi * ratio - 1, live(carry), carry))
    o_ref[...] = jnp.where(lane < HEAD_DIM, carry[0][1], carry[1][1]).astype(BF16)


def _stick_attention(proj, batch, seq, col0):
    n = proj.shape[0]
    pairs = 4
    tq, tk = SB_Q_TILE, SB_K_TILE
    nq = seq // tq
    ntri = -(jnp.arange(tk)[:, None] > jnp.arange(tk)[None, :]).astype(BF16)
    return pl.pallas_call(
        functools.partial(_stick_kernel, tq=tq, tk=tk),
        grid=(batch, pairs, nq),
        in_specs=[pl.BlockSpec((tq, LANES), lambda b, h, i: (b * nq + i, col0 + h)),
                  pl.BlockSpec((seq, LANES), lambda b, h, i: (b, col0 + pairs + h)),
                  pl.BlockSpec((seq, LANES), lambda b, h, i: (b, col0 + 2 * pairs + h)),
                  pl.BlockSpec((tk, tk), lambda b, h, i: (0, 0))],
        out_specs=pl.BlockSpec((tq, LANES), lambda b, h, i: (b * nq + i, h)),
        out_shape=jax.ShapeDtypeStruct((n, pairs * LANES), BF16),
        compiler_params=_cparams(("arbitrary", "arbitrary", "arbitrary")),
        name="stick_attention",
    )(proj, proj, proj, ntri)


def _out_router_kernel(*refs, n_parts):
    x_ref = refs[0]
    o_refs = refs[1:1 + n_parts]
    w_refs = refs[1 + n_parts:1 + 2 * n_parts]
    (g_ref, wr_hi_ref, wr_lo_ref, br_ref, ltri_ref, xo_ref, h_ref, route_ref, count_ref,
     seen_ref) = refs[1 + 2 * n_parts:]
    mix = _dot(o_refs[0][...], w_refs[0][...])
    for p in range(1, n_parts):
        mix = mix + _dot(o_refs[p][...], w_refs[p][...])
    x = x_ref[...] + mix
    xo_ref[...] = x
    h = _rms_rows(x, g_ref[...])
    h_ref[...] = h

    h_hi, h_lo = _split2(h)
    logits = (_dot(h_hi, wr_hi_ref[...]) + _dot(h_hi, wr_lo_ref[...]) + _dot(h_lo, wr_hi_ref[...])
              + br_ref[...])
    lane = lax.broadcasted_iota(jnp.int32, logits.shape, 1)

    def first_max(vals):
        top = jnp.max(vals, axis=-1, keepdims=True)
        idx = jnp.min(jnp.where(vals == top, lane, LANES), axis=-1, keepdims=True)
        return top, idx

    g_logits = jnp.where(lane < N_GROUPS, logits, NEG)
    g_top, g_idx = first_max(g_logits)
    g_w = 1.0 / jnp.sum(jnp.exp(g_logits - g_top), axis=-1, keepdims=True)
    lo_lane = N_GROUPS + EXPERTS_PER_GROUP * g_idx
    in_group = (lane >= lo_lane) & (lane < lo_lane + EXPERTS_PER_GROUP)
    e_logits = jnp.where(in_group, logits, NEG)
    v1, i1 = first_max(e_logits)
    v2, i2 = first_max(jnp.where(lane == i1, NEG, e_logits))
    e2 = jnp.exp(v2 - v1)
    w1 = g_w / (1.0 + e2)
    w2 = g_w * e2 / (1.0 + e2)

    @pl.when(pl.program_id(0) == 0)
    def _():
        seen_ref[...] = jnp.zeros_like(seen_ref)

    uses = ((lane == i1) | (lane == i2)).astype(BF16)
    earlier = _dot(ltri_ref[...], uses) + seen_ref[...]
    r1 = jnp.sum(jnp.where(lane == i1, earlier, 0.0), axis=-1, keepdims=True)
    r2 = jnp.sum(jnp.where(lane == i2, earlier, 0.0), axis=-1, keepdims=True)
    seen = earlier[-1:, :] + uses[-1:, :].astype(F32)
    seen_ref[...] = seen
    count_ref[...] = jnp.broadcast_to(seen, count_ref.shape)

    cols = ((i1 - N_GROUPS).astype(F32), (i2 - N_GROUPS).astype(F32), w1, w2, r1, r2)
    route = jnp.zeros(logits.shape, F32)
    for c, val in enumerate(cols):
        route = jnp.where(lane == c, val, route)
    route_ref[...] = route


def _out_router(x, o_parts, w_out, g_ffn, w_group, b_group, w_router, b_router):
    n, d = x.shape
    n_parts = len(o_parts)
    w_bf = w_out.astype(BF16)
    w_parts, r0 = [], 0
    for o in o_parts:
        w_parts.append(w_bf[r0:r0 + o.shape[1]])
        r0 += o.shape[1]
    wr = jnp.zeros((d, LANES), F32).at[:, :N_GROUPS].set(w_group)
    wr = wr.at[:, N_GROUPS:N_GROUPS + N_EXPERTS].set(w_router)
    wr_hi = wr.astype(BF16)
    wr_lo = (wr - wr_hi.astype(F32)).astype(BF16)
    br = jnp.zeros((1, LANES), F32).at[0, :N_GROUPS].set(b_group.astype(F32))
    br = br.at[0, N_GROUPS:N_GROUPS + N_EXPERTS].set(b_router.astype(F32))
    ltri = (jnp.arange(ROW_TILE)[:, None] > jnp.arange(ROW_TILE)[None, :]).astype(BF16)
    rows = lambda width: pl.BlockSpec((ROW_TILE, width), lambda i: (i, 0))
    const = lambda shape: pl.BlockSpec(shape, lambda i: (0, 0))
    return pl.pallas_call(
        functools.partial(_out_router_kernel, n_parts=n_parts),
        grid=(n // ROW_TILE,),
        in_specs=([rows(d)] + [rows(o.shape[1]) for o in o_parts]
                  + [const(w.shape) for w in w_parts]
                  + [const((1, d)), const((d, LANES)), const((d, LANES)), const((1, LANES)),
                     const((ROW_TILE, ROW_TILE))]),
        out_specs=[rows(d), rows(d), rows(LANES), const((8, LANES))],
        out_shape=[jax.ShapeDtypeStruct((n, d), F32), jax.ShapeDtypeStruct((n, d), F32),
                   jax.ShapeDtypeStruct((n, LANES), F32), jax.ShapeDtypeStruct((8, LANES), F32)],
        scratch_shapes=[pltpu.VMEM((1, LANES), F32)],
        compiler_params=_cparams(("arbitrary",)),
        name="out_router",
    )(x, *o_parts, *w_parts, g_ffn.reshape(1, d), wr_hi, wr_lo, br, ltri)


def _dispatch_plan(route, counts_row, n_blocks):
    n = route.shape[0]
    e_flat = route[:, 0:2].astype(jnp.int32).reshape(-1)
    rank = route[:, 4:6].astype(jnp.int32).reshape(-1)
    counts = counts_row[N_GROUPS:N_GROUPS + N_EXPERTS].astype(jnp.int32)
    padded = (counts + MOE_TILE - 1) // MOE_TILE * MOE_TILE
    pad_ends = jnp.cumsum(padded)
    dest = ((pad_ends - padded)[e_flat] + rank).astype(jnp.int32)
    p = n_blocks * MOE_TILE
    tok_buf = jnp.zeros((p,), jnp.int32).at[dest].set(jnp.arange(2 * n, dtype=jnp.int32) // 2)
    blk_start = jnp.arange(n_blocks, dtype=jnp.int32) * MOE_TILE
    blk_e = jnp.sum((pad_ends[None, :] <= blk_start[:, None]).astype(jnp.int32), axis=1)
    blk_e = jnp.minimum(blk_e, N_EXPERTS - 1).astype(jnp.int32)
    return dest, tok_buf, blk_e


def _moe_kernel(blk_e_ref, tok_ref, h_hbm, wg_ref, wu_ref, wd_ref, y_ref, xbuf, wg_bf, wu_bf, wd_bf, sem):
    i = pl.program_id(0)
    last = pl.num_programs(0) - 1
    slot = i % MOE_SLOTS

    def start_rows(blk, slot_, row_lo=0, row_hi=MOE_TILE):
        for r in range(row_lo, row_hi):
            tok = tok_ref[blk * MOE_TILE + r]
            pltpu.make_async_copy(h_hbm.at[pl.ds(tok, 1), :], xbuf.at[slot_, pl.ds(r, 1), :],
                                  sem.at[slot_]).start(priority=r % 2)

    def wait_rows(slot_):
        pltpu.make_async_copy(h_hbm.at[pl.ds(0, MOE_TILE), :], xbuf.at[slot_], sem.at[slot_]).wait()

    @pl.when(i == 0)
    def _():
        for ahead in range(MOE_SLOTS - 1):
            start_rows(jnp.minimum(ahead, last), ahead)

    @pl.when((i == 0) | (blk_e_ref[i] != blk_e_ref[jnp.maximum(i - 1, 0)]))
    def _():
        wg_bf[...] = wg_ref[...].astype(BF16)
        wu_bf[...] = wu_ref[...].astype(BF16)
        wd_bf[...] = wd_ref[...].astype(BF16)

    wait_rows(slot)
    xb = xbuf[slot].astype(BF16)
    nxt, nxt_slot = jnp.minimum(i + MOE_SLOTS - 1, last), (i + MOE_SLOTS - 1) % MOE_SLOTS
    cut1, cut2 = MOE_TILE // 3, 2 * MOE_TILE // 3
    start_rows(nxt, nxt_slot, 0, cut1)
    gate = _dot(xb, wg_bf[...])
    start_rows(nxt, nxt_slot, cut1, cut2)
    up = _dot(xb, wu_bf[...])
    act = (gate * jax.nn.sigmoid(gate) * up).astype(BF16)
    start_rows(nxt, nxt_slot, cut2, MOE_TILE)
    y_ref[...] = _dot(act, wd_bf[...])

    @pl.when(i == last)
    def _():
        for ahead in range(1, MOE_SLOTS):
            wait_rows((i + ahead) % MOE_SLOTS)


def _moe_experts(h, tok_buf, blk_e, w_gate, w_up, w_down, layer, n_blocks):
    n, d = h.shape
    f = w_gate.shape[-1]
    p = n_blocks * MOE_TILE
    return pl.pallas_call(
        _moe_kernel,
        grid_spec=pltpu.PrefetchScalarGridSpec(
            num_scalar_prefetch=2,
            grid=(n_blocks,),
            in_specs=[pl.BlockSpec(memory_space=pl.ANY),
                      pl.BlockSpec((None, None, d, f), lambda i, be, tk: (layer, be[i], 0, 0)),
                      pl.BlockSpec((None, None, d, f), lambda i, be, tk: (layer, be[i], 0, 0)),
                      pl.BlockSpec((None, None, f, d), lambda i, be, tk: (layer, be[i], 0, 0))],
            out_specs=pl.BlockSpec((MOE_TILE, d), lambda i, be, tk: (i, 0)),
            scratch_shapes=[pltpu.VMEM((MOE_SLOTS, MOE_TILE, d), F32), pltpu.VMEM((d, f), BF16),
                            pltpu.VMEM((d, f), BF16), pltpu.VMEM((f, d), BF16),
                            pltpu.SemaphoreType.DMA((MOE_SLOTS,))]),
        out_shape=jax.ShapeDtypeStruct((p, d), F32),
        compiler_params=_cparams(("arbitrary",)),
        name="moe_experts",
    )(blk_e, tok_buf, h, w_gate, w_up, w_down)


def _combine_kernel(dest_ref, x_ref, route_ref, y_hbm, o_ref, ybuf, sem):
    i = pl.program_id(0)
    nsteps = pl.num_programs(0)
    slot = i % 2

    def start_gather(step, slot_):
        def body(r, c):
            for k in range(2):
                src = dest_ref[(step * CMB_TILE + r) * 2 + k]
                pltpu.make_async_copy(y_hbm.at[pl.ds(src, 1), :], ybuf.at[slot_, k, pl.ds(r, 1), :],
                                      sem.at[slot_]).start(priority=k)
            return c
        lax.fori_loop(0, CMB_TILE, body, 0, unroll=4)

    @pl.when(i == 0)
    def _():
        start_gather(0, 0)

    @pl.when(i + 1 < nsteps)
    def _():
        start_gather(i + 1, 1 - slot)

    for k in range(2):
        pltpu.make_async_copy(y_hbm.at[pl.ds(0, CMB_TILE), :], ybuf.at[slot, k], sem.at[slot]).wait()
    route = route_ref[...]
    o_ref[...] = x_ref[...] + (ybuf[slot, 0] * route[:, 2:3] + ybuf[slot, 1] * route[:, 3:4])


def _combine(x, y_buf, dest, route):
    n, d = x.shape
    return pl.pallas_call(
        _combine_kernel,
        grid_spec=pltpu.PrefetchScalarGridSpec(
            num_scalar_prefetch=1,
            grid=(n // CMB_TILE,),
            in_specs=[pl.BlockSpec((CMB_TILE, d), lambda i, ds: (i, 0)),
                      pl.BlockSpec((CMB_TILE, LANES), lambda i, ds: (i, 0)),
                      pl.BlockSpec(memory_space=pl.ANY)],
            out_specs=pl.BlockSpec((CMB_TILE, d), lambda i, ds: (i, 0)),
            scratch_shapes=[pltpu.VMEM((2, 2, CMB_TILE, d), F32), pltpu.SemaphoreType.DMA((2,))]),
        out_shape=jax.ShapeDtypeStruct((n, d), F32),
        compiler_params=_cparams(("arbitrary",)),
        name="moe_combine",
    )(dest, x, route, y_buf)


def kernel(x, norm_mix_g, norm_ffn_g, ab_w_in, a_q_gain, a_k_gain, a_lam_q1, a_lam_k1, a_lam_q2,
           a_lam_k2, a_subln_g, ab_w_out, c_w_in, c_b_f, c_q_gain, c_k_gain, c_w_out, moe_w_group,
           moe_b_group, moe_w_router, moe_b_router, moe_w_gate, moe_w_up, moe_w_down):
    batch, seq, d = x.shape
    n = batch * seq
    depth = norm_mix_g.shape[0]
    heads = d // HEAD_DIM
    n_blocks = (2 * n) // MOE_TILE + N_EXPERTS
    xs = x.reshape(n, d).astype(F32)
    moe = None
    for layer in range(depth):
        i = layer // 2
        if layer % 2 == 0:
            outs = _even_in_proj(xs, norm_mix_g[layer], ab_w_in[i], a_q_gain[i], a_k_gain[i], moe)
            proj = outs[0]
            xs = outs[-1] if moe is not None else xs
            lam_params = jnp.stack([a_lam_q1[i], a_lam_k1[i], a_lam_q2[i], a_lam_k2[i]])
            o_a = _diff_attention(proj, lam_params, a_subln_g[i], _zero_cutoff(a_q_gain[i], a_k_gain[i]),
                                  batch, seq, layer)
            o_b = _stick_attention(proj, batch, seq, 3 * A_HEADS)
            o_parts, w_out = [o_a, o_b], ab_w_out[i]
        else:
            outs = _odd_in_proj(xs, norm_mix_g[layer], c_w_in[i], c_b_f[i], c_q_gain[i], c_k_gain[i], seq,
                                moe)
            qkv, cum = outs[0], outs[1]
            xs = outs[-1] if moe is not None else xs
            cum_t = cum[:, :heads].reshape(batch, seq, heads // 2, 2).transpose(0, 2, 3, 1)
            cutoff = _zero_cutoff(c_q_gain[i], c_k_gain[i])
            o_parts, w_out = [_fox_attention(qkv, cum_t, cutoff, batch, seq)], c_w_out[i]
        xs, h, route, counts = _out_router(xs, o_parts, w_out, norm_ffn_g[layer], moe_w_group[layer],
                                           moe_b_group[layer], moe_w_router[layer], moe_b_router[layer])
        dest, tok_buf, blk_e = _dispatch_plan(route, counts[0], n_blocks)
        y_buf = _moe_experts(h, tok_buf, blk_e, moe_w_gate, moe_w_up, moe_w_down, layer, n_blocks)
        moe = (y_buf, dest, route)
    xs = _combine(xs, *moe)
    return xs.reshape(batch, seq, d).astype(x.dtype)
```

```python
import functools
import math

import jax
import jax.numpy as jnp
from jax import lax
from jax.experimental import pallas as pl
from jax.experimental.pallas import tpu as pltpu

F32 = jnp.float32
BF16 = jnp.bfloat16

HEAD_DIM = 64
LANES = 128
EPS = 1e-6
SCALE = HEAD_DIM ** -0.5
N_GROUPS = 4
EXPERTS_PER_GROUP = 4
N_EXPERTS = N_GROUPS * EXPERTS_PER_GROUP
A_HEADS = 4
NEG = -1e30
LOG2E = math.log2(math.e)

ROW_TILE = 512
ATT_TILE = 512
FOX_Q_TILE = 1024
SB_Q_TILE = 512
SB_K_TILE = 256
EXP_ZERO_BELOW = -104.0
MOE_TILE = 256
MOE_SLOTS = 3
CMB_TILE = 256
SM_ROWS = 32
SM_WIDTH = 1024
SEC = 512
IN_SECTIONS = 6
VMEM_LIMIT = 48 * 1024 * 1024


def _cparams(sem):
    return pltpu.CompilerParams(dimension_semantics=sem, vmem_limit_bytes=VMEM_LIMIT)


def _dot(a, b):
    return jnp.dot(a, b, preferred_element_type=F32)


def _dot_nt(a, b):
    return lax.dot_general(a, b, (((1,), (1,)), ((), ())), preferred_element_type=F32)


def _split2(x):
    hi = x.astype(BF16)
    lo = (x - hi.astype(F32)).astype(BF16)
    return hi, lo


def _rms_rows(x, g):
    return x * lax.rsqrt(jnp.mean(x * x, axis=-1, keepdims=True) + EPS) * g


def _group_rsqrt(acc, e_ref, et_ref):
    hi, lo = _split2(acc * acc)
    ss = _dot(hi, e_ref[...]) + _dot(lo, e_ref[...])
    rhi, rlo = _split2(lax.rsqrt(ss * (1.0 / HEAD_DIM) + EPS))
    return _dot(rhi, et_ref[...]) + _dot(rlo, et_ref[...])


def _softplus(z):
    return jnp.maximum(z, 0.0) + jnp.log(1.0 + jnp.exp(-jnp.abs(z)))


def _add_expert_rows(dest_ref, x_ref, route_ref, y_hbm, xo_ref, ybuf, sem):
    i = pl.program_id(0)
    last = pl.num_programs(0) - 1
    slot = i % 2

    def start_rows(step, slot_, row_lo=0, row_hi=ROW_TILE):
        for r in range(row_lo, row_hi):
            for k in range(2):
                src = dest_ref[(step * ROW_TILE + r) * 2 + k]
                pltpu.make_async_copy(y_hbm.at[pl.ds(src, 1), :], ybuf.at[slot_, k, pl.ds(r, 1), :],
                                      sem.at[slot_]).start(priority=k)

    def wait_rows(slot_):
        for k in range(2):
            pltpu.make_async_copy(y_hbm.at[pl.ds(0, ROW_TILE), :], ybuf.at[slot_, k], sem.at[slot_]).wait()

    @pl.when(i == 0)
    def _():
        start_rows(0, 0)

    wait_rows(slot)
    route = route_ref[...]
    x = x_ref[...] + (ybuf[slot, 0] * route[:, 2:3] + ybuf[slot, 1] * route[:, 3:4])
    xo_ref[...] = x
    def issue(j):
        start_rows(jnp.minimum(i + 1, last), 1 - slot, j * ROW_TILE // IN_SECTIONS,
                   (j + 1) * ROW_TILE // IN_SECTIONS)

    def drain():
        @pl.when(i == last)
        def _():
            wait_rows(1 - slot)

    return x, issue, drain


def _even_in_kernel(*refs, fused):
    if fused:
        (dest_ref, x_ref, route_ref, y_hbm, g_ref, w_ref, gq_ref, gk_ref, e_ref, et_ref, o_ref, xo_ref,
         ybuf, sem) = refs
        x, issue, drain = _add_expert_rows(dest_ref, x_ref, route_ref, y_hbm, xo_ref, ybuf, sem)
    else:
        x_ref, g_ref, w_ref, gq_ref, gk_ref, e_ref, et_ref, o_ref = refs
        x = x_ref[...]
    xn = _rms_rows(x, g_ref[...]).astype(BF16)
    for sec in range(IN_SECTIONS):
        if fused:
            issue(sec)
        cols = slice(sec * SEC, (sec + 1) * SEC)
        acc = _dot(xn, w_ref[:, cols])
        if sec == 0:
            acc = acc * _group_rsqrt(acc, e_ref, et_ref) * gq_ref[...]
        elif sec == 1:
            acc = acc * _group_rsqrt(acc, e_ref, et_ref) * gk_ref[...]
        elif sec == 3:
            acc = acc * SCALE
        o_ref[:, cols] = acc.astype(BF16)
    if fused:
        drain()


def _odd_in_kernel(*refs, tiles_per_seq, fused):
    if fused:
        (dest_ref, x_ref, route_ref, y_hbm, g_ref, w_ref, wf_ref, bf_ref, gq_ref, gk_ref, e_ref, et_ref,
         tri_ref, o_ref, cum_ref, xo_ref, carry_ref, ybuf, sem) = refs
        x, issue, drain = _add_expert_rows(dest_ref, x_ref, route_ref, y_hbm, xo_ref, ybuf, sem)
    else:
        (x_ref, g_ref, w_ref, wf_ref, bf_ref, gq_ref, gk_ref, e_ref, et_ref, tri_ref, o_ref, cum_ref,
         carry_ref) = refs
        x = x_ref[...]
    i = pl.program_id(0)
    xn = _rms_rows(x, g_ref[...]).astype(BF16)
    for sec in range(IN_SECTIONS):
        if fused:
            issue(sec)
        cols = slice(sec * SEC, (sec + 1) * SEC)
        acc = _dot(xn, w_ref[:, cols])
        if sec < 2:
            acc = acc * _group_rsqrt(acc, e_ref, et_ref) * gq_ref[...]
        elif sec < 4:
            acc = acc * _group_rsqrt(acc, e_ref, et_ref) * gk_ref[...]
        o_ref[:, cols] = acc.astype(BF16)

    log_f = -_softplus(-(_dot(xn, wf_ref[...]) + bf_ref[...]))
    p1 = log_f.astype(BF16)
    r1 = log_f - p1.astype(F32)
    p2 = r1.astype(BF16)
    p3 = (r1 - p2.astype(F32)).astype(BF16)
    tri = tri_ref[...]
    within = _dot(tri, p1) + _dot(tri, p2) + _dot(tri, p3)

    @pl.when(i % tiles_per_seq == 0)
    def _():
        carry_ref[...] = jnp.zeros_like(carry_ref)

    cum = within + carry_ref[...]
    cum_ref[...] = cum
    carry_ref[...] = cum[-1:, :]
    if fused:
        drain()


def _group_maps():
    lane_grp = jnp.arange(SEC)[:, None] // HEAD_DIM
    e = (lane_grp == jnp.arange(LANES)[None, :]).astype(BF16)
    return e, e.T


def _in_proj_call(kernel_fn, name, x, moe, in_arrays, in_shapes, out_specs_fn, out_shapes, scratch):
    n, d = x.shape
    rows = lambda width: pl.BlockSpec((ROW_TILE, width), lambda i, *_: (i, 0))
    const = lambda shape: pl.BlockSpec(shape, lambda i, *_: (0,) * len(shape))
    in_specs = [rows(d)] + [const(s) for s in in_shapes]
    out_specs = out_specs_fn(rows)
    args = [x] + list(in_arrays)
    prefetch = []
    if moe is not None:
        y_buf, dest, route = moe
        in_specs = [rows(d), rows(LANES), pl.BlockSpec(memory_space=pl.ANY)] + in_specs[1:]
        args = [x, route, y_buf] + list(in_arrays)
        prefetch = [dest]
        out_specs = out_specs + [rows(d)]
        out_shapes = out_shapes + [jax.ShapeDtypeStruct((n, d), F32)]
        scratch = scratch + [pltpu.VMEM((2, 2, ROW_TILE, d), F32), pltpu.SemaphoreType.DMA((2,))]
    return pl.pallas_call(
        functools.partial(kernel_fn, fused=moe is not None),
        grid_spec=pltpu.PrefetchScalarGridSpec(
            num_scalar_prefetch=len(prefetch), grid=(n // ROW_TILE,), in_specs=in_specs,
            out_specs=out_specs, scratch_shapes=scratch),
        out_shape=out_shapes,
        compiler_params=_cparams(("arbitrary",)),
        name=name,
    )(*prefetch, *args)


def _even_in_proj(x, g, w_in, q_gain, k_gain, moe=None):
    n, d = x.shape
    e, et = _group_maps()
    reps = SEC // HEAD_DIM
    gq = jnp.tile(q_gain.astype(F32) * (SCALE * LOG2E), reps).reshape(1, SEC)
    gk = jnp.tile(k_gain.astype(F32), reps).reshape(1, SEC)
    return _in_proj_call(
        _even_in_kernel, "even_in_proj", x, moe,
        [g.reshape(1, d), w_in.astype(BF16), gq, gk, e, et],
        [(1, d), (d, 3 * d), (1, SEC), (1, SEC), (SEC, LANES), (LANES, SEC)],
        lambda rows: [rows(3 * d)], [jax.ShapeDtypeStruct((n, 3 * d), BF16)], [])


def _odd_in_proj(x, g, w_in, b_f, q_gain, k_gain, seq, moe=None):
    n, d = x.shape
    heads = d // HEAD_DIM
    e, et = _group_maps()
    reps = SEC // HEAD_DIM
    gq = jnp.tile(q_gain.astype(F32) * (SCALE * LOG2E), reps).reshape(1, SEC)
    gk = jnp.tile(k_gain.astype(F32), reps).reshape(1, SEC)
    wf = jnp.zeros((d, LANES), F32).at[:, :heads].set(w_in[:, 3 * d:]).astype(BF16)
    bf = jnp.zeros((1, LANES), F32).at[0, :heads].set(b_f.astype(F32))
    tri = (jnp.arange(ROW_TILE)[:, None] >= jnp.arange(ROW_TILE)[None, :]).astype(BF16)
    return _in_proj_call(
        functools.partial(_odd_in_kernel, tiles_per_seq=seq // ROW_TILE), "odd_in_proj", x, moe,
        [g.reshape(1, d), w_in[:, :3 * d].astype(BF16), wf, bf, gq, gk, e, et, tri],
        [(1, d), (d, 3 * d), (d, LANES), (1, LANES), (1, SEC), (1, SEC), (SEC, LANES), (LANES, SEC),
         (ROW_TILE, ROW_TILE)],
        lambda rows: [rows(3 * d), rows(LANES)],
        [jax.ShapeDtypeStruct((n, 3 * d), BF16), jax.ShapeDtypeStruct((n, LANES), F32)],
        [pltpu.VMEM((1, LANES), F32)])


def _flash_parts(q_parts, row0s, k_ref, vaug_ref, bias_fn, qi, kstart, tq, tile, m_refs, acc_refs):
    n_parts = len(q_parts)
    for p in range(n_parts):
        m_refs[p][...] = jnp.full(m_refs[p].shape, NEG, F32)
        acc_refs[p][...] = jnp.zeros(acc_refs[p].shape, F32)

    def softmax_rows(s, brow, m_all, diag_off):
        rows_p, width = s.shape
        chunk = SM_ROWS if width <= SM_WIDTH else SM_ROWS // 2
        if diag_off is not None:
            rel = (lax.broadcasted_iota(jnp.int32, (chunk, width), 0)
                   - lax.broadcasted_iota(jnp.int32, (chunk, width), 1))
        m_news, chunks = [], []
        for c in range(rows_p // chunk):
            rows = slice(c * chunk, (c + 1) * chunk)
            sc = s[rows] + brow
            if diag_off is not None:
                sc = jnp.where(rel >= -(diag_off + c * chunk), sc, NEG)
            blocks = [sc[:, j * LANES:(j + 1) * LANES] for j in range(width // LANES)]
            top = functools.reduce(jnp.maximum, blocks)
            m_new = jnp.maximum(m_all[rows], jnp.max(top, axis=-1, keepdims=True))
            m_news.append(m_new)
            chunks.append(jnp.concatenate([jnp.exp2((b - m_new).astype(BF16)) for b in blocks], axis=1))
        return jnp.concatenate(m_news, axis=0), jnp.concatenate(chunks, axis=0)

    def step(key_start, width, modes):
        key_start = pl.multiple_of(key_start, tile)
        k = k_ref[pl.ds(key_start, width), :]
        va = vaug_ref[pl.ds(key_start, width), :]
        brow = bias_fn(key_start, width)
        live = [p for p in range(n_parts) if modes[p] != "skip"]
        s = {p: _dot_nt(q_parts[p], k) for p in live}
        m_old = {p: m_refs[p][...] for p in live}
        acc_old = {p: acc_refs[p][...] for p in live}
        m_new, acc_new = {}, {}
        for p in live:
            m_new[p], w = softmax_rows(s[p], brow, m_old[p], modes[p])
            alpha = jnp.exp2(m_old[p] - m_new[p])
            pv = _dot(w, va)
            acc_new[p] = jnp.concatenate([alpha * acc_old[p][:, :LANES] + pv[:, :LANES],
                                          alpha * acc_old[p][:, LANES:] + pv[:, LANES:]], axis=1)
        for p in live:
            m_refs[p][...] = m_new[p]
            acc_refs[p][...] = acc_new[p]

    plain = (None,) * n_parts
    first_own = qi * (tq // tile)
    n_plain = first_own - kstart
    n_quads = n_plain // 4
    rest = n_plain - 4 * n_quads

    def quad_body(j, c):
        step((kstart + 4 * j) * tile, 4 * tile, plain)
        return c

    lax.fori_loop(0, n_quads, quad_body, 0)

    @pl.when(rest >= 2)
    def _():
        step((kstart + 4 * n_quads) * tile, 2 * tile, plain)

    @pl.when(rest % 2 == 1)
    def _():
        step((first_own - 1) * tile, tile, plain)

    for j in range(tq // tile):
        modes = []
        for p in range(n_parts):
            rows_p = q_parts[p].shape[0]
            if row0s[p] + rows_p - 1 < j * tile:
                modes.append("skip")
            elif row0s[p] >= (j + 1) * tile - 1:
                modes.append(None)
            else:
                modes.append(row0s[p] - j * tile)
        step(qi * tq + j * tile, tile, tuple(modes))


def _fill_vaug(qi, v_ref, vaug_ref):
    @pl.when(qi == 0)
    def _():
        vaug_ref[:, :LANES] = v_ref[...]
        vaug_ref[:, LANES:] = jnp.ones(v_ref.shape, BF16)


def _first_live_tile(first_own, decay_to_tile, cutoff):
    def cond(kt):
        return (kt < first_own) & (decay_to_tile(jnp.minimum(kt, first_own - 1)) > cutoff)
    return lax.while_loop(cond, lambda kt: kt + 1, jnp.int32(0))


def _flash_scratch(seq, rows):
    return [pltpu.VMEM((seq, 2 * LANES), BF16), pltpu.VMEM((rows, LANES), F32), pltpu.VMEM((rows, LANES), F32),
            pltpu.VMEM((rows, 2 * LANES), F32), pltpu.VMEM((rows, 2 * LANES), F32)]


def _fox_kernel(ftab_ref, cut_ref, q_ref, k_ref, v_ref, f_ref, o_ref, vaug_ref, m0_ref, m1_ref, acc0_ref,
                acc1_ref, *, tq, tile, nk):
    b = pl.program_id(0)
    hp = pl.program_id(1)
    qi = pl.program_id(2)
    qstart = pl.multiple_of(qi * tq, tq)
    first_own = qi * (tq // tile)
    half = tq // 2
    _fill_vaug(qi, v_ref, vaug_ref)
    q = q_ref[...]
    lane = lax.broadcasted_iota(jnp.int32, q.shape, 1)
    cutoff = cut_ref[0]

    def one_head(hh, carry):
        in_head = (lane >= hh * HEAD_DIM) & (lane < (hh + 1) * HEAD_DIM)
        qh = jnp.where(in_head, q, jnp.zeros_like(q))
        lane_half = lax.broadcasted_iota(jnp.int32, (half, LANES), 1)
        in_head_half = (lane_half >= hh * HEAD_DIM) & (lane_half < (hh + 1) * HEAD_DIM)

        def f_row(start, width):
            first = jnp.full((1, width), hh, jnp.int32) == 0
            return jnp.where(first, f_ref[0:1, pl.ds(start, width)], f_ref[1:2, pl.ds(start, width)])

        f_q0 = f_row(qstart, LANES)[:, 0:1]

        def bias(start, width):
            return LOG2E * (f_q0 - f_row(start, width))

        base = ((b * pl.num_programs(1) + hp) * 2 + hh) * nk

        def decay(kt):
            return (ftab_ref[base + kt + 1] - ftab_ref[base + first_own]) * LOG2E

        kstart = _first_live_tile(first_own, decay, cutoff)
        _flash_parts((qh[:half], qh[half:]), (0, half), k_ref, vaug_ref, bias, qi, kstart, tq, tile,
                     (m0_ref, m1_ref), (acc0_ref, acc1_ref))
        for p, acc_ref in enumerate((acc0_ref, acc1_ref)):
            rows = slice(p * half, (p + 1) * half)
            acc = acc_ref[...]
            o_h = acc[:, :LANES] / acc[:, LANES:]
            o_ref[rows, :] = jnp.where(in_head_half, o_h, o_ref[rows, :].astype(F32)).astype(BF16)
        return carry

    o_ref[...] = jnp.zeros_like(o_ref)
    lax.fori_loop(0, 2, one_head, 0)


def _zero_cutoff(q_gain, k_gain):
    bound = (HEAD_DIM * SCALE * LOG2E * 1.02) * jnp.max(jnp.abs(q_gain)) * jnp.max(jnp.abs(k_gain))
    return (150.0 + 2.0 * bound).astype(F32).reshape(1)


def _fox_attention(qkv, cum_t, cutoff, batch, seq):
    n, d3 = qkv.shape
    d = d3 // 3
    pairs = d // LANES
    tq, tile = FOX_Q_TILE, ATT_TILE
    nq = seq // tq
    ftab = cum_t[:, :, :, ::tile].reshape(-1)
    return pl.pallas_call(
        functools.partial(_fox_kernel, tq=tq, tile=tile, nk=seq // tile),
        grid_spec=pltpu.PrefetchScalarGridSpec(
            num_scalar_prefetch=2,
            grid=(batch, pairs, nq),
            in_specs=[pl.BlockSpec((tq, LANES), lambda b, h, i, ft, ct: (b * nq + i, h)),
                      pl.BlockSpec((seq, LANES), lambda b, h, i, ft, ct: (b, pairs + h)),
                      pl.BlockSpec((seq, LANES), lambda b, h, i, ft, ct: (b, 2 * pairs + h)),
                      pl.BlockSpec((None, None, 2, seq), lambda b, h, i, ft, ct: (b, h, 0, 0))],
            out_specs=pl.BlockSpec((tq, LANES), lambda b, h, i, ft, ct: (b * nq + i, h)),
            scratch_shapes=_flash_scratch(seq, tq // 2)),
        out_shape=jax.ShapeDtypeStruct((n, d), BF16),
        compiler_params=_cparams(("arbitrary", "arbitrary", "arbitrary")),
        name="fox_attention",
    )(ftab, cutoff, qkv, qkv, qkv, cum_t)


def _diff_kernel(slope_ref, cut_ref, q_ref, k_ref, v_ref, lam_ref, sub_ref, o_ref, vaug_ref, m0_ref, m1_ref,
                 acc0_ref, acc1_ref, *, tile, lam_init):
    h = pl.program_id(1)
    qi = pl.program_id(2)
    slope = slope_ref[h] * LOG2E
    qstart = qi * tile

    def bias(start, width):
        key_off = lax.broadcasted_iota(jnp.int32, (1, width), 1)
        return slope * (start - qstart + key_off).astype(F32)

    def decay(kt):
        return slope * (qstart - (kt + 1) * tile + 1).astype(F32)

    kstart = _first_live_tile(qi, decay, cut_ref[0])
    _fill_vaug(qi, v_ref, vaug_ref)
    q = q_ref[...]
    lane = lax.broadcasted_iota(jnp.int32, q.shape, 1)
    zero = jnp.zeros_like(q)
    q_parts = (jnp.where(lane < HEAD_DIM, q, zero), jnp.where(lane >= HEAD_DIM, q, zero))
    _flash_parts(q_parts, (0, 0), k_ref, vaug_ref, bias, qi, kstart, tile, tile, (m0_ref, m1_ref),
                 (acc0_ref, acc1_ref))
    acc0, acc1 = acc0_ref[...], acc1_ref[...]
    lp = lam_ref[...]
    lam = (jnp.exp(jnp.sum(lp[0:1] * lp[1:2], axis=-1, keepdims=True))
           - jnp.exp(jnp.sum(lp[2:3] * lp[3:4], axis=-1, keepdims=True)) + lam_init)
    o = acc0[:, :LANES] / acc0[:, LANES:] - lam * (acc1[:, :LANES] / acc1[:, LANES:])
    o_ref[...] = (_rms_rows(o, sub_ref[...]) * (1.0 - lam_init)).astype(BF16)


def _diff_attention(proj, lam_params, subln_g, cutoff, batch, seq, layer):
    n = proj.shape[0]
    tile = ATT_TILE
    nq = seq // tile
    lam_init = 0.8 - 0.6 * math.exp(-0.3 * layer)
    slopes = jnp.asarray([2.0 ** (-8.0 * (i + 1) / A_HEADS) for i in range(A_HEADS)], F32)
    imap = lambda f: (lambda b, h, i, s, c: f(b, h, i))
    return pl.pallas_call(
        functools.partial(_diff_kernel, tile=tile, lam_init=lam_init),
        grid_spec=pltpu.PrefetchScalarGridSpec(
            num_scalar_prefetch=2,
            grid=(batch, A_HEADS, nq),
            in_specs=[pl.BlockSpec((tile, LANES), imap(lambda b, h, i: (b * nq + i, h))),
                      pl.BlockSpec((seq, LANES), imap(lambda b, h, i: (b, A_HEADS + h))),
                      pl.BlockSpec((seq, LANES), imap(lambda b, h, i: (b, 2 * A_HEADS + h))),
                      pl.BlockSpec((4, HEAD_DIM), imap(lambda b, h, i: (0, 0))),
                      pl.BlockSpec((1, LANES), imap(lambda b, h, i: (0, 0)))],
            out_specs=pl.BlockSpec((tile, LANES), imap(lambda b, h, i: (b * nq + i, h))),
            scratch_shapes=_flash_scratch(seq, tile)),
        out_shape=jax.ShapeDtypeStruct((n, A_HEADS * LANES), BF16),
        compiler_params=_cparams(("arbitrary", "arbitrary", "arbitrary")),
        name="diff_attention",
    )(slopes, cutoff, proj, proj, proj, lam_params.astype(F32), subln_g.astype(F32).reshape(1, LANES))


def _stick_kernel(q_ref, k_ref, v_ref, ntri_ref, o_ref, *, tq, tk):
    qi = pl.program_id(2)
    ratio = tq // tk
    q = q_ref[...]
    lane = lax.broadcasted_iota(jnp.int32, q.shape, 1)
    zero = jnp.zeros_like(q)
    qs = (jnp.where(lane < HEAD_DIM, q, zero), jnp.where(lane >= HEAD_DIM, q, zero))
    rel = (lax.broadcasted_iota(jnp.int32, (tq, tk), 0)
           - lax.broadcasted_iota(jnp.int32, (tq, tk), 1))
    ntri = ntri_ref[...]

    def sweep(tiles, carry):
        loaded = []
        for kt, _ in tiles:
            start = pl.multiple_of(kt * tk, tk)
            loaded.append((k_ref[pl.ds(start, tk), :], v_ref[pl.ds(start, tk), :]))
        new = []
        for hh in range(2):
            run, acc = carry[hh]
            for (kt, offset), (k, v) in zip(tiles, loaded):
                z = _dot_nt(qs[hh], k)
                sp = _softplus(z)
                if offset is None:
                    drop = sp
                else:
                    before = rel > offset
                    drop = jnp.where(before, sp, 0.0)
                hi, lo = _split2(drop)
                later = _dot(hi, ntri) + _dot(lo, ntri)
                w = jnp.exp((z - sp) + later + run)
                if offset is not None:
                    w = jnp.where(before, w, 0.0)
                acc = acc + _dot(w.astype(BF16), v)
                run = run - jnp.sum(drop, axis=-1, keepdims=True)
            new.append((run, acc))
        return tuple(new)

    init_one = (jnp.zeros((tq, 1), F32), jnp.zeros((tq, LANES), F32))
    own = [(qi * ratio + d, (qi * ratio + d) * tk - qi * tq) for d in range(ratio - 1, -1, -1)]
    prev = jnp.maximum(qi * ratio - 1, 0)
    prev_offset = jnp.where(qi > 0, -tk, tq)
    carry = sweep(own + [(prev, prev_offset)], (init_one, init_one))

    def live(c):
        top = jnp.maximum(jnp.max(c[0][0]), jnp.max(c[1][0]))
        return (top > EXP_ZERO_BELOW).astype(jnp.int32)

    def cond(state):
        kt, alive, _ = state
        return (kt >= 0) & (alive > 0)

    def body(state):
        kt, _, c = state
        c = sweep([(kt, None)], c)
        return kt - 1, live(c), c

    _, _, carry = lax.while_loop(cond, body, (qi * ratio - 2, live(carry), carry))
    o_ref[...] = jnp.where(lane < HEAD_DIM, carry[0][1], carry[1][1]).astype(BF16)


def _stick_attention(proj, batch, seq, col0):
    n = proj.shape[0]
    pairs = 4
    tq, tk = SB_Q_TILE, SB_K_TILE
    nq = seq // tq
    ntri = -(jnp.arange(tk)[:, None] > jnp.arange(tk)[None, :]).astype(BF16)
    return pl.pallas_call(
        functools.partial(_stick_kernel, tq=tq, tk=tk),
        grid=(batch, pairs, nq),
        in_specs=[pl.BlockSpec((tq, LANES), lambda b, h, i: (b * nq + i, col0 + h)),
                  pl.BlockSpec((seq, LANES), lambda b, h, i: (b, col0 + pairs + h)),
                  pl.BlockSpec((seq, LANES), lambda b, h, i: (b, col0 + 2 * pairs + h)),
                  pl.BlockSpec((tk, tk), lambda b, h, i: (0, 0))],
        out_specs=pl.BlockSpec((tq, LANES), lambda b, h, i: (b * nq + i, h)),
        out_shape=jax.ShapeDtypeStruct((n, pairs * LANES), BF16),
        compiler_params=_cparams(("arbitrary", "arbitrary", "arbitrary")),
        name="stick_attention",
    )(proj, proj, proj, ntri)


def _out_router_kernel(*refs, n_parts):
    x_ref = refs[0]
    o_refs = refs[1:1 + n_parts]
    w_refs = refs[1 + n_parts:1 + 2 * n_parts]
    (g_ref, wr_hi_ref, wr_lo_ref, br_ref, ltri_ref, xo_ref, h_ref, route_ref, count_ref,
     seen_ref) = refs[1 + 2 * n_parts:]
    mix = _dot(o_refs[0][...], w_refs[0][...])
    for p in range(1, n_parts):
        mix = mix + _dot(o_refs[p][...], w_refs[p][...])
    x = x_ref[...] + mix
    xo_ref[...] = x
    h = _rms_rows(x, g_ref[...])
    h_ref[...] = h

    h_hi, h_lo = _split2(h)
    logits = (_dot(h_hi, wr_hi_ref[...]) + _dot(h_hi, wr_lo_ref[...]) + _dot(h_lo, wr_hi_ref[...])
              + br_ref[...])
    lane = lax.broadcasted_iota(jnp.int32, logits.shape, 1)

    def first_max(vals):
        top = jnp.max(vals, axis=-1, keepdims=True)
        idx = jnp.min(jnp.where(vals == top, lane, LANES), axis=-1, keepdims=True)
        return top, idx

    g_logits = jnp.where(lane < N_GROUPS, logits, NEG)
    g_top, g_idx = first_max(g_logits)
    g_w = 1.0 / jnp.sum(jnp.exp(g_logits - g_top), axis=-1, keepdims=True)
    lo_lane = N_GROUPS + EXPERTS_PER_GROUP * g_idx
    in_group = (lane >= lo_lane) & (lane < lo_lane + EXPERTS_PER_GROUP)
    e_logits = jnp.where(in_group, logits, NEG)
    v1, i1 = first_max(e_logits)
    v2, i2 = first_max(jnp.where(lane == i1, NEG, e_logits))
    e2 = jnp.exp(v2 - v1)
    w1 = g_w / (1.0 + e2)
    w2 = g_w * e2 / (1.0 + e2)

    @pl.when(pl.program_id(0) == 0)
    def _():
        seen_ref[...] = jnp.zeros_like(seen_ref)

    uses = ((lane == i1) | (lane == i2)).astype(BF16)
    earlier = _dot(ltri_ref[...], uses) + seen_ref[...]
    r1 = jnp.sum(jnp.where(lane == i1, earlier, 0.0), axis=-1, keepdims=True)
    r2 = jnp.sum(jnp.where(lane == i2, earlier, 0.0), axis=-1, keepdims=True)
    seen = earlier[-1:, :] + uses[-1:, :].astype(F32)
    seen_ref[...] = seen
    count_ref[...] = jnp.broadcast_to(seen, count_ref.shape)

    cols = ((i1 - N_GROUPS).astype(F32), (i2 - N_GROUPS).astype(F32), w1, w2, r1, r2)
    route = jnp.zeros(logits.shape, F32)
    for c, val in enumerate(cols):
        route = jnp.where(lane == c, val, route)
    route_ref[...] = route


def _out_router(x, o_parts, w_out, g_ffn, w_group, b_group, w_router, b_router):
    n, d = x.shape
    n_parts = len(o_parts)
    w_bf = w_out.astype(BF16)
    w_parts, r0 = [], 0
    for o in o_parts:
        w_parts.append(w_bf[r0:r0 + o.shape[1]])
        r0 += o.shape[1]
    wr = jnp.zeros((d, LANES), F32).at[:, :N_GROUPS].set(w_group)
    wr = wr.at[:, N_GROUPS:N_GROUPS + N_EXPERTS].set(w_router)
    wr_hi = wr.astype(BF16)
    wr_lo = (wr - wr_hi.astype(F32)).astype(BF16)
    br = jnp.zeros((1, LANES), F32).at[0, :N_GROUPS].set(b_group.astype(F32))
    br = br.at[0, N_GROUPS:N_GROUPS + N_EXPERTS].set(b_router.astype(F32))
    ltri = (jnp.arange(ROW_TILE)[:, None] > jnp.arange(ROW_TILE)[None, :]).astype(BF16)
    rows = lambda width: pl.BlockSpec((ROW_TILE, width), lambda i: (i, 0))
    const = lambda shape: pl.BlockSpec(shape, lambda i: (0, 0))
    return pl.pallas_call(
        functools.partial(_out_router_kernel, n_parts=n_parts),
        grid=(n // ROW_TILE,),
        in_specs=([rows(d)] + [rows(o.shape[1]) for o in o_parts]
                  + [const(w.shape) for w in w_parts]
                  + [const((1, d)), const((d, LANES)), const((d, LANES)), const((1, LANES)),
                     const((ROW_TILE, ROW_TILE))]),
        out_specs=[rows(d), rows(d), rows(LANES), const((8, LANES))],
        out_shape=[jax.ShapeDtypeStruct((n, d), F32), jax.ShapeDtypeStruct((n, d), F32),
                   jax.ShapeDtypeStruct((n, LANES), F32), jax.ShapeDtypeStruct((8, LANES), F32)],
        scratch_shapes=[pltpu.VMEM((1, LANES), F32)],
        compiler_params=_cparams(("arbitrary",)),
        name="out_router",
    )(x, *o_parts, *w_parts, g_ffn.reshape(1, d), wr_hi, wr_lo, br, ltri)


def _dispatch_plan(route, counts_row, n_blocks):
    n = route.shape[0]
    e_flat = route[:, 0:2].astype(jnp.int32).reshape(-1)
    rank = route[:, 4:6].astype(jnp.int32).reshape(-1)
    counts = counts_row[N_GROUPS:N_GROUPS + N_EXPERTS].astype(jnp.int32)
    padded = (counts + MOE_TILE - 1) // MOE_TILE * MOE_TILE
    pad_ends = jnp.cumsum(padded)
    dest = ((pad_ends - padded)[e_flat] + rank).astype(jnp.int32)
    p = n_blocks * MOE_TILE
    tok_buf = jnp.zeros((p,), jnp.int32).at[dest].set(jnp.arange(2 * n, dtype=jnp.int32) // 2)
    blk_start = jnp.arange(n_blocks, dtype=jnp.int32) * MOE_TILE
    blk_e = jnp.sum((pad_ends[None, :] <= blk_start[:, None]).astype(jnp.int32), axis=1)
    blk_e = jnp.minimum(blk_e, N_EXPERTS - 1).astype(jnp.int32)
    n_used = (pad_ends[-1:] // MOE_TILE).astype(jnp.int32)
    return dest, tok_buf, blk_e, n_used


def _moe_kernel(blk_e_ref, tok_ref, nused_ref, h_hbm, wg_ref, wu_ref, wd_ref, y_ref, xbuf, wg_bf, wu_bf, wd_bf, sem):
    i = pl.program_id(0)
    last = pl.num_programs(0) - 1
    slot = i % MOE_SLOTS

    def start_rows(blk, slot_, row_lo=0, row_hi=MOE_TILE):
        for r in range(row_lo, row_hi):
            tok = tok_ref[blk * MOE_TILE + r]
            pltpu.make_async_copy(h_hbm.at[pl.ds(tok, 1), :], xbuf.at[slot_, pl.ds(r, 1), :],
                                  sem.at[slot_]).start(priority=r % 2)

    def wait_rows(slot_):
        pltpu.make_async_copy(h_hbm.at[pl.ds(0, MOE_TILE), :], xbuf.at[slot_], sem.at[slot_]).wait()

    @pl.when(i == 0)
    def _():
        for ahead in range(MOE_SLOTS - 1):
            start_rows(jnp.minimum(ahead, last), ahead)

    @pl.when((i == 0) | (blk_e_ref[i] != blk_e_ref[jnp.maximum(i - 1, 0)]))
    def _():
        wg_bf[...] = wg_ref[...].astype(BF16)
        wu_bf[...] = wu_ref[...].astype(BF16)
        wd_bf[...] = wd_ref[...].astype(BF16)

    n_used = nused_ref[0]

    @pl.when(i < n_used)
    def _():
        wait_rows(slot)
        xb = xbuf[slot].astype(BF16)
        nxt, nxt_slot = jnp.minimum(i + MOE_SLOTS - 1, last), (i + MOE_SLOTS - 1) % MOE_SLOTS
        cut1, cut2 = MOE_TILE // 3, 2 * MOE_TILE // 3
        start_rows(nxt, nxt_slot, 0, cut1)
        gate = _dot(xb, wg_bf[...])
        start_rows(nxt, nxt_slot, cut1, cut2)
        up = _dot(xb, wu_bf[...])
        act = (gate * jax.nn.sigmoid(gate) * up).astype(BF16)
        start_rows(nxt, nxt_slot, cut2, MOE_TILE)
        y_ref[...] = _dot(act, wd_bf[...])

        @pl.when(i == n_used - 1)
        def _():
            for ahead in range(1, MOE_SLOTS):
                wait_rows((i + ahead) % MOE_SLOTS)

    @pl.when(i >= n_used)
    def _():
        y_ref[...] = jnp.zeros_like(y_ref)


def _moe_experts(h, tok_buf, blk_e, n_used, w_gate, w_up, w_down, layer, n_blocks):
    n, d = h.shape
    f = w_gate.shape[-1]
    p = n_blocks * MOE_TILE
    return pl.pallas_call(
        _moe_kernel,
        grid_spec=pltpu.PrefetchScalarGridSpec(
            num_scalar_prefetch=3,
            grid=(n_blocks,),
            in_specs=[pl.BlockSpec(memory_space=pl.ANY),
                      pl.BlockSpec((None, None, d, f), lambda i, be, tk, nu: (layer, be[i], 0, 0)),
                      pl.BlockSpec((None, None, d, f), lambda i, be, tk, nu: (layer, be[i], 0, 0)),
                      pl.BlockSpec((None, None, f, d), lambda i, be, tk, nu: (layer, be[i], 0, 0))],
            out_specs=pl.BlockSpec((MOE_TILE, d), lambda i, be, tk, nu: (i, 0)),
            scratch_shapes=[pltpu.VMEM((MOE_SLOTS, MOE_TILE, d), F32), pltpu.VMEM((d, f), BF16),
                            pltpu.VMEM((d, f), BF16), pltpu.VMEM((f, d), BF16),
                            pltpu.SemaphoreType.DMA((MOE_SLOTS,))]),
        out_shape=jax.ShapeDtypeStruct((p, d), F32),
        compiler_params=_cparams(("arbitrary",)),
        name="moe_experts",
    )(blk_e, tok_buf, n_used, h, w_gate, w_up, w_down)


def _combine_kernel(dest_ref, x_ref, route_ref, y_hbm, o_ref, ybuf, sem):
    i = pl.program_id(0)
    nsteps = pl.num_programs(0)
    slot = i % 2

    def start_gather(step, slot_):
        def body(r, c):
            for k in range(2):
                src = dest_ref[(step * CMB_TILE + r) * 2 + k]
                pltpu.make_async_copy(y_hbm.at[pl.ds(src, 1), :], ybuf.at[slot_, k, pl.ds(r, 1), :],
                                      sem.at[slot_]).start(priority=k)
            return c
        lax.fori_loop(0, CMB_TILE, body, 0, unroll=4)

    @pl.when(i == 0)
    def _():
        start_gather(0, 0)

    @pl.when(i + 1 < nsteps)
    def _():
        start_gather(i + 1, 1 - slot)

    for k in range(2):
        pltpu.make_async_copy(y_hbm.at[pl.ds(0, CMB_TILE), :], ybuf.at[slot, k], sem.at[slot]).wait()
    route = route_ref[...]
    o_ref[...] = x_ref[...] + (ybuf[slot, 0] * route[:, 2:3] + ybuf[slot, 1] * route[:, 3:4])


def _combine(x, y_buf, dest, route):
    n, d = x.shape
    return pl.pallas_call(
        _combine_kernel,
        grid_spec=pltpu.PrefetchScalarGridSpec(
            num_scalar_prefetch=1,
            grid=(n // CMB_TILE,),
            in_specs=[pl.BlockSpec((CMB_TILE, d), lambda i, ds: (i, 0)),
                      pl.BlockSpec((CMB_TILE, LANES), lambda i, ds: (i, 0)),
                      pl.BlockSpec(memory_space=pl.ANY)],
            out_specs=pl.BlockSpec((CMB_TILE, d), lambda i, ds: (i, 0)),
            scratch_shapes=[pltpu.VMEM((2, 2, CMB_TILE, d), F32), pltpu.SemaphoreType.DMA((2,))]),
        out_shape=jax.ShapeDtypeStruct((n, d), F32),
        compiler_params=_cparams(("arbitrary",)),
        name="moe_combine",
    )(dest, x, route, y_buf)


def kernel(x, norm_mix_g, norm_ffn_g, ab_w_in, a_q_gain, a_k_gain, a_lam_q1, a_lam_k1, a_lam_q2,
           a_lam_k2, a_subln_g, ab_w_out, c_w_in, c_b_f, c_q_gain, c_k_gain, c_w_out, moe_w_group,
           moe_b_group, moe_w_router, moe_b_router, moe_w_gate, moe_w_up, moe_w_down):
    batch, seq, d = x.shape
    n = batch * seq
    depth = norm_mix_g.shape[0]
    heads = d // HEAD_DIM
    n_blocks = (2 * n) // MOE_TILE + N_EXPERTS
    xs = x.reshape(n, d).astype(F32)
    moe = None
    for layer in range(depth):
        i = layer // 2
        if layer % 2 == 0:
            outs = _even_in_proj(xs, norm_mix_g[layer], ab_w_in[i], a_q_gain[i], a_k_gain[i], moe)
            proj = outs[0]
            xs = outs[-1] if moe is not None else xs
            lam_params = jnp.stack([a_lam_q1[i], a_lam_k1[i], a_lam_q2[i], a_lam_k2[i]])
            o_a = _diff_attention(proj, lam_params, a_subln_g[i], _zero_cutoff(a_q_gain[i], a_k_gain[i]),
                                  batch, seq, layer)
            o_b = _stick_attention(proj, batch, seq, 3 * A_HEADS)
            o_parts, w_out = [o_a, o_b], ab_w_out[i]
        else:
            outs = _odd_in_proj(xs, norm_mix_g[layer], c_w_in[i], c_b_f[i], c_q_gain[i], c_k_gain[i], seq,
                                moe)
            qkv, cum = outs[0], outs[1]
            xs = outs[-1] if moe is not None else xs
            cum_t = cum[:, :heads].reshape(batch, seq, heads // 2, 2).transpose(0, 2, 3, 1)
            cutoff = _zero_cutoff(c_q_gain[i], c_k_gain[i])
            o_parts, w_out = [_fox_attention(qkv, cum_t, cutoff, batch, seq)], c_w_out[i]
        xs, h, route, counts = _out_router(xs, o_parts, w_out, norm_ffn_g[layer], moe_w_group[layer],
                                           moe_b_group[layer], moe_w_router[layer], moe_b_router[layer])
        dest, tok_buf, blk_e, n_used = _dispatch_plan(route, counts[0], n_blocks)
        y_buf = _moe_experts(h, tok_buf, blk_e, n_used, moe_w_gate, moe_w_up, moe_w_down, layer, n_blocks)
        moe = (y_buf, dest, route)
    xs = _combine(xs, *moe)
    return xs.reshape(batch, seq, d).astype(x.dtype)
```

```python
import functools
import math

import jax
import jax.numpy as jnp
from jax import lax
from jax.experimental import pallas as pl
from jax.experimental.pallas import tpu as pltpu

F32 = jnp.float32
BF16 = jnp.bfloat16

HEAD_DIM = 64
LANES = 128
EPS = 1e-6
SCALE = HEAD_DIM ** -0.5
N_GROUPS = 4
EXPERTS_PER_GROUP = 4
N_EXPERTS = N_GROUPS * EXPERTS_PER_GROUP
A_HEADS = 4
NEG = -1e30
LOG2E = math.log2(math.e)

ROW_TILE = 512
ATT_TILE = 512
FOX_Q_TILE = 1024
SB_Q_TILE = 512
SB_K_TILE = 256
EXP_ZERO_BELOW = -104.0
MOE_TILE = 256
MOE_SLOTS = 3
CMB_TILE = 256
SM_ROWS = 32
SM_WIDTH = 1024
SEC = 512
IN_SECTIONS = 6
VMEM_LIMIT = 48 * 1024 * 1024


def _cparams(sem):
    return pltpu.CompilerParams(dimension_semantics=sem, vmem_limit_bytes=VMEM_LIMIT)


def _dot(a, b):
    return jnp.dot(a, b, preferred_element_type=F32)


def _dot_nt(a, b):
    return lax.dot_general(a, b, (((1,), (1,)), ((), ())), preferred_element_type=F32)


def _split2(x):
    hi = x.astype(BF16)
    lo = (x - hi.astype(F32)).astype(BF16)
    return hi, lo


def _rms_rows(x, g):
    return x * lax.rsqrt(jnp.mean(x * x, axis=-1, keepdims=True) + EPS) * g


def _group_rsqrt(acc, e_ref, et_ref):
    hi, lo = _split2(acc * acc)
    ss = _dot(hi, e_ref[...]) + _dot(lo, e_ref[...])
    rhi, rlo = _split2(lax.rsqrt(ss * (1.0 / HEAD_DIM) + EPS))
    return _dot(rhi, et_ref[...]) + _dot(rlo, et_ref[...])


def _softplus(z):
    return jnp.maximum(z, 0.0) + jnp.log(1.0 + jnp.exp(-jnp.abs(z)))


def _add_expert_rows(dest_ref, x_ref, route_ref, y_hbm, xo_ref, ybuf, sem):
    i = pl.program_id(0)
    last = pl.num_programs(0) - 1
    slot = i % 2

    def start_rows(step, slot_, row_lo=0, row_hi=ROW_TILE):
        for r in range(row_lo, row_hi):
            for k in range(2):
                src = dest_ref[(step * ROW_TILE + r) * 2 + k]
                pltpu.make_async_copy(y_hbm.at[pl.ds(src, 1), :], ybuf.at[slot_, k, pl.ds(r, 1), :],
                                      sem.at[slot_]).start(priority=k)

    def wait_rows(slot_):
        for k in range(2):
            pltpu.make_async_copy(y_hbm.at[pl.ds(0, ROW_TILE), :], ybuf.at[slot_, k], sem.at[slot_]).wait()

    @pl.when(i == 0)
    def _():
        start_rows(0, 0)

    wait_rows(slot)
    route = route_ref[...]
    x = x_ref[...] + (ybuf[slot, 0] * route[:, 2:3] + ybuf[slot, 1] * route[:, 3:4])
    xo_ref[...] = x
    def issue(j):
        start_rows(jnp.minimum(i + 1, last), 1 - slot, j * ROW_TILE // IN_SECTIONS,
                   (j + 1) * ROW_TILE // IN_SECTIONS)

    def drain():
        @pl.when(i == last)
        def _():
            wait_rows(1 - slot)

    return x, issue, drain


def _even_in_kernel(*refs, fused):
    if fused:
        (dest_ref, x_ref, route_ref, y_hbm, g_ref, w_ref, gq_ref, gk_ref, e_ref, et_ref, o_ref, xo_ref,
         ybuf, sem) = refs
        x, issue, drain = _add_expert_rows(dest_ref, x_ref, route_ref, y_hbm, xo_ref, ybuf, sem)
    else:
        x_ref, g_ref, w_ref, gq_ref, gk_ref, e_ref, et_ref, o_ref = refs
        x = x_ref[...]
    xn = _rms_rows(x, g_ref[...]).astype(BF16)
    for sec in range(IN_SECTIONS):
        if fused:
            issue(sec)
        cols = slice(sec * SEC, (sec + 1) * SEC)
        acc = _dot(xn, w_ref[:, cols])
        if sec == 0:
            acc = acc * _group_rsqrt(acc, e_ref, et_ref) * gq_ref[...]
        elif sec == 1:
            acc = acc * _group_rsqrt(acc, e_ref, et_ref) * gk_ref[...]
        elif sec == 3:
            acc = acc * SCALE
        o_ref[:, cols] = acc.astype(BF16)
    if fused:
        drain()


def _odd_in_kernel(*refs, tiles_per_seq, fused):
    if fused:
        (dest_ref, x_ref, route_ref, y_hbm, g_ref, w_ref, wf_ref, bf_ref, gq_ref, gk_ref, e_ref, et_ref,
         tri_ref, o_ref, cum_ref, xo_ref, carry_ref, ybuf, sem) = refs
        x, issue, drain = _add_expert_rows(dest_ref, x_ref, route_ref, y_hbm, xo_ref, ybuf, sem)
    else:
        (x_ref, g_ref, w_ref, wf_ref, bf_ref, gq_ref, gk_ref, e_ref, et_ref, tri_ref, o_ref, cum_ref,
         carry_ref) = refs
        x = x_ref[...]
    i = pl.program_id(0)
    xn = _rms_rows(x, g_ref[...]).astype(BF16)
    for sec in range(IN_SECTIONS):
        if fused:
            issue(sec)
        cols = slice(sec * SEC, (sec + 1) * SEC)
        acc = _dot(xn, w_ref[:, cols])
        if sec < 2:
            acc = acc * _group_rsqrt(acc, e_ref, et_ref) * gq_ref[...]
        elif sec < 4:
            acc = acc * _group_rsqrt(acc, e_ref, et_ref) * gk_ref[...]
        o_ref[:, cols] = acc.astype(BF16)

    log_f = -_softplus(-(_dot(xn, wf_ref[...]) + bf_ref[...]))
    p1 = log_f.astype(BF16)
    r1 = log_f - p1.astype(F32)
    p2 = r1.astype(BF16)
    p3 = (r1 - p2.astype(F32)).astype(BF16)
    tri = tri_ref[...]
    within = _dot(tri, p1) + _dot(tri, p2) + _dot(tri, p3)

    @pl.when(i % tiles_per_seq == 0)
    def _():
        carry_ref[...] = jnp.zeros_like(carry_ref)

    cum = within + carry_ref[...]
    cum_ref[...] = cum
    carry_ref[...] = cum[-1:, :]
    if fused:
        drain()


def _group_maps():
    lane_grp = jnp.arange(SEC)[:, None] // HEAD_DIM
    e = (lane_grp == jnp.arange(LANES)[None, :]).astype(BF16)
    return e, e.T


def _in_proj_call(kernel_fn, name, x, moe, in_arrays, in_shapes, out_specs_fn, out_shapes, scratch):
    n, d = x.shape
    rows = lambda width: pl.BlockSpec((ROW_TILE, width), lambda i, *_: (i, 0))
    const = lambda shape: pl.BlockSpec(shape, lambda i, *_: (0,) * len(shape))
    in_specs = [rows(d)] + [const(s) for s in in_shapes]
    out_specs = out_specs_fn(rows)
    args = [x] + list(in_arrays)
    prefetch = []
    if moe is not None:
        y_buf, dest, route = moe
        in_specs = [rows(d), rows(LANES), pl.BlockSpec(memory_space=pl.ANY)] + in_specs[1:]
        args = [x, route, y_buf] + list(in_arrays)
        prefetch = [dest]
        out_specs = out_specs + [rows(d)]
        out_shapes = out_shapes + [jax.ShapeDtypeStruct((n, d), F32)]
        scratch = scratch + [pltpu.VMEM((2, 2, ROW_TILE, d), F32), pltpu.SemaphoreType.DMA((2,))]
    return pl.pallas_call(
        functools.partial(kernel_fn, fused=moe is not None),
        grid_spec=pltpu.PrefetchScalarGridSpec(
            num_scalar_prefetch=len(prefetch), grid=(n // ROW_TILE,), in_specs=in_specs,
            out_specs=out_specs, scratch_shapes=scratch),
        out_shape=out_shapes,
        compiler_params=_cparams(("arbitrary",)),
        name=name,
    )(*prefetch, *args)


def _even_in_proj(x, g, w_in, q_gain, k_gain, moe=None):
    n, d = x.shape
    e, et = _group_maps()
    reps = SEC // HEAD_DIM
    gq = jnp.tile(q_gain.astype(F32) * (SCALE * LOG2E), reps).reshape(1, SEC)
    gk = jnp.tile(k_gain.astype(F32), reps).reshape(1, SEC)
    return _in_proj_call(
        _even_in_kernel, "even_in_proj", x, moe,
        [g.reshape(1, d), w_in.astype(BF16), gq, gk, e, et],
        [(1, d), (d, 3 * d), (1, SEC), (1, SEC), (SEC, LANES), (LANES, SEC)],
        lambda rows: [rows(3 * d)], [jax.ShapeDtypeStruct((n, 3 * d), BF16)], [])


def _odd_in_proj(x, g, w_in, b_f, q_gain, k_gain, seq, moe=None):
    n, d = x.shape
    heads = d // HEAD_DIM
    e, et = _group_maps()
    reps = SEC // HEAD_DIM
    gq = jnp.tile(q_gain.astype(F32) * (SCALE * LOG2E), reps).reshape(1, SEC)
    gk = jnp.tile(k_gain.astype(F32), reps).reshape(1, SEC)
    wf = jnp.zeros((d, LANES), F32).at[:, :heads].set(w_in[:, 3 * d:]).astype(BF16)
    bf = jnp.zeros((1, LANES), F32).at[0, :heads].set(b_f.astype(F32))
    tri = (jnp.arange(ROW_TILE)[:, None] >= jnp.arange(ROW_TILE)[None, :]).astype(BF16)
    return _in_proj_call(
        functools.partial(_odd_in_kernel, tiles_per_seq=seq // ROW_TILE), "odd_in_proj", x, moe,
        [g.reshape(1, d), w_in[:, :3 * d].astype(BF16), wf, bf, gq, gk, e, et, tri],
        [(1, d), (d, 3 * d), (d, LANES), (1, LANES), (1, SEC), (1, SEC), (SEC, LANES), (LANES, SEC),
         (ROW_TILE, ROW_TILE)],
        lambda rows: [rows(3 * d), rows(LANES)],
        [jax.ShapeDtypeStruct((n, 3 * d), BF16), jax.ShapeDtypeStruct((n, LANES), F32)],
        [pltpu.VMEM((1, LANES), F32)])


def _flash_parts(q_parts, row0s, k_ref, vaug_ref, bias_fn, qi, kstart, tq, tile, m_refs, acc_refs):
    n_parts = len(q_parts)
    for p in range(n_parts):
        m_refs[p][...] = jnp.full(m_refs[p].shape, NEG, F32)
        acc_refs[p][...] = jnp.zeros(acc_refs[p].shape, F32)

    def softmax_rows(s, brow, m_all, diag_off):
        rows_p, width = s.shape
        chunk = SM_ROWS if width <= SM_WIDTH else SM_ROWS // 2
        if diag_off is not None:
            rel = (lax.broadcasted_iota(jnp.int32, (chunk, width), 0)
                   - lax.broadcasted_iota(jnp.int32, (chunk, width), 1))
        m_news, chunks = [], []
        for c in range(rows_p // chunk):
            rows = slice(c * chunk, (c + 1) * chunk)
            sc = s[rows] + brow
            if diag_off is not None:
                sc = jnp.where(rel >= -(diag_off + c * chunk), sc, NEG)
            blocks = [sc[:, j * LANES:(j + 1) * LANES] for j in range(width // LANES)]
            top = functools.reduce(jnp.maximum, blocks)
            m_new = jnp.maximum(m_all[rows], jnp.max(top, axis=-1, keepdims=True))
            m_news.append(m_new)
            chunks.append(jnp.concatenate([jnp.exp2((b - m_new).astype(BF16)) for b in blocks], axis=1))
        return jnp.concatenate(m_news, axis=0), jnp.concatenate(chunks, axis=0)

    def step(key_start, width, modes):
        key_start = pl.multiple_of(key_start, tile)
        k = k_ref[pl.ds(key_start, width), :]
        va = vaug_ref[pl.ds(key_start, width), :]
        brow = bias_fn(key_start, width)
        live = [p for p in range(n_parts) if modes[p] != "skip"]
        s = {p: _dot_nt(q_parts[p], k) for p in live}
        m_old = {p: m_refs[p][...] for p in live}
        acc_old = {p: acc_refs[p][...] for p in live}
        m_new, acc_new = {}, {}
        for p in live:
            m_new[p], w = softmax_rows(s[p], brow, m_old[p], modes[p])
            alpha = jnp.exp2(m_old[p] - m_new[p])
            pv = _dot(w, va)
            acc_new[p] = jnp.concatenate([alpha * acc_old[p][:, :LANES] + pv[:, :LANES],
                                          alpha * acc_old[p][:, LANES:] + pv[:, LANES:]], axis=1)
        for p in live:
            m_refs[p][...] = m_new[p]
            acc_refs[p][...] = acc_new[p]

    plain = (None,) * n_parts
    first_own = qi * (tq // tile)
    n_plain = first_own - kstart
    n_quads = n_plain // 4
    rest = n_plain - 4 * n_quads

    def quad_body(j, c):
        step((kstart + 4 * j) * tile, 4 * tile, plain)
        return c

    lax.fori_loop(0, n_quads, quad_body, 0)

    @pl.when(rest >= 2)
    def _():
        step((kstart + 4 * n_quads) * tile, 2 * tile, plain)

    @pl.when(rest % 2 == 1)
    def _():
        step((first_own - 1) * tile, tile, plain)

    for j in range(tq // tile):
        modes = []
        for p in range(n_parts):
            rows_p = q_parts[p].shape[0]
            if row0s[p] + rows_p - 1 < j * tile:
                modes.append("skip")
            elif row0s[p] >= (j + 1) * tile - 1:
                modes.append(None)
            else:
                modes.append(row0s[p] - j * tile)
        step(qi * tq + j * tile, tile, tuple(modes))


def _fill_vaug(qi, v_ref, vaug_ref):
    @pl.when(qi == 0)
    def _():
        vaug_ref[:, :LANES] = v_ref[...]
        vaug_ref[:, LANES:] = jnp.ones(v_ref.shape, BF16)


def _first_live_tile(first_own, decay_to_tile, cutoff):
    def cond(kt):
        return (kt < first_own) & (decay_to_tile(jnp.minimum(kt, first_own - 1)) > cutoff)
    return lax.while_loop(cond, lambda kt: kt + 1, jnp.int32(0))


def _flash_scratch(seq, rows):
    return [pltpu.VMEM((seq, 2 * LANES), BF16), pltpu.VMEM((rows, LANES), F32), pltpu.VMEM((rows, LANES), F32),
            pltpu.VMEM((rows, 2 * LANES), F32), pltpu.VMEM((rows, 2 * LANES), F32)]


def _fox_kernel(ftab_ref, cut_ref, q_ref, k_ref, v_ref, f_ref, o_ref, vaug_ref, m0_ref, m1_ref, acc0_ref,
                acc1_ref, *, tq, tile, nk):
    b = pl.program_id(0)
    hp = pl.program_id(1)
    qi = pl.program_id(2)
    qstart = pl.multiple_of(qi * tq, tq)
    first_own = qi * (tq // tile)
    half = tq // 2
    _fill_vaug(qi, v_ref, vaug_ref)
    q = q_ref[...]
    lane = lax.broadcasted_iota(jnp.int32, q.shape, 1)
    cutoff = cut_ref[0]

    def one_head(hh, carry):
        in_head = (lane >= hh * HEAD_DIM) & (lane < (hh + 1) * HEAD_DIM)
        qh = jnp.where(in_head, q, jnp.zeros_like(q))
        lane_half = lax.broadcasted_iota(jnp.int32, (half, LANES), 1)
        in_head_half = (lane_half >= hh * HEAD_DIM) & (lane_half < (hh + 1) * HEAD_DIM)

        def f_row(start, width):
            first = jnp.full((1, width), hh, jnp.int32) == 0
            return jnp.where(first, f_ref[0:1, pl.ds(start, width)], f_ref[1:2, pl.ds(start, width)])

        f_q0 = f_row(qstart, LANES)[:, 0:1]

        def bias(start, width):
            return LOG2E * (f_q0 - f_row(start, width))

        base = ((b * pl.num_programs(1) + hp) * 2 + hh) * nk

        def decay(kt):
            return (ftab_ref[base + kt + 1] - ftab_ref[base + first_own]) * LOG2E

        kstart = _first_live_tile(first_own, decay, cutoff)
        _flash_parts((qh[:half], qh[half:]), (0, half), k_ref, vaug_ref, bias, qi, kstart, tq, tile,
                     (m0_ref, m1_ref), (acc0_ref, acc1_ref))
        for p, acc_ref in enumerate((acc0_ref, acc1_ref)):
            rows = slice(p * half, (p + 1) * half)
            acc = acc_ref[...]
            o_h = acc[:, :LANES] / acc[:, LANES:]
            o_ref[rows, :] = jnp.where(in_head_half, o_h, o_ref[rows, :].astype(F32)).astype(BF16)
        return carry

    o_ref[...] = jnp.zeros_like(o_ref)
    lax.fori_loop(0, 2, one_head, 0)


def _zero_cutoff(q_gain, k_gain):
    bound = (HEAD_DIM * SCALE * LOG2E * 1.02) * jnp.max(jnp.abs(q_gain)) * jnp.max(jnp.abs(k_gain))
    return (150.0 + 2.0 * bound).astype(F32).reshape(1)


def _fox_attention(qkv, cum_t, cutoff, batch, seq):
    n, d3 = qkv.shape
    d = d3 // 3
    pairs = d // LANES
    tq, tile = FOX_Q_TILE, ATT_TILE
    nq = seq // tq
    ftab = cum_t[:, :, :, ::tile].reshape(-1)
    return pl.pallas_call(
        functools.partial(_fox_kernel, tq=tq, tile=tile, nk=seq // tile),
        grid_spec=pltpu.PrefetchScalarGridSpec(
            num_scalar_prefetch=2,
            grid=(batch, pairs, nq),
            in_specs=[pl.BlockSpec((tq, LANES), lambda b, h, i, ft, ct: (b * nq + i, h)),
                      pl.BlockSpec((seq, LANES), lambda b, h, i, ft, ct: (b, pairs + h)),
                      pl.BlockSpec((seq, LANES), lambda b, h, i, ft, ct: (b, 2 * pairs + h)),
                      pl.BlockSpec((None, None, 2, seq), lambda b, h, i, ft, ct: (b, h, 0, 0))],
            out_specs=pl.BlockSpec((tq, LANES), lambda b, h, i, ft, ct: (b * nq + i, h)),
            scratch_shapes=_flash_scratch(seq, tq // 2)),
        out_shape=jax.ShapeDtypeStruct((n, d), BF16),
        compiler_params=_cparams(("arbitrary", "arbitrary", "arbitrary")),
        name="fox_attention",
    )(ftab, cutoff, qkv, qkv, qkv, cum_t)


def _diff_kernel(slope_ref, cut_ref, q_ref, k_ref, v_ref, lam_ref, sub_ref, o_ref, vaug_ref, m0_ref, m1_ref,
                 acc0_ref, acc1_ref, *, tile, lam_init):
    h = pl.program_id(1)
    qi = pl.program_id(2)
    slope = slope_ref[h] * LOG2E
    qstart = qi * tile

    def bias(start, width):
        key_off = lax.broadcasted_iota(jnp.int32, (1, width), 1)
        return slope * (start - qstart + key_off).astype(F32)

    def decay(kt):
        return slope * (qstart - (kt + 1) * tile + 1).astype(F32)

    kstart = _first_live_tile(qi, decay, cut_ref[0])
    _fill_vaug(qi, v_ref, vaug_ref)
    q = q_ref[...]
    lane = lax.broadcasted_iota(jnp.int32, q.shape, 1)
    zero = jnp.zeros_like(q)
    q_parts = (jnp.where(lane < HEAD_DIM, q, zero), jnp.where(lane >= HEAD_DIM, q, zero))
    _flash_parts(q_parts, (0, 0), k_ref, vaug_ref, bias, qi, kstart, tile, tile, (m0_ref, m1_ref),
                 (acc0_ref, acc1_ref))
    acc0, acc1 = acc0_ref[...], acc1_ref[...]
    lp = lam_ref[...]
    lam = (jnp.exp(jnp.sum(lp[0:1] * lp[1:2], axis=-1, keepdims=True))
           - jnp.exp(jnp.sum(lp[2:3] * lp[3:4], axis=-1, keepdims=True)) + lam_init)
    o = acc0[:, :LANES] / acc0[:, LANES:] - lam * (acc1[:, :LANES] / acc1[:, LANES:])
    o_ref[...] = (_rms_rows(o, sub_ref[...]) * (1.0 - lam_init)).astype(BF16)


def _diff_attention(proj, lam_params, subln_g, cutoff, batch, seq, layer):
    n = proj.shape[0]
    tile = ATT_TILE
    nq = seq // tile
    lam_init = 0.8 - 0.6 * math.exp(-0.3 * layer)
    slopes = jnp.asarray([2.0 ** (-8.0 * (i + 1) / A_HEADS) for i in range(A_HEADS)], F32)
    imap = lambda f: (lambda b, h, i, s, c: f(b, h, i))
    return pl.pallas_call(
        functools.partial(_diff_kernel, tile=tile, lam_init=lam_init),
        grid_spec=pltpu.PrefetchScalarGridSpec(
            num_scalar_prefetch=2,
            grid=(batch, A_HEADS, nq),
            in_specs=[pl.BlockSpec((tile, LANES), imap(lambda b, h, i: (b * nq + i, h))),
                      pl.BlockSpec((seq, LANES), imap(lambda b, h, i: (b, A_HEADS + h))),
                      pl.BlockSpec((seq, LANES), imap(lambda b, h, i: (b, 2 * A_HEADS + h))),
                      pl.BlockSpec((4, HEAD_DIM), imap(lambda b, h, i: (0, 0))),
                      pl.BlockSpec((1, LANES), imap(lambda b, h, i: (0, 0)))],
            out_specs=pl.BlockSpec((tile, LANES), imap(lambda b, h, i: (b * nq + i, h))),
            scratch_shapes=_flash_scratch(seq, tile)),
        out_shape=jax.ShapeDtypeStruct((n, A_HEADS * LANES), BF16),
        compiler_params=_cparams(("arbitrary", "arbitrary", "arbitrary")),
        name="diff_attention",
    )(slopes, cutoff, proj, proj, proj, lam_params.astype(F32), subln_g.astype(F32).reshape(1, LANES))


def _stick_kernel(q_ref, k_ref, v_ref, ntri_ref, o_ref, *, tq, tk):
    qi = pl.program_id(2)
    ratio = tq // tk
    q = q_ref[...]
    lane = lax.broadcasted_iota(jnp.int32, q.shape, 1)
    zero = jnp.zeros_like(q)
    qs = (jnp.where(lane < HEAD_DIM, q, zero), jnp.where(lane >= HEAD_DIM, q, zero))
    rel = (lax.broadcasted_iota(jnp.int32, (tq, tk), 0)
           - lax.broadcasted_iota(jnp.int32, (tq, tk), 1))
    ntri = ntri_ref[...]

    def sweep(tiles, carry):
        loaded = []
        for kt, _ in tiles:
            start = pl.multiple_of(kt * tk, tk)
            loaded.append((k_ref[pl.ds(start, tk), :], v_ref[pl.ds(start, tk), :]))
        new = []
        for hh in range(2):
            run, acc = carry[hh]
            for (kt, offset), (k, v) in zip(tiles, loaded):
                z = _dot_nt(qs[hh], k)
                sp = _softplus(z)
                if offset is None:
                    drop = sp
                else:
                    before = rel > offset
                    drop = jnp.where(before, sp, 0.0)
                hi, lo = _split2(drop)
                later = _dot(hi, ntri) + _dot(lo, ntri)
                w = jnp.exp((z - sp) + later + run)
                if offset is not None:
                    w = jnp.where(before, w, 0.0)
                acc = acc + _dot(w.astype(BF16), v)
                run = run - jnp.sum(drop, axis=-1, keepdims=True)
            new.append((run, acc))
        return tuple(new)

    init_one = (jnp.zeros((tq, 1), F32), jnp.zeros((tq, LANES), F32))
    own = [(qi * ratio + d, (qi * ratio + d) * tk - qi * tq) for d in range(ratio - 1, -1, -1)]
    prev = jnp.maximum(qi * ratio - 1, 0)
    prev_offset = jnp.where(qi > 0, -tk, tq)
    carry = sweep(own + [(prev, prev_offset)], (init_one, init_one))

    def live(c):
        top = jnp.maximum(jnp.max(c[0][0]), jnp.max(c[1][0]))
        return (top > EXP_ZERO_BELOW).astype(jnp.int32)

    def cond(state):
        kt, alive, _ = state
        return (kt >= 0) & (alive > 0)

    def body(state):
        kt, _, c = state
        c = sweep([(kt, None)], c)
        return kt - 1, live(c), c

    _, _, carry = lax.while_loop(cond, body, (qi * ratio - 2, live(carry), carry))
    o_ref[...] = jnp.where(lane < HEAD_DIM, carry[0][1], carry[1][1]).astype(BF16)


def _stick_attention(proj, batch, seq, col0):
    n = proj.shape[0]
    pairs = 4
    tq, tk = SB_Q_TILE, SB_K_TILE
    nq = seq // tq
    ntri = -(jnp.arange(tk)[:, None] > jnp.arange(tk)[None, :]).astype(BF16)
    return pl.pallas_call(
        functools.partial(_stick_kernel, tq=tq, tk=tk),
        grid=(batch, pairs, nq),
        in_specs=[pl.BlockSpec((tq, LANES), lambda b, h, i: (b * nq + i, col0 + h)),
                  pl.BlockSpec((seq, LANES), lambda b, h, i: (b, col0 + pairs + h)),
                  pl.BlockSpec((seq, LANES), lambda b, h, i: (b, col0 + 2 * pairs + h)),
                  pl.BlockSpec((tk, tk), lambda b, h, i: (0, 0))],
        out_specs=pl.BlockSpec((tq, LANES), lambda b, h, i: (b * nq + i, h)),
        out_shape=jax.ShapeDtypeStruct((n, pairs * LANES), BF16),
        compiler_params=_cparams(("arbitrary", "arbitrary", "arbitrary")),
        name="stick_attention",
    )(proj, proj, proj, ntri)


def _out_router_kernel(*refs, n_parts):
    x_ref = refs[0]
    o_refs = refs[1:1 + n_parts]
    w_refs = refs[1 + n_parts:1 + 2 * n_parts]
    (g_ref, wr_hi_ref, wr_lo_ref, br_ref, ltri_ref, xo_ref, h_ref, route_ref, count_ref,
     seen_ref) = refs[1 + 2 * n_parts:]
    mix = _dot(o_refs[0][...], w_refs[0][...])
    for p in range(1, n_parts):
        mix = mix + _dot(o_refs[p][...], w_refs[p][...])
    x = x_ref[...] + mix
    xo_ref[...] = x
    h = _rms_rows(x, g_ref[...])
    bits = lax.bitcast_convert_type(h.astype(BF16).astype(F32), jnp.uint32)
    half = h.shape[1] // 2
    h_ref[...] = (bits[:, half:] & jnp.uint32(0xFFFF0000)) | (bits[:, :half] >> 16)

    h_hi, h_lo = _split2(h)
    logits = (_dot(h_hi, wr_hi_ref[...]) + _dot(h_hi, wr_lo_ref[...]) + _dot(h_lo, wr_hi_ref[...])
              + br_ref[...])
    lane = lax.broadcasted_iota(jnp.int32, logits.shape, 1)

    def first_max(vals):
        top = jnp.max(vals, axis=-1, keepdims=True)
        idx = jnp.min(jnp.where(vals == top, lane, LANES), axis=-1, keepdims=True)
        return top, idx

    g_logits = jnp.where(lane < N_GROUPS, logits, NEG)
    g_top, g_idx = first_max(g_logits)
    g_w = 1.0 / jnp.sum(jnp.exp(g_logits - g_top), axis=-1, keepdims=True)
    lo_lane = N_GROUPS + EXPERTS_PER_GROUP * g_idx
    in_group = (lane >= lo_lane) & (lane < lo_lane + EXPERTS_PER_GROUP)
    e_logits = jnp.where(in_group, logits, NEG)
    v1, i1 = first_max(e_logits)
    v2, i2 = first_max(jnp.where(lane == i1, NEG, e_logits))
    e2 = jnp.exp(v2 - v1)
    w1 = g_w / (1.0 + e2)
    w2 = g_w * e2 / (1.0 + e2)

    @pl.when(pl.program_id(0) == 0)
    def _():
        seen_ref[...] = jnp.zeros_like(seen_ref)

    uses = ((lane == i1) | (lane == i2)).astype(BF16)
    earlier = _dot(ltri_ref[...], uses) + seen_ref[...]
    r1 = jnp.sum(jnp.where(lane == i1, earlier, 0.0), axis=-1, keepdims=True)
    r2 = jnp.sum(jnp.where(lane == i2, earlier, 0.0), axis=-1, keepdims=True)
    seen = earlier[-1:, :] + uses[-1:, :].astype(F32)
    seen_ref[...] = seen
    count_ref[...] = jnp.broadcast_to(seen, count_ref.shape)

    cols = ((i1 - N_GROUPS).astype(F32), (i2 - N_GROUPS).astype(F32), w1, w2, r1, r2)
    route = jnp.zeros(logits.shape, F32)
    for c, val in enumerate(cols):
        route = jnp.where(lane == c, val, route)
    route_ref[...] = route


def _out_router(x, o_parts, w_out, g_ffn, w_group, b_group, w_router, b_router):
    n, d = x.shape
    n_parts = len(o_parts)
    w_bf = w_out.astype(BF16)
    w_parts, r0 = [], 0
    for o in o_parts:
        w_parts.append(w_bf[r0:r0 + o.shape[1]])
        r0 += o.shape[1]
    wr = jnp.zeros((d, LANES), F32).at[:, :N_GROUPS].set(w_group)
    wr = wr.at[:, N_GROUPS:N_GROUPS + N_EXPERTS].set(w_router)
    wr_hi = wr.astype(BF16)
    wr_lo = (wr - wr_hi.astype(F32)).astype(BF16)
    br = jnp.zeros((1, LANES), F32).at[0, :N_GROUPS].set(b_group.astype(F32))
    br = br.at[0, N_GROUPS:N_GROUPS + N_EXPERTS].set(b_router.astype(F32))
    ltri = (jnp.arange(ROW_TILE)[:, None] > jnp.arange(ROW_TILE)[None, :]).astype(BF16)
    rows = lambda width: pl.BlockSpec((ROW_TILE, width), lambda i: (i, 0))
    const = lambda shape: pl.BlockSpec(shape, lambda i: (0, 0))
    return pl.pallas_call(
        functools.partial(_out_router_kernel, n_parts=n_parts),
        grid=(n // ROW_TILE,),
        in_specs=([rows(d)] + [rows(o.shape[1]) for o in o_parts]
                  + [const(w.shape) for w in w_parts]
                  + [const((1, d)), const((d, LANES)), const((d, LANES)), const((1, LANES)),
                     const((ROW_TILE, ROW_TILE))]),
        out_specs=[rows(d), rows(d // 2), rows(LANES), const((8, LANES))],
        out_shape=[jax.ShapeDtypeStruct((n, d), F32), jax.ShapeDtypeStruct((n, d // 2), jnp.uint32),
                   jax.ShapeDtypeStruct((n, LANES), F32), jax.ShapeDtypeStruct((8, LANES), F32)],
        scratch_shapes=[pltpu.VMEM((1, LANES), F32)],
        compiler_params=_cparams(("arbitrary",)),
        name="out_router",
    )(x, *o_parts, *w_parts, g_ffn.reshape(1, d), wr_hi, wr_lo, br, ltri)


def _dispatch_plan(route, counts_row, n_blocks):
    n = route.shape[0]
    e_flat = route[:, 0:2].astype(jnp.int32).reshape(-1)
    rank = route[:, 4:6].astype(jnp.int32).reshape(-1)
    counts = counts_row[N_GROUPS:N_GROUPS + N_EXPERTS].astype(jnp.int32)
    padded = (counts + MOE_TILE - 1) // MOE_TILE * MOE_TILE
    pad_ends = jnp.cumsum(padded)
    dest = ((pad_ends - padded)[e_flat] + rank).astype(jnp.int32)
    p = n_blocks * MOE_TILE
    tok_buf = (jnp.arange(p, dtype=jnp.int32) % n).at[dest].set(jnp.arange(2 * n, dtype=jnp.int32) // 2)
    blk_start = jnp.arange(n_blocks, dtype=jnp.int32) * MOE_TILE
    blk_e = jnp.sum((pad_ends[None, :] <= blk_start[:, None]).astype(jnp.int32), axis=1)
    blk_e = jnp.minimum(blk_e, N_EXPERTS - 1).astype(jnp.int32)
    n_used = (pad_ends[-1:] // MOE_TILE).astype(jnp.int32)
    return dest, tok_buf, blk_e, n_used


def _moe_kernel(blk_e_ref, tok_ref, nused_ref, h_hbm, wg_ref, wu_ref, wd_ref, y_ref, xbuf, wg_bf, wu_bf, wd_bf, sem):
    i = pl.program_id(0)
    last = pl.num_programs(0) - 1
    slot = i % MOE_SLOTS

    def start_rows(blk, slot_, row_lo=0, row_hi=MOE_TILE):
        for r in range(row_lo, row_hi):
            tok = tok_ref[blk * MOE_TILE + r]
            pltpu.make_async_copy(h_hbm.at[pl.ds(tok, 1), :], xbuf.at[slot_, pl.ds(r, 1), :],
                                  sem.at[slot_]).start(priority=r % 2)

    def wait_rows(slot_):
        pltpu.make_async_copy(h_hbm.at[pl.ds(0, MOE_TILE), :], xbuf.at[slot_], sem.at[slot_]).wait()

    @pl.when(i == 0)
    def _():
        for ahead in range(MOE_SLOTS - 1):
            start_rows(jnp.minimum(ahead, last), ahead)

    @pl.when((i == 0) | (blk_e_ref[i] != blk_e_ref[jnp.maximum(i - 1, 0)]))
    def _():
        wg_bf[...] = wg_ref[...].astype(BF16)
        wu_bf[...] = wu_ref[...].astype(BF16)
        wd_bf[...] = wd_ref[...].astype(BF16)

    n_used = nused_ref[0]

    @pl.when(i < n_used)
    def _():
        wait_rows(slot)
        packed = xbuf[slot]
        xb = jnp.concatenate(
            [lax.bitcast_convert_type(packed << 16, F32),
             lax.bitcast_convert_type(packed & jnp.uint32(0xFFFF0000), F32)], axis=1).astype(BF16)
        nxt, nxt_slot = jnp.minimum(i + MOE_SLOTS - 1, last), (i + MOE_SLOTS - 1) % MOE_SLOTS
        cut1, cut2 = MOE_TILE // 3, 2 * MOE_TILE // 3
        start_rows(nxt, nxt_slot, 0, cut1)
        gate = _dot(xb, wg_bf[...])
        start_rows(nxt, nxt_slot, cut1, cut2)
        up = _dot(xb, wu_bf[...])
        act = (gate * jax.nn.sigmoid(gate) * up).astype(BF16)
        start_rows(nxt, nxt_slot, cut2, MOE_TILE)
        y_ref[...] = _dot(act, wd_bf[...])

        @pl.when(i == n_used - 1)
        def _():
            for ahead in range(1, MOE_SLOTS):
                wait_rows((i + ahead) % MOE_SLOTS)

    @pl.when(i >= n_used)
    def _():
        y_ref[...] = jnp.zeros_like(y_ref)


def _moe_experts(h, tok_buf, blk_e, n_used, w_gate, w_up, w_down, layer, n_blocks):
    n, d = h.shape[0], 2 * h.shape[1]
    f = w_gate.shape[-1]
    p = n_blocks * MOE_TILE
    return pl.pallas_call(
        _moe_kernel,
        grid_spec=pltpu.PrefetchScalarGridSpec(
            num_scalar_prefetch=3,
            grid=(n_blocks,),
            in_specs=[pl.BlockSpec(memory_space=pl.ANY),
                      pl.BlockSpec((None, None, d, f), lambda i, be, tk, nu: (layer, be[i], 0, 0)),
                      pl.BlockSpec((None, None, d, f), lambda i, be, tk, nu: (layer, be[i], 0, 0)),
                      pl.BlockSpec((None, None, f, d), lambda i, be, tk, nu: (layer, be[i], 0, 0))],
            out_specs=pl.BlockSpec((MOE_TILE, d), lambda i, be, tk, nu: (i, 0)),
            scratch_shapes=[pltpu.VMEM((MOE_SLOTS, MOE_TILE, d // 2), jnp.uint32), pltpu.VMEM((d, f), BF16),
                            pltpu.VMEM((d, f), BF16), pltpu.VMEM((f, d), BF16),
                            pltpu.SemaphoreType.DMA((MOE_SLOTS,))]),
        out_shape=jax.ShapeDtypeStruct((p, d), F32),
        compiler_params=_cparams(("arbitrary",)),
        name="moe_experts",
    )(blk_e, tok_buf, n_used, h, w_gate, w_up, w_down)


def _combine_kernel(dest_ref, x_ref, route_ref, y_hbm, o_ref, ybuf, sem):
    i = pl.program_id(0)
    nsteps = pl.num_programs(0)
    slot = i % 2

    def start_gather(step, slot_):
        def body(r, c):
            for k in range(2):
                src = dest_ref[(step * CMB_TILE + r) * 2 + k]
                pltpu.make_async_copy(y_hbm.at[pl.ds(src, 1), :], ybuf.at[slot_, k, pl.ds(r, 1), :],
                                      sem.at[slot_]).start(priority=k)
            return c
        lax.fori_loop(0, CMB_TILE, body, 0, unroll=4)

    @pl.when(i == 0)
    def _():
        start_gather(0, 0)

    @pl.when(i + 1 < nsteps)
    def _():
        start_gather(i + 1, 1 - slot)

    for k in range(2):
        pltpu.make_async_copy(y_hbm.at[pl.ds(0, CMB_TILE), :], ybuf.at[slot, k], sem.at[slot]).wait()
    route = route_ref[...]
    o_ref[...] = x_ref[...] + (ybuf[slot, 0] * route[:, 2:3] + ybuf[slot, 1] * route[:, 3:4])


def _combine(x, y_buf, dest, route):
    n, d = x.shape
    return pl.pallas_call(
        _combine_kernel,
        grid_spec=pltpu.PrefetchScalarGridSpec(
            num_scalar_prefetch=1,
            grid=(n // CMB_TILE,),
            in_specs=[pl.BlockSpec((CMB_TILE, d), lambda i, ds: (i, 0)),
                      pl.BlockSpec((CMB_TILE, LANES), lambda i, ds: (i, 0)),
                      pl.BlockSpec(memory_space=pl.ANY)],
            out_specs=pl.BlockSpec((CMB_TILE, d), lambda i, ds: (i, 0)),
            scratch_shapes=[pltpu.VMEM((2, 2, CMB_TILE, d), F32), pltpu.SemaphoreType.DMA((2,))]),
        out_shape=jax.ShapeDtypeStruct((n, d), F32),
        compiler_params=_cparams(("arbitrary",)),
        name="moe_combine",
    )(dest, x, route, y_buf)


def kernel(x, norm_mix_g, norm_ffn_g, ab_w_in, a_q_gain, a_k_gain, a_lam_q1, a_lam_k1, a_lam_q2,
           a_lam_k2, a_subln_g, ab_w_out, c_w_in, c_b_f, c_q_gain, c_k_gain, c_w_out, moe_w_group,
           moe_b_group, moe_w_router, moe_b_router, moe_w_gate, moe_w_up, moe_w_down):
    batch, seq, d = x.shape
    n = batch * seq
    depth = norm_mix_g.shape[0]
    heads = d // HEAD_DIM
    n_blocks = (2 * n) // MOE_TILE + N_EXPERTS
    xs = x.reshape(n, d).astype(F32)
    moe = None
    for layer in range(depth):
        i = layer // 2
        if layer % 2 == 0:
            outs = _even_in_proj(xs, norm_mix_g[layer], ab_w_in[i], a_q_gain[i], a_k_gain[i], moe)
            proj = outs[0]
            xs = outs[-1] if moe is not None else xs
            lam_params = jnp.stack([a_lam_q1[i], a_lam_k1[i], a_lam_q2[i], a_lam_k2[i]])
            o_a = _diff_attention(proj, lam_params, a_subln_g[i], _zero_cutoff(a_q_gain[i], a_k_gain[i]),
                                  batch, seq, layer)
            o_b = _stick_attention(proj, batch, seq, 3 * A_HEADS)
            o_parts, w_out = [o_a, o_b], ab_w_out[i]
        else:
            outs = _odd_in_proj(xs, norm_mix_g[layer], c_w_in[i], c_b_f[i], c_q_gain[i], c_k_gain[i], seq,
                                moe)
            qkv, cum = outs[0], outs[1]
            xs = outs[-1] if moe is not None else xs
            cum_t = cum[:, :heads].reshape(batch, seq, heads // 2, 2).transpose(0, 2, 3, 1)
            cutoff = _zero_cutoff(c_q_gain[i], c_k_gain[i])
            o_parts, w_out = [_fox_attention(qkv, cum_t, cutoff, batch, seq)], c_w_out[i]
        xs, h, route, counts = _out_router(xs, o_parts, w_out, norm_ffn_g[layer], moe_w_group[layer],
                                           moe_b_group[layer], moe_w_router[layer], moe_b_router[layer])
        dest, tok_buf, blk_e, n_used = _dispatch_plan(route, counts[0], n_blocks)
        y_buf = _moe_experts(h, tok_buf, blk_e, n_used, moe_w_gate, moe_w_up, moe_w_down, layer, n_blocks)
        moe = (y_buf, dest, route)
    xs = _combine(xs, *moe)
    return xs.reshape(batch, seq, d).astype(x.dtype)
```

```python
import functools
import math

import jax
import jax.numpy as jnp
from jax import lax
from jax.experimental import pallas as pl
from jax.experimental.pallas import tpu as pltpu

F32 = jnp.float32
BF16 = jnp.bfloat16

HEAD_DIM = 64
LANES = 128
EPS = 1e-6
SCALE = HEAD_DIM ** -0.5
N_GROUPS = 4
EXPERTS_PER_GROUP = 4
N_EXPERTS = N_GROUPS * EXPERTS_PER_GROUP
A_HEADS = 4
NEG = -1e30
LOG2E = math.log2(math.e)

ROW_TILE = 512
ATT_TILE = 512
FOX_Q_TILE = 1024
SB_Q_TILE = 512
SB_K_TILE = 256
EXP2_UNDERFLOW = 150.0
EXP_ZERO_BELOW = -104.0
QK_BOUND_MARGIN = 1.02
MOE_TILE = 256
MOE_SLOTS = 3
CMB_TILE = 256
SM_ROWS = 32
SM_WIDTH = 1024
SEC = 512
IN_SECTIONS = 6
VMEM_LIMIT = 48 * 1024 * 1024


def _cparams(sem):
    return pltpu.CompilerParams(dimension_semantics=sem, vmem_limit_bytes=VMEM_LIMIT)


def _dot(a, b):
    return jnp.dot(a, b, preferred_element_type=F32)


def _dot_nt(a, b):
    return lax.dot_general(a, b, (((1,), (1,)), ((), ())), preferred_element_type=F32)


def _split2(x):
    hi = x.astype(BF16)
    lo = (x - hi.astype(F32)).astype(BF16)
    return hi, lo


def _rms_rows(x, g):
    return x * lax.rsqrt(jnp.mean(x * x, axis=-1, keepdims=True) + EPS) * g


def _group_rsqrt(acc, e_ref, et_ref):
    hi, lo = _split2(acc * acc)
    ss = _dot(hi, e_ref[...]) + _dot(lo, e_ref[...])
    rhi, rlo = _split2(lax.rsqrt(ss * (1.0 / HEAD_DIM) + EPS))
    return _dot(rhi, et_ref[...]) + _dot(rlo, et_ref[...])


def _pack_bf16_pairs(x):
    bits = lax.bitcast_convert_type(x.astype(BF16).astype(F32), jnp.uint32)
    half = x.shape[1] // 2
    return (bits[:, half:] & jnp.uint32(0xFFFF0000)) | (bits[:, :half] >> 16)


def _unpack_bf16_pairs(packed):
    return jnp.concatenate([lax.bitcast_convert_type(packed << 16, F32),
                            lax.bitcast_convert_type(packed & jnp.uint32(0xFFFF0000), F32)], axis=1)


def _softplus(z):
    return jnp.maximum(z, 0.0) + jnp.log(1.0 + jnp.exp(-jnp.abs(z)))


def _add_expert_rows(dest_ref, x_ref, route_ref, y_hbm, xo_ref, ybuf, sem):
    i = pl.program_id(0)
    last = pl.num_programs(0) - 1
    slot = i % 2

    def start_rows(step, slot_, row_lo=0, row_hi=ROW_TILE):
        for r in range(row_lo, row_hi):
            for k in range(2):
                src = dest_ref[(step * ROW_TILE + r) * 2 + k]
                pltpu.make_async_copy(y_hbm.at[pl.ds(src, 1), :], ybuf.at[slot_, k, pl.ds(r, 1), :],
                                      sem.at[slot_]).start(priority=k)

    def wait_rows(slot_):
        for k in range(2):
            pltpu.make_async_copy(y_hbm.at[pl.ds(0, ROW_TILE), :], ybuf.at[slot_, k], sem.at[slot_]).wait()

    @pl.when(i == 0)
    def _():
        start_rows(0, 0)

    wait_rows(slot)
    route = route_ref[...]
    x = x_ref[...] + (_unpack_bf16_pairs(ybuf[slot, 0]) * route[:, 2:3]
                      + _unpack_bf16_pairs(ybuf[slot, 1]) * route[:, 3:4])
    xo_ref[...] = x
    def issue(j):
        start_rows(jnp.minimum(i + 1, last), 1 - slot, j * ROW_TILE // IN_SECTIONS,
                   (j + 1) * ROW_TILE // IN_SECTIONS)

    def drain():
        @pl.when(i == last)
        def _():
            wait_rows(1 - slot)

    return x, issue, drain


def _even_in_kernel(*refs, fused):
    if fused:
        (dest_ref, x_ref, route_ref, y_hbm, g_ref, w_ref, gq_ref, gk_ref, e_ref, et_ref, o_ref, xo_ref,
         ybuf, sem) = refs
        x, issue, drain = _add_expert_rows(dest_ref, x_ref, route_ref, y_hbm, xo_ref, ybuf, sem)
    else:
        x_ref, g_ref, w_ref, gq_ref, gk_ref, e_ref, et_ref, o_ref = refs
        x = x_ref[...]
    xn = _rms_rows(x, g_ref[...]).astype(BF16)
    for sec in range(IN_SECTIONS):
        if fused:
            issue(sec)
        cols = slice(sec * SEC, (sec + 1) * SEC)
        acc = _dot(xn, w_ref[:, cols])
        if sec == 0:
            acc = acc * _group_rsqrt(acc, e_ref, et_ref) * gq_ref[...]
        elif sec == 1:
            acc = acc * _group_rsqrt(acc, e_ref, et_ref) * gk_ref[...]
        elif sec == 3:
            acc = acc * SCALE
        o_ref[:, cols] = acc.astype(BF16)
    if fused:
        drain()


def _odd_in_kernel(*refs, tiles_per_seq, fused):
    if fused:
        (dest_ref, x_ref, route_ref, y_hbm, g_ref, w_ref, wf_ref, bf_ref, gq_ref, gk_ref, e_ref, et_ref,
         tri_ref, o_ref, cum_ref, xo_ref, carry_ref, ybuf, sem) = refs
        x, issue, drain = _add_expert_rows(dest_ref, x_ref, route_ref, y_hbm, xo_ref, ybuf, sem)
    else:
        (x_ref, g_ref, w_ref, wf_ref, bf_ref, gq_ref, gk_ref, e_ref, et_ref, tri_ref, o_ref, cum_ref,
         carry_ref) = refs
        x = x_ref[...]
    i = pl.program_id(0)
    xn = _rms_rows(x, g_ref[...]).astype(BF16)
    for sec in range(IN_SECTIONS):
        if fused:
            issue(sec)
        cols = slice(sec * SEC, (sec + 1) * SEC)
        acc = _dot(xn, w_ref[:, cols])
        if sec < 2:
            acc = acc * _group_rsqrt(acc, e_ref, et_ref) * gq_ref[...]
        elif sec < 4:
            acc = acc * _group_rsqrt(acc, e_ref, et_ref) * gk_ref[...]
        o_ref[:, cols] = acc.astype(BF16)

    log_f = -_softplus(-(_dot(xn, wf_ref[...]) + bf_ref[...]))
    p1 = log_f.astype(BF16)
    r1 = log_f - p1.astype(F32)
    p2 = r1.astype(BF16)
    p3 = (r1 - p2.astype(F32)).astype(BF16)
    tri = tri_ref[...]
    within = _dot(tri, p1) + _dot(tri, p2) + _dot(tri, p3)

    @pl.when(i % tiles_per_seq == 0)
    def _():
        carry_ref[...] = jnp.zeros_like(carry_ref)

    cum = within + carry_ref[...]
    cum_ref[...] = cum
    carry_ref[...] = cum[-1:, :]
    if fused:
        drain()


def _group_maps():
    lane_grp = jnp.arange(SEC)[:, None] // HEAD_DIM
    e = (lane_grp == jnp.arange(LANES)[None, :]).astype(BF16)
    return e, e.T


def _in_proj_call(kernel_fn, name, x, moe, in_arrays, in_shapes, out_specs_fn, out_shapes, scratch):
    n, d = x.shape
    rows = lambda width: pl.BlockSpec((ROW_TILE, width), lambda i, *_: (i, 0))
    const = lambda shape: pl.BlockSpec(shape, lambda i, *_: (0,) * len(shape))
    in_specs = [rows(d)] + [const(s) for s in in_shapes]
    out_specs = out_specs_fn(rows)
    args = [x] + list(in_arrays)
    prefetch = []
    if moe is not None:
        y_buf, dest, route = moe
        in_specs = [rows(d), rows(LANES), pl.BlockSpec(memory_space=pl.ANY)] + in_specs[1:]
        args = [x, route, y_buf] + list(in_arrays)
        prefetch = [dest]
        out_specs = out_specs + [rows(d)]
        out_shapes = out_shapes + [jax.ShapeDtypeStruct((n, d), F32)]
        scratch = scratch + [pltpu.VMEM((2, 2, ROW_TILE, d // 2), jnp.uint32), pltpu.SemaphoreType.DMA((2,))]
    return pl.pallas_call(
        functools.partial(kernel_fn, fused=moe is not None),
        grid_spec=pltpu.PrefetchScalarGridSpec(
            num_scalar_prefetch=len(prefetch), grid=(n // ROW_TILE,), in_specs=in_specs,
            out_specs=out_specs, scratch_shapes=scratch),
        out_shape=out_shapes,
        compiler_params=_cparams(("arbitrary",)),
        name=name,
    )(*prefetch, *args)


def _even_in_proj(x, g, w_in, q_gain, k_gain, moe=None):
    n, d = x.shape
    e, et = _group_maps()
    reps = SEC // HEAD_DIM
    gq = jnp.tile(q_gain.astype(F32) * (SCALE * LOG2E), reps).reshape(1, SEC)
    gk = jnp.tile(k_gain.astype(F32), reps).reshape(1, SEC)
    return _in_proj_call(
        _even_in_kernel, "even_in_proj", x, moe,
        [g.reshape(1, d), w_in.astype(BF16), gq, gk, e, et],
        [(1, d), (d, 3 * d), (1, SEC), (1, SEC), (SEC, LANES), (LANES, SEC)],
        lambda rows: [rows(3 * d)], [jax.ShapeDtypeStruct((n, 3 * d), BF16)], [])


def _odd_in_proj(x, g, w_in, b_f, q_gain, k_gain, seq, moe=None):
    n, d = x.shape
    heads = d // HEAD_DIM
    e, et = _group_maps()
    reps = SEC // HEAD_DIM
    gq = jnp.tile(q_gain.astype(F32) * (SCALE * LOG2E), reps).reshape(1, SEC)
    gk = jnp.tile(k_gain.astype(F32), reps).reshape(1, SEC)
    wf = jnp.zeros((d, LANES), F32).at[:, :heads].set(w_in[:, 3 * d:]).astype(BF16)
    bf = jnp.zeros((1, LANES), F32).at[0, :heads].set(b_f.astype(F32))
    tri = (jnp.arange(ROW_TILE)[:, None] >= jnp.arange(ROW_TILE)[None, :]).astype(BF16)
    return _in_proj_call(
        functools.partial(_odd_in_kernel, tiles_per_seq=seq // ROW_TILE), "odd_in_proj", x, moe,
        [g.reshape(1, d), w_in[:, :3 * d].astype(BF16), wf, bf, gq, gk, e, et, tri],
        [(1, d), (d, 3 * d), (d, LANES), (1, LANES), (1, SEC), (1, SEC), (SEC, LANES), (LANES, SEC),
         (ROW_TILE, ROW_TILE)],
        lambda rows: [rows(3 * d), rows(LANES)],
        [jax.ShapeDtypeStruct((n, 3 * d), BF16), jax.ShapeDtypeStruct((n, LANES), F32)],
        [pltpu.VMEM((1, LANES), F32)])


def _flash_parts(q_parts, row0s, k_ref, vaug_ref, bias_fn, qi, kstart, tq, tile, m_refs, acc_refs):
    n_parts = len(q_parts)
    for p in range(n_parts):
        m_refs[p][...] = jnp.full(m_refs[p].shape, NEG, F32)
        acc_refs[p][...] = jnp.zeros(acc_refs[p].shape, F32)

    def softmax_rows(s, brow, m_all, diag_off):
        rows_p, width = s.shape
        chunk = SM_ROWS if width <= SM_WIDTH else SM_ROWS // 2
        if diag_off is not None:
            rel = (lax.broadcasted_iota(jnp.int32, (chunk, width), 0)
                   - lax.broadcasted_iota(jnp.int32, (chunk, width), 1))
        m_news, chunks = [], []
        for c in range(rows_p // chunk):
            rows = slice(c * chunk, (c + 1) * chunk)
            sc = s[rows] + brow
            if diag_off is not None:
                sc = jnp.where(rel >= -(diag_off + c * chunk), sc, NEG)
            blocks = [sc[:, j * LANES:(j + 1) * LANES] for j in range(width // LANES)]
            top = functools.reduce(jnp.maximum, blocks)
            m_new = jnp.maximum(m_all[rows], jnp.max(top, axis=-1, keepdims=True))
            m_news.append(m_new)
            chunks.append(jnp.concatenate([jnp.exp2((b - m_new).astype(BF16)) for b in blocks], axis=1))
        return jnp.concatenate(m_news, axis=0), jnp.concatenate(chunks, axis=0)

    def step(key_start, width, modes):
        key_start = pl.multiple_of(key_start, tile)
        k = k_ref[pl.ds(key_start, width), :]
        va = vaug_ref[pl.ds(key_start, width), :]
        brow = bias_fn(key_start, width)
        live = [p for p in range(n_parts) if modes[p] != "skip"]
        s = {p: _dot_nt(q_parts[p], k) for p in live}
        m_old = {p: m_refs[p][...] for p in live}
        acc_old = {p: acc_refs[p][...] for p in live}
        m_new, acc_new = {}, {}
        for p in live:
            m_new[p], w = softmax_rows(s[p], brow, m_old[p], modes[p])
            alpha = jnp.exp2(m_old[p] - m_new[p])
            pv = _dot(w, va)
            acc_new[p] = jnp.concatenate([alpha * acc_old[p][:, :LANES] + pv[:, :LANES],
                                          alpha * acc_old[p][:, LANES:] + pv[:, LANES:]], axis=1)
        for p in live:
            m_refs[p][...] = m_new[p]
            acc_refs[p][...] = acc_new[p]

    plain = (None,) * n_parts
    first_own = qi * (tq // tile)
    n_plain = first_own - kstart
    n_quads = n_plain // 4
    rest = n_plain - 4 * n_quads

    def quad_body(j, c):
        step((kstart + 4 * j) * tile, 4 * tile, plain)
        return c

    lax.fori_loop(0, n_quads, quad_body, 0)

    @pl.when(rest >= 2)
    def _():
        step((kstart + 4 * n_quads) * tile, 2 * tile, plain)

    @pl.when(rest % 2 == 1)
    def _():
        step((first_own - 1) * tile, tile, plain)

    for j in range(tq // tile):
        modes = []
        for p in range(n_parts):
            rows_p = q_parts[p].shape[0]
            if row0s[p] + rows_p - 1 < j * tile:
                modes.append("skip")
            elif row0s[p] >= (j + 1) * tile - 1:
                modes.append(None)
            else:
                modes.append(row0s[p] - j * tile)
        step(qi * tq + j * tile, tile, tuple(modes))


def _fill_vaug(qi, v_ref, vaug_ref):
    @pl.when(qi == 0)
    def _():
        vaug_ref[:, :LANES] = v_ref[...]
        vaug_ref[:, LANES:] = jnp.ones(v_ref.shape, BF16)


def _first_live_tile(first_own, decay_to_tile, cutoff):
    def cond(kt):
        return (kt < first_own) & (decay_to_tile(jnp.minimum(kt, first_own - 1)) > cutoff)
    return lax.while_loop(cond, lambda kt: kt + 1, jnp.int32(0))


def _flash_scratch(seq, rows):
    return [pltpu.VMEM((seq, 2 * LANES), BF16), pltpu.VMEM((rows, LANES), F32), pltpu.VMEM((rows, LANES), F32),
            pltpu.VMEM((rows, 2 * LANES), F32), pltpu.VMEM((rows, 2 * LANES), F32)]


def _fox_kernel(ftab_ref, cut_ref, q_ref, k_ref, v_ref, f_ref, o_ref, vaug_ref, m0_ref, m1_ref, acc0_ref,
                acc1_ref, *, tq, tile, nk):
    b = pl.program_id(0)
    hp = pl.program_id(1)
    qi = pl.program_id(2)
    qstart = pl.multiple_of(qi * tq, tq)
    first_own = qi * (tq // tile)
    half = tq // 2
    _fill_vaug(qi, v_ref, vaug_ref)
    q = q_ref[...]
    lane = lax.broadcasted_iota(jnp.int32, q.shape, 1)
    cutoff = cut_ref[0]

    def one_head(hh, carry):
        in_head = (lane >= hh * HEAD_DIM) & (lane < (hh + 1) * HEAD_DIM)
        qh = jnp.where(in_head, q, jnp.zeros_like(q))
        lane_half = lax.broadcasted_iota(jnp.int32, (half, LANES), 1)
        in_head_half = (lane_half >= hh * HEAD_DIM) & (lane_half < (hh + 1) * HEAD_DIM)

        def f_row(start, width):
            first = jnp.full((1, width), hh, jnp.int32) == 0
            return jnp.where(first, f_ref[0:1, pl.ds(start, width)], f_ref[1:2, pl.ds(start, width)])

        f_q0 = f_row(qstart, LANES)[:, 0:1]

        def bias(start, width):
            return LOG2E * (f_q0 - f_row(start, width))

        base = ((b * pl.num_programs(1) + hp) * 2 + hh) * nk

        def decay(kt):
            return (ftab_ref[base + kt + 1] - ftab_ref[base + first_own]) * LOG2E

        kstart = _first_live_tile(first_own, decay, cutoff)
        _flash_parts((qh[:half], qh[half:]), (0, half), k_ref, vaug_ref, bias, qi, kstart, tq, tile,
                     (m0_ref, m1_ref), (acc0_ref, acc1_ref))
        for p, acc_ref in enumerate((acc0_ref, acc1_ref)):
            rows = slice(p * half, (p + 1) * half)
            acc = acc_ref[...]
            o_h = acc[:, :LANES] / acc[:, LANES:]
            o_ref[rows, :] = jnp.where(in_head_half, o_h, o_ref[rows, :].astype(F32)).astype(BF16)
        return carry

    o_ref[...] = jnp.zeros_like(o_ref)
    lax.fori_loop(0, 2, one_head, 0)


def _zero_cutoff(q_gain, k_gain):
    bound = (HEAD_DIM * SCALE * LOG2E * QK_BOUND_MARGIN) * jnp.max(jnp.abs(q_gain)) * jnp.max(jnp.abs(k_gain))
    return (EXP2_UNDERFLOW + 2.0 * bound).astype(F32).reshape(1)


def _fox_attention(qkv, cum_t, cutoff, batch, seq):
    n, d3 = qkv.shape
    d = d3 // 3
    pairs = d // LANES
    tq, tile = FOX_Q_TILE, ATT_TILE
    nq = seq // tq
    ftab = cum_t[:, :, :, ::tile].reshape(-1)
    return pl.pallas_call(
        functools.partial(_fox_kernel, tq=tq, tile=tile, nk=seq // tile),
        grid_spec=pltpu.PrefetchScalarGridSpec(
            num_scalar_prefetch=2,
            grid=(batch, pairs, nq),
            in_specs=[pl.BlockSpec((tq, LANES), lambda b, h, i, ft, ct: (b * nq + i, h)),
                      pl.BlockSpec((seq, LANES), lambda b, h, i, ft, ct: (b, pairs + h)),
                      pl.BlockSpec((seq, LANES), lambda b, h, i, ft, ct: (b, 2 * pairs + h)),
                      pl.BlockSpec((None, None, 2, seq), lambda b, h, i, ft, ct: (b, h, 0, 0))],
            out_specs=pl.BlockSpec((tq, LANES), lambda b, h, i, ft, ct: (b * nq + i, h)),
            scratch_shapes=_flash_scratch(seq, tq // 2)),
        out_shape=jax.ShapeDtypeStruct((n, d), BF16),
        compiler_params=_cparams(("arbitrary", "arbitrary", "arbitrary")),
        name="fox_attention",
    )(ftab, cutoff, qkv, qkv, qkv, cum_t)


def _diff_kernel(slope_ref, cut_ref, q_ref, k_ref, v_ref, lam_ref, sub_ref, o_ref, vaug_ref, m0_ref, m1_ref,
                 acc0_ref, acc1_ref, *, tile, lam_init):
    h = pl.program_id(1)
    qi = pl.program_id(2)
    slope = slope_ref[h] * LOG2E
    qstart = qi * tile

    def bias(start, width):
        key_off = lax.broadcasted_iota(jnp.int32, (1, width), 1)
        return slope * (start - qstart + key_off).astype(F32)

    def decay(kt):
        return slope * (qstart - (kt + 1) * tile + 1).astype(F32)

    kstart = _first_live_tile(qi, decay, cut_ref[0])
    _fill_vaug(qi, v_ref, vaug_ref)
    q = q_ref[...]
    lane = lax.broadcasted_iota(jnp.int32, q.shape, 1)
    zero = jnp.zeros_like(q)
    q_parts = (jnp.where(lane < HEAD_DIM, q, zero), jnp.where(lane >= HEAD_DIM, q, zero))
    _flash_parts(q_parts, (0, 0), k_ref, vaug_ref, bias, qi, kstart, tile, tile, (m0_ref, m1_ref),
                 (acc0_ref, acc1_ref))
    acc0, acc1 = acc0_ref[...], acc1_ref[...]
    lp = lam_ref[...]
    lam = (jnp.exp(jnp.sum(lp[0:1] * lp[1:2], axis=-1, keepdims=True))
           - jnp.exp(jnp.sum(lp[2:3] * lp[3:4], axis=-1, keepdims=True)) + lam_init)
    o = acc0[:, :LANES] / acc0[:, LANES:] - lam * (acc1[:, :LANES] / acc1[:, LANES:])
    o_ref[...] = (_rms_rows(o, sub_ref[...]) * (1.0 - lam_init)).astype(BF16)


def _diff_attention(proj, lam_params, subln_g, cutoff, batch, seq, layer):
    n = proj.shape[0]
    tile = ATT_TILE
    nq = seq // tile
    lam_init = 0.8 - 0.6 * math.exp(-0.3 * layer)
    slopes = jnp.asarray([2.0 ** (-8.0 * (i + 1) / A_HEADS) for i in range(A_HEADS)], F32)
    imap = lambda f: (lambda b, h, i, s, c: f(b, h, i))
    return pl.pallas_call(
        functools.partial(_diff_kernel, tile=tile, lam_init=lam_init),
        grid_spec=pltpu.PrefetchScalarGridSpec(
            num_scalar_prefetch=2,
            grid=(batch, A_HEADS, nq),
            in_specs=[pl.BlockSpec((tile, LANES), imap(lambda b, h, i: (b * nq + i, h))),
                      pl.BlockSpec((seq, LANES), imap(lambda b, h, i: (b, A_HEADS + h))),
                      pl.BlockSpec((seq, LANES), imap(lambda b, h, i: (b, 2 * A_HEADS + h))),
                      pl.BlockSpec((4, HEAD_DIM), imap(lambda b, h, i: (0, 0))),
                      pl.BlockSpec((1, LANES), imap(lambda b, h, i: (0, 0)))],
            out_specs=pl.BlockSpec((tile, LANES), imap(lambda b, h, i: (b * nq + i, h))),
            scratch_shapes=_flash_scratch(seq, tile)),
        out_shape=jax.ShapeDtypeStruct((n, A_HEADS * LANES), BF16),
        compiler_params=_cparams(("arbitrary", "arbitrary", "arbitrary")),
        name="diff_attention",
    )(slopes, cutoff, proj, proj, proj, lam_params.astype(F32), subln_g.astype(F32).reshape(1, LANES))


def _stick_kernel(q_ref, k_ref, v_ref, ntri_ref, o_ref, *, tq, tk):
    qi = pl.program_id(2)
    ratio = tq // tk
    q = q_ref[...]
    lane = lax.broadcasted_iota(jnp.int32, q.shape, 1)
    zero = jnp.zeros_like(q)
    qs = (jnp.where(lane < HEAD_DIM, q, zero), jnp.where(lane >= HEAD_DIM, q, zero))
    rel = (lax.broadcasted_iota(jnp.int32, (tq, tk), 0)
           - lax.broadcasted_iota(jnp.int32, (tq, tk), 1))
    ntri = ntri_ref[...]

    def sweep(tiles, carry):
        loaded = []
        for kt, _ in tiles:
            start = pl.multiple_of(kt * tk, tk)
            loaded.append((k_ref[pl.ds(start, tk), :], v_ref[pl.ds(start, tk), :]))
        new = []
        for hh in range(2):
            run, acc = carry[hh]
            for (kt, offset), (k, v) in zip(tiles, loaded):
                z = _dot_nt(qs[hh], k)
                sp = _softplus(z)
                if offset is None:
                    drop = sp
                else:
                    before = rel > offset
                    drop = jnp.where(before, sp, 0.0)
                hi, lo = _split2(drop)
                later = _dot(hi, ntri) + _dot(lo, ntri)
                w = jnp.exp((z - sp) + later + run)
                if offset is not None:
                    w = jnp.where(before, w, 0.0)
                acc = acc + _dot(w.astype(BF16), v)
                run = run - jnp.sum(drop, axis=-1, keepdims=True)
            new.append((run, acc))
        return tuple(new)

    init_one = (jnp.zeros((tq, 1), F32), jnp.zeros((tq, LANES), F32))
    own = [(qi * ratio + d, (qi * ratio + d) * tk - qi * tq) for d in range(ratio - 1, -1, -1)]
    prev = jnp.maximum(qi * ratio - 1, 0)
    prev_offset = jnp.where(qi > 0, -tk, tq)
    carry = sweep(own + [(prev, prev_offset)], (init_one, init_one))

    def live(c):
        top = jnp.maximum(jnp.max(c[0][0]), jnp.max(c[1][0]))
        return (top > EXP_ZERO_BELOW).astype(jnp.int32)

    def cond(state):
        kt, alive, _ = state
        return (kt >= 0) & (alive > 0)

    def body(state):
        kt, _, c = state
        c = sweep([(kt, None)], c)
        return kt - 1, live(c), c

    _, _, carry = lax.while_loop(cond, body, (qi * ratio - 2, live(carry), carry))
    o_ref[...] = jnp.where(lane < HEAD_DIM, carry[0][1], carry[1][1]).astype(BF16)


def _stick_attention(proj, batch, seq, col0):
    n = proj.shape[0]
    pairs = 4
    tq, tk = SB_Q_TILE, SB_K_TILE
    nq = seq // tq
    ntri = -(jnp.arange(tk)[:, None] > jnp.arange(tk)[None, :]).astype(BF16)
    return pl.pallas_call(
        functools.partial(_stick_kernel, tq=tq, tk=tk),
        grid=(batch, pairs, nq),
        in_specs=[pl.BlockSpec((tq, LANES), lambda b, h, i: (b * nq + i, col0 + h)),
                  pl.BlockSpec((seq, LANES), lambda b, h, i: (b, col0 + pairs + h)),
                  pl.BlockSpec((seq, LANES), lambda b, h, i: (b, col0 + 2 * pairs + h)),
                  pl.BlockSpec((tk, tk), lambda b, h, i: (0, 0))],
        out_specs=pl.BlockSpec((tq, LANES), lambda b, h, i: (b * nq + i, h)),
        out_shape=jax.ShapeDtypeStruct((n, pairs * LANES), BF16),
        compiler_params=_cparams(("arbitrary", "arbitrary", "arbitrary")),
        name="stick_attention",
    )(proj, proj, proj, ntri)


def _out_router_kernel(*refs, n_parts):
    x_ref = refs[0]
    o_refs = refs[1:1 + n_parts]
    w_refs = refs[1 + n_parts:1 + 2 * n_parts]
    (g_ref, wr_hi_ref, wr_lo_ref, br_ref, ltri_ref, xo_ref, h_ref, route_ref, count_ref,
     seen_ref) = refs[1 + 2 * n_parts:]
    mix = _dot(o_refs[0][...], w_refs[0][...])
    for p in range(1, n_parts):
        mix = mix + _dot(o_refs[p][...], w_refs[p][...])
    x = x_ref[...] + mix
    xo_ref[...] = x
    h = _rms_rows(x, g_ref[...])
    h_ref[...] = _pack_bf16_pairs(h)

    h_hi, h_lo = _split2(h)
    logits = (_dot(h_hi, wr_hi_ref[...]) + _dot(h_hi, wr_lo_ref[...]) + _dot(h_lo, wr_hi_ref[...])
              + br_ref[...])
    lane = lax.broadcasted_iota(jnp.int32, logits.shape, 1)

    def first_max(vals):
        top = jnp.max(vals, axis=-1, keepdims=True)
        idx = jnp.min(jnp.where(vals == top, lane, LANES), axis=-1, keepdims=True)
        return top, idx

    g_logits = jnp.where(lane < N_GROUPS, logits, NEG)
    g_top, g_idx = first_max(g_logits)
    g_w = 1.0 / jnp.sum(jnp.exp(g_logits - g_top), axis=-1, keepdims=True)
    lo_lane = N_GROUPS + EXPERTS_PER_GROUP * g_idx
    in_group = (lane >= lo_lane) & (lane < lo_lane + EXPERTS_PER_GROUP)
    e_logits = jnp.where(in_group, logits, NEG)
    v1, i1 = first_max(e_logits)
    v2, i2 = first_max(jnp.where(lane == i1, NEG, e_logits))
    e2 = jnp.exp(v2 - v1)
    w1 = g_w / (1.0 + e2)
    w2 = g_w * e2 / (1.0 + e2)

    @pl.when(pl.program_id(0) == 0)
    def _():
        seen_ref[...] = jnp.zeros_like(seen_ref)

    uses = ((lane == i1) | (lane == i2)).astype(BF16)
    earlier = _dot(ltri_ref[...], uses) + seen_ref[...]
    r1 = jnp.sum(jnp.where(lane == i1, earlier, 0.0), axis=-1, keepdims=True)
    r2 = jnp.sum(jnp.where(lane == i2, earlier, 0.0), axis=-1, keepdims=True)
    seen = earlier[-1:, :] + uses[-1:, :].astype(F32)
    seen_ref[...] = seen
    count_ref[...] = jnp.broadcast_to(seen, count_ref.shape)

    cols = ((i1 - N_GROUPS).astype(F32), (i2 - N_GROUPS).astype(F32), w1, w2, r1, r2)
    route = jnp.zeros(logits.shape, F32)
    for c, val in enumerate(cols):
        route = jnp.where(lane == c, val, route)
    route_ref[...] = route


def _out_router(x, o_parts, w_out, g_ffn, w_group, b_group, w_router, b_router):
    n, d = x.shape
    n_parts = len(o_parts)
    w_bf = w_out.astype(BF16)
    w_parts, r0 = [], 0
    for o in o_parts:
        w_parts.append(w_bf[r0:r0 + o.shape[1]])
        r0 += o.shape[1]
    wr = jnp.zeros((d, LANES), F32).at[:, :N_GROUPS].set(w_group)
    wr = wr.at[:, N_GROUPS:N_GROUPS + N_EXPERTS].set(w_router)
    wr_hi = wr.astype(BF16)
    wr_lo = (wr - wr_hi.astype(F32)).astype(BF16)
    br = jnp.zeros((1, LANES), F32).at[0, :N_GROUPS].set(b_group.astype(F32))
    br = br.at[0, N_GROUPS:N_GROUPS + N_EXPERTS].set(b_router.astype(F32))
    ltri = (jnp.arange(ROW_TILE)[:, None] > jnp.arange(ROW_TILE)[None, :]).astype(BF16)
    rows = lambda width: pl.BlockSpec((ROW_TILE, width), lambda i: (i, 0))
    const = lambda shape: pl.BlockSpec(shape, lambda i: (0, 0))
    return pl.pallas_call(
        functools.partial(_out_router_kernel, n_parts=n_parts),
        grid=(n // ROW_TILE,),
        in_specs=([rows(d)] + [rows(o.shape[1]) for o in o_parts]
                  + [const(w.shape) for w in w_parts]
                  + [const((1, d)), const((d, LANES)), const((d, LANES)), const((1, LANES)),
                     const((ROW_TILE, ROW_TILE))]),
        out_specs=[rows(d), rows(d // 2), rows(LANES), const((8, LANES))],
        out_shape=[jax.ShapeDtypeStruct((n, d), F32), jax.ShapeDtypeStruct((n, d // 2), jnp.uint32),
                   jax.ShapeDtypeStruct((n, LANES), F32), jax.ShapeDtypeStruct((8, LANES), F32)],
        scratch_shapes=[pltpu.VMEM((1, LANES), F32)],
        compiler_params=_cparams(("arbitrary",)),
        name="out_router",
    )(x, *o_parts, *w_parts, g_ffn.reshape(1, d), wr_hi, wr_lo, br, ltri)


def _dispatch_plan(route, counts_row, n_blocks):
    n = route.shape[0]
    e_flat = route[:, 0:2].astype(jnp.int32).reshape(-1)
    rank = route[:, 4:6].astype(jnp.int32).reshape(-1)
    counts = counts_row[N_GROUPS:N_GROUPS + N_EXPERTS].astype(jnp.int32)
    padded = (counts + MOE_TILE - 1) // MOE_TILE * MOE_TILE
    pad_ends = jnp.cumsum(padded)
    dest = ((pad_ends - padded)[e_flat] + rank).astype(jnp.int32)
    p = n_blocks * MOE_TILE
    tok_buf = (jnp.arange(p, dtype=jnp.int32) % n).at[dest].set(jnp.arange(2 * n, dtype=jnp.int32) // 2)
    blk_start = jnp.arange(n_blocks, dtype=jnp.int32) * MOE_TILE
    blk_e = jnp.sum((pad_ends[None, :] <= blk_start[:, None]).astype(jnp.int32), axis=1)
    blk_e = jnp.minimum(blk_e, N_EXPERTS - 1).astype(jnp.int32)
    n_used = (pad_ends[-1:] // MOE_TILE).astype(jnp.int32)
    return dest, tok_buf, blk_e, n_used


def _moe_kernel(blk_e_ref, tok_ref, nused_ref, h_hbm, wg_ref, wu_ref, wd_ref, y_ref, xbuf, wg_bf, wu_bf, wd_bf, sem):
    i = pl.program_id(0)
    last = pl.num_programs(0) - 1
    slot = i % MOE_SLOTS

    def start_rows(blk, slot_, row_lo=0, row_hi=MOE_TILE):
        for r in range(row_lo, row_hi):
            tok = tok_ref[blk * MOE_TILE + r]
            pltpu.make_async_copy(h_hbm.at[pl.ds(tok, 1), :], xbuf.at[slot_, pl.ds(r, 1), :],
                                  sem.at[slot_]).start(priority=r % 2)

    def wait_rows(slot_):
        pltpu.make_async_copy(h_hbm.at[pl.ds(0, MOE_TILE), :], xbuf.at[slot_], sem.at[slot_]).wait()

    @pl.when(i == 0)
    def _():
        for ahead in range(MOE_SLOTS - 1):
            start_rows(jnp.minimum(ahead, last), ahead)

    @pl.when((i == 0) | (blk_e_ref[i] != blk_e_ref[jnp.maximum(i - 1, 0)]))
    def _():
        wg_bf[...] = wg_ref[...].astype(BF16)
        wu_bf[...] = wu_ref[...].astype(BF16)
        wd_bf[...] = wd_ref[...].astype(BF16)

    n_used = nused_ref[0]

    @pl.when(i < n_used)
    def _():
        wait_rows(slot)
        xb = _unpack_bf16_pairs(xbuf[slot]).astype(BF16)
        nxt, nxt_slot = jnp.minimum(i + MOE_SLOTS - 1, last), (i + MOE_SLOTS - 1) % MOE_SLOTS
        cut1, cut2 = MOE_TILE // 3, 2 * MOE_TILE // 3
        start_rows(nxt, nxt_slot, 0, cut1)
        gate = _dot(xb, wg_bf[...])
        start_rows(nxt, nxt_slot, cut1, cut2)
        up = _dot(xb, wu_bf[...])
        act = (gate * jax.nn.sigmoid(gate) * up).astype(BF16)
        start_rows(nxt, nxt_slot, cut2, MOE_TILE)
        y_ref[...] = _pack_bf16_pairs(_dot(act, wd_bf[...]))

        @pl.when(i == n_used - 1)
        def _():
            for ahead in range(1, MOE_SLOTS):
                wait_rows((i + ahead) % MOE_SLOTS)

    @pl.when(i >= n_used)
    def _():
        y_ref[...] = jnp.zeros_like(y_ref)


def _moe_experts(h, tok_buf, blk_e, n_used, w_gate, w_up, w_down, layer, n_blocks):
    n, d = h.shape[0], 2 * h.shape[1]
    f = w_gate.shape[-1]
    p = n_blocks * MOE_TILE
    return pl.pallas_call(
        _moe_kernel,
        grid_spec=pltpu.PrefetchScalarGridSpec(
            num_scalar_prefetch=3,
            grid=(n_blocks,),
            in_specs=[pl.BlockSpec(memory_space=pl.ANY),
                      pl.BlockSpec((None, None, d, f), lambda i, be, tk, nu: (layer, be[i], 0, 0)),
                      pl.BlockSpec((None, None, d, f), lambda i, be, tk, nu: (layer, be[i], 0, 0)),
                      pl.BlockSpec((None, None, f, d), lambda i, be, tk, nu: (layer, be[i], 0, 0))],
            out_specs=pl.BlockSpec((MOE_TILE, d // 2), lambda i, be, tk, nu: (i, 0)),
            scratch_shapes=[pltpu.VMEM((MOE_SLOTS, MOE_TILE, d // 2), jnp.uint32), pltpu.VMEM((d, f), BF16),
                            pltpu.VMEM((d, f), BF16), pltpu.VMEM((f, d), BF16),
                            pltpu.SemaphoreType.DMA((MOE_SLOTS,))]),
        out_shape=jax.ShapeDtypeStruct((p, d // 2), jnp.uint32),
        compiler_params=_cparams(("arbitrary",)),
        name="moe_experts",
    )(blk_e, tok_buf, n_used, h, w_gate, w_up, w_down)


def _combine_kernel(dest_ref, x_ref, route_ref, y_hbm, o_ref, ybuf, sem):
    i = pl.program_id(0)
    nsteps = pl.num_programs(0)
    slot = i % 2

    def start_gather(step, slot_):
        def body(r, c):
            for k in range(2):
                src = dest_ref[(step * CMB_TILE + r) * 2 + k]
                pltpu.make_async_copy(y_hbm.at[pl.ds(src, 1), :], ybuf.at[slot_, k, pl.ds(r, 1), :],
                                      sem.at[slot_]).start(priority=k)
            return c
        lax.fori_loop(0, CMB_TILE, body, 0, unroll=4)

    @pl.when(i == 0)
    def _():
        start_gather(0, 0)

    @pl.when(i + 1 < nsteps)
    def _():
        start_gather(i + 1, 1 - slot)

    for k in range(2):
        pltpu.make_async_copy(y_hbm.at[pl.ds(0, CMB_TILE), :], ybuf.at[slot, k], sem.at[slot]).wait()
    route = route_ref[...]
    o_ref[...] = x_ref[...] + (_unpack_bf16_pairs(ybuf[slot, 0]) * route[:, 2:3]
                               + _unpack_bf16_pairs(ybuf[slot, 1]) * route[:, 3:4])


def _combine(x, y_buf, dest, route):
    n, d = x.shape
    return pl.pallas_call(
        _combine_kernel,
        grid_spec=pltpu.PrefetchScalarGridSpec(
            num_scalar_prefetch=1,
            grid=(n // CMB_TILE,),
            in_specs=[pl.BlockSpec((CMB_TILE, d), lambda i, ds: (i, 0)),
                      pl.BlockSpec((CMB_TILE, LANES), lambda i, ds: (i, 0)),
                      pl.BlockSpec(memory_space=pl.ANY)],
            out_specs=pl.BlockSpec((CMB_TILE, d), lambda i, ds: (i, 0)),
            scratch_shapes=[pltpu.VMEM((2, 2, CMB_TILE, d // 2), jnp.uint32), pltpu.SemaphoreType.DMA((2,))]),
        out_shape=jax.ShapeDtypeStruct((n, d), F32),
        compiler_params=_cparams(("arbitrary",)),
        name="moe_combine",
    )(dest, x, route, y_buf)


def kernel(x, norm_mix_g, norm_ffn_g, ab_w_in, a_q_gain, a_k_gain, a_lam_q1, a_lam_k1, a_lam_q2,
           a_lam_k2, a_subln_g, ab_w_out, c_w_in, c_b_f, c_q_gain, c_k_gain, c_w_out, moe_w_group,
           moe_b_group, moe_w_router, moe_b_router, moe_w_gate, moe_w_up, moe_w_down):
    batch, seq, d = x.shape
    n = batch * seq
    depth = norm_mix_g.shape[0]
    heads = d // HEAD_DIM
    n_blocks = (2 * n) // MOE_TILE + N_EXPERTS
    xs = x.reshape(n, d).astype(F32)
    moe = None
    for layer in range(depth):
        i = layer // 2
        if layer % 2 == 0:
            outs = _even_in_proj(xs, norm_mix_g[layer], ab_w_in[i], a_q_gain[i], a_k_gain[i], moe)
            proj = outs[0]
            xs = outs[-1] if moe is not None else xs
            lam_params = jnp.stack([a_lam_q1[i], a_lam_k1[i], a_lam_q2[i], a_lam_k2[i]])
            o_a = _diff_attention(proj, lam_params, a_subln_g[i], _zero_cutoff(a_q_gain[i], a_k_gain[i]),
                                  batch, seq, layer)
            o_b = _stick_attention(proj, batch, seq, 3 * A_HEADS)
            o_parts, w_out = [o_a, o_b], ab_w_out[i]
        else:
            outs = _odd_in_proj(xs, norm_mix_g[layer], c_w_in[i], c_b_f[i], c_q_gain[i], c_k_gain[i], seq,
                                moe)
            qkv, cum = outs[0], outs[1]
            xs = outs[-1] if moe is not None else xs
            cum_t = cum[:, :heads].reshape(batch, seq, heads // 2, 2).transpose(0, 2, 3, 1)
            cutoff = _zero_cutoff(c_q_gain[i], c_k_gain[i])
            o_parts, w_out = [_fox_attention(qkv, cum_t, cutoff, batch, seq)], c_w_out[i]
        xs, h, route, counts = _out_router(xs, o_parts, w_out, norm_ffn_g[layer], moe_w_group[layer],
                                           moe_b_group[layer], moe_w_router[layer], moe_b_router[layer])
        dest, tok_buf, blk_e, n_used = _dispatch_plan(route, counts[0], n_blocks)
        y_buf = _moe_experts(h, tok_buf, blk_e, n_used, moe_w_gate, moe_w_up, moe_w_down, layer, n_blocks)
        moe = (y_buf, dest, route)
    xs = _combine(xs, *moe)
    return xs.reshape(batch, seq, d).astype(x.dtype)
```

```python
import functools
import math

import jax
import jax.numpy as jnp
from jax import lax
from jax.experimental import pallas as pl
from jax.experimental.pallas import tpu as pltpu

F32 = jnp.float32
BF16 = jnp.bfloat16

HEAD_DIM = 64
LANES = 128
EPS = 1e-6
SCALE = HEAD_DIM ** -0.5
N_GROUPS = 4
EXPERTS_PER_GROUP = 4
N_EXPERTS = N_GROUPS * EXPERTS_PER_GROUP
A_HEADS = 4
NEG = -1e30
LOG2E = math.log2(math.e)

ROW_TILE = 512
ATT_TILE = 512
FOX_Q_TILE = 1024
SB_Q_TILE = 512
SB_K_TILE = 256
EXP2_UNDERFLOW = 150.0
EXP_ZERO_BELOW = -104.0
QK_BOUND_MARGIN = 1.02
MOE_TILE = 256
MOE_SLOTS = 3
CMB_TILE = 256
SM_ROWS = 32
SM_WIDTH = 1024
SEC = 512
IN_SECTIONS = 6
VMEM_LIMIT = 48 * 1024 * 1024


def _cparams(sem):
    return pltpu.CompilerParams(dimension_semantics=sem, vmem_limit_bytes=VMEM_LIMIT)


def _dot(a, b):
    return jnp.dot(a, b, preferred_element_type=F32)


def _dot_nt(a, b):
    return lax.dot_general(a, b, (((1,), (1,)), ((), ())), preferred_element_type=F32)


def _split2(x):
    hi = x.astype(BF16)
    lo = (x - hi.astype(F32)).astype(BF16)
    return hi, lo


def _rms_rows(x, g):
    return x * lax.rsqrt(jnp.mean(x * x, axis=-1, keepdims=True) + EPS) * g


def _group_rsqrt(acc, e_ref, et_ref):
    hi, lo = _split2(acc * acc)
    ss = _dot(hi, e_ref[...]) + _dot(lo, e_ref[...])
    rhi, rlo = _split2(lax.rsqrt(ss * (1.0 / HEAD_DIM) + EPS))
    return _dot(rhi, et_ref[...]) + _dot(rlo, et_ref[...])


def _pack_bf16_pairs(x):
    bits = lax.bitcast_convert_type(x.astype(BF16).astype(F32), jnp.uint32)
    half = x.shape[1] // 2
    return (bits[:, half:] & jnp.uint32(0xFFFF0000)) | (bits[:, :half] >> 16)


def _unpack_bf16_pairs(packed):
    return jnp.concatenate([lax.bitcast_convert_type(packed << 16, F32),
                            lax.bitcast_convert_type(packed & jnp.uint32(0xFFFF0000), F32)], axis=1)


def _softplus(z):
    return jnp.maximum(z, 0.0) + jnp.log(1.0 + jnp.exp(-jnp.abs(z)))


def _add_expert_rows(dest_ref, x_ref, route_ref, y_hbm, xo_ref, ybuf, sem):
    i = pl.program_id(0)
    last = pl.num_programs(0) - 1
    slot = i % 2

    def start_rows(step, slot_, row_lo=0, row_hi=ROW_TILE):
        for r in range(row_lo, row_hi):
            for k in range(2):
                src = dest_ref[(step * ROW_TILE + r) * 2 + k]
                pltpu.make_async_copy(y_hbm.at[pl.ds(src, 1), :], ybuf.at[slot_, k, pl.ds(r, 1), :],
                                      sem.at[slot_]).start(priority=k)

    def wait_rows(slot_):
        for k in range(2):
            pltpu.make_async_copy(y_hbm.at[pl.ds(0, ROW_TILE), :], ybuf.at[slot_, k], sem.at[slot_]).wait()

    @pl.when(i == 0)
    def _():
        start_rows(0, 0)

    wait_rows(slot)
    route = route_ref[...]
    x = x_ref[...] + (ybuf[slot, 0] * route[:, 2:3] + ybuf[slot, 1] * route[:, 3:4])
    xo_ref[...] = x
    def issue(j):
        start_rows(jnp.minimum(i + 1, last), 1 - slot, j * ROW_TILE // IN_SECTIONS,
                   (j + 1) * ROW_TILE // IN_SECTIONS)

    def drain():
        @pl.when(i == last)
        def _():
            wait_rows(1 - slot)

    return x, issue, drain


def _even_in_kernel(*refs, fused):
    if fused:
        (dest_ref, x_ref, route_ref, y_hbm, g_ref, w_ref, gq_ref, gk_ref, e_ref, et_ref, o_ref, xo_ref,
         ybuf, sem) = refs
        x, issue, drain = _add_expert_rows(dest_ref, x_ref, route_ref, y_hbm, xo_ref, ybuf, sem)
    else:
        x_ref, g_ref, w_ref, gq_ref, gk_ref, e_ref, et_ref, o_ref = refs
        x = x_ref[...]
    xn = _rms_rows(x, g_ref[...]).astype(BF16)
    for sec in range(IN_SECTIONS):
        if fused:
            issue(sec)
        cols = slice(sec * SEC, (sec + 1) * SEC)
        acc = _dot(xn, w_ref[:, cols])
        if sec == 0:
            acc = acc * _group_rsqrt(acc, e_ref, et_ref) * gq_ref[...]
        elif sec == 1:
            acc = acc * _group_rsqrt(acc, e_ref, et_ref) * gk_ref[...]
        elif sec == 3:
            acc = acc * SCALE
        o_ref[:, cols] = acc.astype(BF16)
    if fused:
        drain()


def _odd_in_kernel(*refs, tiles_per_seq, fused):
    if fused:
        (dest_ref, x_ref, route_ref, y_hbm, g_ref, w_ref, wf_ref, bf_ref, gq_ref, gk_ref, e_ref, et_ref,
         tri_ref, o_ref, cum_ref, xo_ref, carry_ref, ybuf, sem) = refs
        x, issue, drain = _add_expert_rows(dest_ref, x_ref, route_ref, y_hbm, xo_ref, ybuf, sem)
    else:
        (x_ref, g_ref, w_ref, wf_ref, bf_ref, gq_ref, gk_ref, e_ref, et_ref, tri_ref, o_ref, cum_ref,
         carry_ref) = refs
        x = x_ref[...]
    i = pl.program_id(0)
    xn = _rms_rows(x, g_ref[...]).astype(BF16)
    for sec in range(IN_SECTIONS):
        if fused:
            issue(sec)
        cols = slice(sec * SEC, (sec + 1) * SEC)
        acc = _dot(xn, w_ref[:, cols])
        if sec < 2:
            acc = acc * _group_rsqrt(acc, e_ref, et_ref) * gq_ref[...]
        elif sec < 4:
            acc = acc * _group_rsqrt(acc, e_ref, et_ref) * gk_ref[...]
        o_ref[:, cols] = acc.astype(BF16)

    log_f = -_softplus(-(_dot(xn, wf_ref[...]) + bf_ref[...]))
    p1 = log_f.astype(BF16)
    r1 = log_f - p1.astype(F32)
    p2 = r1.astype(BF16)
    p3 = (r1 - p2.astype(F32)).astype(BF16)
    tri = tri_ref[...]
    within = _dot(tri, p1) + _dot(tri, p2) + _dot(tri, p3)

    @pl.when(i % tiles_per_seq == 0)
    def _():
        carry_ref[...] = jnp.zeros_like(carry_ref)

    cum = within + carry_ref[...]
    cum_ref[...] = cum
    carry_ref[...] = cum[-1:, :]
    if fused:
        drain()


def _group_maps():
    lane_grp = jnp.arange(SEC)[:, None] // HEAD_DIM
    e = (lane_grp == jnp.arange(LANES)[None, :]).astype(BF16)
    return e, e.T


def _in_proj_call(kernel_fn, name, x, moe, in_arrays, in_shapes, out_specs_fn, out_shapes, scratch):
    n, d = x.shape
    rows = lambda width: pl.BlockSpec((ROW_TILE, width), lambda i, *_: (i, 0))
    const = lambda shape: pl.BlockSpec(shape, lambda i, *_: (0,) * len(shape))
    in_specs = [rows(d)] + [const(s) for s in in_shapes]
    out_specs = out_specs_fn(rows)
    args = [x] + list(in_arrays)
    prefetch = []
    if moe is not None:
        y_buf, dest, route = moe
        in_specs = [rows(d), rows(LANES), pl.BlockSpec(memory_space=pl.ANY)] + in_specs[1:]
        args = [x, route, y_buf] + list(in_arrays)
        prefetch = [dest]
        out_specs = out_specs + [rows(d)]
        out_shapes = out_shapes + [jax.ShapeDtypeStruct((n, d), F32)]
        scratch = scratch + [pltpu.VMEM((2, 2, ROW_TILE, d), F32), pltpu.SemaphoreType.DMA((2,))]
    return pl.pallas_call(
        functools.partial(kernel_fn, fused=moe is not None),
        grid_spec=pltpu.PrefetchScalarGridSpec(
            num_scalar_prefetch=len(prefetch), grid=(n // ROW_TILE,), in_specs=in_specs,
            out_specs=out_specs, scratch_shapes=scratch),
        out_shape=out_shapes,
        compiler_params=_cparams(("arbitrary",)),
        name=name,
    )(*prefetch, *args)


def _even_in_proj(x, g, w_in, q_gain, k_gain, moe=None):
    n, d = x.shape
    e, et = _group_maps()
    reps = SEC // HEAD_DIM
    gq = jnp.tile(q_gain.astype(F32) * (SCALE * LOG2E), reps).reshape(1, SEC)
    gk = jnp.tile(k_gain.astype(F32), reps).reshape(1, SEC)
    return _in_proj_call(
        _even_in_kernel, "even_in_proj", x, moe,
        [g.reshape(1, d), w_in.astype(BF16), gq, gk, e, et],
        [(1, d), (d, 3 * d), (1, SEC), (1, SEC), (SEC, LANES), (LANES, SEC)],
        lambda rows: [rows(3 * d)], [jax.ShapeDtypeStruct((n, 3 * d), BF16)], [])


def _odd_in_proj(x, g, w_in, b_f, q_gain, k_gain, seq, moe=None):
    n, d = x.shape
    heads = d // HEAD_DIM
    e, et = _group_maps()
    reps = SEC // HEAD_DIM
    gq = jnp.tile(q_gain.astype(F32) * (SCALE * LOG2E), reps).reshape(1, SEC)
    gk = jnp.tile(k_gain.astype(F32), reps).reshape(1, SEC)
    wf = jnp.zeros((d, LANES), F32).at[:, :heads].set(w_in[:, 3 * d:]).astype(BF16)
    bf = jnp.zeros((1, LANES), F32).at[0, :heads].set(b_f.astype(F32))
    tri = (jnp.arange(ROW_TILE)[:, None] >= jnp.arange(ROW_TILE)[None, :]).astype(BF16)
    return _in_proj_call(
        functools.partial(_odd_in_kernel, tiles_per_seq=seq // ROW_TILE), "odd_in_proj", x, moe,
        [g.reshape(1, d), w_in[:, :3 * d].astype(BF16), wf, bf, gq, gk, e, et, tri],
        [(1, d), (d, 3 * d), (d, LANES), (1, LANES), (1, SEC), (1, SEC), (SEC, LANES), (LANES, SEC),
         (ROW_TILE, ROW_TILE)],
        lambda rows: [rows(3 * d), rows(LANES)],
        [jax.ShapeDtypeStruct((n, 3 * d), BF16), jax.ShapeDtypeStruct((n, LANES), F32)],
        [pltpu.VMEM((1, LANES), F32)])


def _flash_parts(q_parts, row0s, k_ref, vaug_ref, bias_fn, qi, kstart, tq, tile, m_refs, acc_refs):
    n_parts = len(q_parts)
    for p in range(n_parts):
        m_refs[p][...] = jnp.full(m_refs[p].shape, NEG, F32)
        acc_refs[p][...] = jnp.zeros(acc_refs[p].shape, F32)

    def softmax_rows(s, brow, m_all, diag_off):
        rows_p, width = s.shape
        chunk = SM_ROWS if width <= SM_WIDTH else SM_ROWS // 2
        if diag_off is not None:
            rel = (lax.broadcasted_iota(jnp.int32, (chunk, width), 0)
                   - lax.broadcasted_iota(jnp.int32, (chunk, width), 1))
        m_news, chunks = [], []
        for c in range(rows_p // chunk):
            rows = slice(c * chunk, (c + 1) * chunk)
            sc = s[rows] + brow
            if diag_off is not None:
                sc = jnp.where(rel >= -(diag_off + c * chunk), sc, NEG)
            blocks = [sc[:, j * LANES:(j + 1) * LANES] for j in range(width // LANES)]
            top = functools.reduce(jnp.maximum, blocks)
            m_new = jnp.maximum(m_all[rows], jnp.max(top, axis=-1, keepdims=True))
            m_news.append(m_new)
            chunks.append(jnp.concatenate([jnp.exp2((b - m_new).astype(BF16)) for b in blocks], axis=1))
        return jnp.concatenate(m_news, axis=0), jnp.concatenate(chunks, axis=0)

    def step(key_start, width, modes):
        key_start = pl.multiple_of(key_start, tile)
        k = k_ref[pl.ds(key_start, width), :]
        va = vaug_ref[pl.ds(key_start, width), :]
        brow = bias_fn(key_start, width)
        live = [p for p in range(n_parts) if modes[p] != "skip"]
        s = {p: _dot_nt(q_parts[p], k) for p in live}
        m_old = {p: m_refs[p][...] for p in live}
        acc_old = {p: acc_refs[p][...] for p in live}
        m_new, acc_new = {}, {}
        for p in live:
            m_new[p], w = softmax_rows(s[p], brow, m_old[p], modes[p])
            alpha = jnp.exp2(m_old[p] - m_new[p])
            pv = _dot(w, va)
            acc_new[p] = jnp.concatenate([alpha * acc_old[p][:, :LANES] + pv[:, :LANES],
                                          alpha * acc_old[p][:, LANES:] + pv[:, LANES:]], axis=1)
        for p in live:
            m_refs[p][...] = m_new[p]
            acc_refs[p][...] = acc_new[p]

    plain = (None,) * n_parts
    first_own = qi * (tq // tile)
    n_plain = first_own - kstart
    n_quads = n_plain // 4
    rest = n_plain - 4 * n_quads

    def quad_body(j, c):
        step((kstart + 4 * j) * tile, 4 * tile, plain)
        return c

    lax.fori_loop(0, n_quads, quad_body, 0)

    @pl.when(rest >= 2)
    def _():
        step((kstart + 4 * n_quads) * tile, 2 * tile, plain)

    @pl.when(rest % 2 == 1)
    def _():
        step((first_own - 1) * tile, tile, plain)

    for j in range(tq // tile):
        modes = []
        for p in range(n_parts):
            rows_p = q_parts[p].shape[0]
            if row0s[p] + rows_p - 1 < j * tile:
                modes.append("skip")
            elif row0s[p] >= (j + 1) * tile - 1:
                modes.append(None)
            else:
                modes.append(row0s[p] - j * tile)
        step(qi * tq + j * tile, tile, tuple(modes))


def _fill_vaug(qi, v_ref, vaug_ref):
    @pl.when(qi == 0)
    def _():
        vaug_ref[:, :LANES] = v_ref[...]
        vaug_ref[:, LANES:] = jnp.ones(v_ref.shape, BF16)


def _first_live_tile(first_own, decay_to_tile, cutoff):
    def cond(kt):
        return (kt < first_own) & (decay_to_tile(jnp.minimum(kt, first_own - 1)) > cutoff)
    return lax.while_loop(cond, lambda kt: kt + 1, jnp.int32(0))


def _flash_scratch(seq, rows):
    return [pltpu.VMEM((seq, 2 * LANES), BF16), pltpu.VMEM((rows, LANES), F32), pltpu.VMEM((rows, LANES), F32),
            pltpu.VMEM((rows, 2 * LANES), F32), pltpu.VMEM((rows, 2 * LANES), F32)]


def _fox_kernel(ftab_ref, cut_ref, q_ref, k_ref, v_ref, f_ref, o_ref, vaug_ref, m0_ref, m1_ref, acc0_ref,
                acc1_ref, *, tq, tile, nk):
    b = pl.program_id(0)
    hp = pl.program_id(1)
    qi = pl.program_id(2)
    qstart = pl.multiple_of(qi * tq, tq)
    first_own = qi * (tq // tile)
    half = tq // 2
    _fill_vaug(qi, v_ref, vaug_ref)
    q = q_ref[...]
    lane = lax.broadcasted_iota(jnp.int32, q.shape, 1)
    cutoff = cut_ref[0]

    def one_head(hh, carry):
        in_head = (lane >= hh * HEAD_DIM) & (lane < (hh + 1) * HEAD_DIM)
        qh = jnp.where(in_head, q, jnp.zeros_like(q))
        lane_half = lax.broadcasted_iota(jnp.int32, (half, LANES), 1)
        in_head_half = (lane_half >= hh * HEAD_DIM) & (lane_half < (hh + 1) * HEAD_DIM)

        def f_row(start, width):
            first = jnp.full((1, width), hh, jnp.int32) == 0
            return jnp.where(first, f_ref[0:1, pl.ds(start, width)], f_ref[1:2, pl.ds(start, width)])

        f_q0 = f_row(qstart, LANES)[:, 0:1]

        def bias(start, width):
            return LOG2E * (f_q0 - f_row(start, width))

        base = ((b * pl.num_programs(1) + hp) * 2 + hh) * nk

        def decay(kt):
            return (ftab_ref[base + kt + 1] - ftab_ref[base + first_own]) * LOG2E

        kstart = _first_live_tile(first_own, decay, cutoff)
        _flash_parts((qh[:half], qh[half:]), (0, half), k_ref, vaug_ref, bias, qi, kstart, tq, tile,
                     (m0_ref, m1_ref), (acc0_ref, acc1_ref))
        for p, acc_ref in enumerate((acc0_ref, acc1_ref)):
            rows = slice(p * half, (p + 1) * half)
            acc = acc_ref[...]
            o_h = acc[:, :LANES] / acc[:, LANES:]
            o_ref[rows, :] = jnp.where(in_head_half, o_h, o_ref[rows, :].astype(F32)).astype(BF16)
        return carry

    o_ref[...] = jnp.zeros_like(o_ref)
    lax.fori_loop(0, 2, one_head, 0)


def _zero_cutoff(q_gain, k_gain):
    bound = (HEAD_DIM * SCALE * LOG2E * QK_BOUND_MARGIN) * jnp.max(jnp.abs(q_gain)) * jnp.max(jnp.abs(k_gain))
    return (EXP2_UNDERFLOW + 2.0 * bound).astype(F32).reshape(1)


def _fox_attention(qkv, cum_t, cutoff, batch, seq):
    n, d3 = qkv.shape
    d = d3 // 3
    pairs = d // LANES
    tq, tile = FOX_Q_TILE, ATT_TILE
    nq = seq // tq
    ftab = cum_t[:, :, :, ::tile].reshape(-1)
    return pl.pallas_call(
        functools.partial(_fox_kernel, tq=tq, tile=tile, nk=seq // tile),
        grid_spec=pltpu.PrefetchScalarGridSpec(
            num_scalar_prefetch=2,
            grid=(batch, pairs, nq),
            in_specs=[pl.BlockSpec((tq, LANES), lambda b, h, i, ft, ct: (b * nq + i, h)),
                      pl.BlockSpec((seq, LANES), lambda b, h, i, ft, ct: (b, pairs + h)),
                      pl.BlockSpec((seq, LANES), lambda b, h, i, ft, ct: (b, 2 * pairs + h)),
                      pl.BlockSpec((None, None, 2, seq), lambda b, h, i, ft, ct: (b, h, 0, 0))],
            out_specs=pl.BlockSpec((tq, LANES), lambda b, h, i, ft, ct: (b * nq + i, h)),
            scratch_shapes=_flash_scratch(seq, tq // 2)),
        out_shape=jax.ShapeDtypeStruct((n, d), BF16),
        compiler_params=_cparams(("arbitrary", "arbitrary", "arbitrary")),
        name="fox_attention",
    )(ftab, cutoff, qkv, qkv, qkv, cum_t)


def _diff_kernel(slope_ref, cut_ref, q_ref, k_ref, v_ref, lam_ref, sub_ref, o_ref, vaug_ref, m0_ref, m1_ref,
                 acc0_ref, acc1_ref, *, tile, lam_init):
    h = pl.program_id(1)
    qi = pl.program_id(2)
    slope = slope_ref[h] * LOG2E
    qstart = qi * tile

    def bias(start, width):
        key_off = lax.broadcasted_iota(jnp.int32, (1, width), 1)
        return slope * (start - qstart + key_off).astype(F32)

    def decay(kt):
        return slope * (qstart - (kt + 1) * tile + 1).astype(F32)

    kstart = _first_live_tile(qi, decay, cut_ref[0])
    _fill_vaug(qi, v_ref, vaug_ref)
    q = q_ref[...]
    lane = lax.broadcasted_iota(jnp.int32, q.shape, 1)
    zero = jnp.zeros_like(q)
    q_parts = (jnp.where(lane < HEAD_DIM, q, zero), jnp.where(lane >= HEAD_DIM, q, zero))
    _flash_parts(q_parts, (0, 0), k_ref, vaug_ref, bias, qi, kstart, tile, tile, (m0_ref, m1_ref),
                 (acc0_ref, acc1_ref))
    acc0, acc1 = acc0_ref[...], acc1_ref[...]
    lp = lam_ref[...]
    lam = (jnp.exp(jnp.sum(lp[0:1] * lp[1:2], axis=-1, keepdims=True))
           - jnp.exp(jnp.sum(lp[2:3] * lp[3:4], axis=-1, keepdims=True)) + lam_init)
    o = acc0[:, :LANES] / acc0[:, LANES:] - lam * (acc1[:, :LANES] / acc1[:, LANES:])
    o_ref[...] = (_rms_rows(o, sub_ref[...]) * (1.0 - lam_init)).astype(BF16)


def _diff_attention(proj, lam_params, subln_g, cutoff, batch, seq, layer):
    n = proj.shape[0]
    tile = ATT_TILE
    nq = seq // tile
    lam_init = 0.8 - 0.6 * math.exp(-0.3 * layer)
    slopes = jnp.asarray([2.0 ** (-8.0 * (i + 1) / A_HEADS) for i in range(A_HEADS)], F32)
    imap = lambda f: (lambda b, h, i, s, c: f(b, h, i))
    return pl.pallas_call(
        functools.partial(_diff_kernel, tile=tile, lam_init=lam_init),
        grid_spec=pltpu.PrefetchScalarGridSpec(
            num_scalar_prefetch=2,
            grid=(batch, A_HEADS, nq),
            in_specs=[pl.BlockSpec((tile, LANES), imap(lambda b, h, i: (b * nq + i, h))),
                      pl.BlockSpec((seq, LANES), imap(lambda b, h, i: (b, A_HEADS + h))),
                      pl.BlockSpec((seq, LANES), imap(lambda b, h, i: (b, 2 * A_HEADS + h))),
                      pl.BlockSpec((4, HEAD_DIM), imap(lambda b, h, i: (0, 0))),
                      pl.BlockSpec((1, LANES), imap(lambda b, h, i: (0, 0)))],
            out_specs=pl.BlockSpec((tile, LANES), imap(lambda b, h, i: (b * nq + i, h))),
            scratch_shapes=_flash_scratch(seq, tile)),
        out_shape=jax.ShapeDtypeStruct((n, A_HEADS * LANES), BF16),
        compiler_params=_cparams(("arbitrary", "arbitrary", "arbitrary")),
        name="diff_attention",
    )(slopes, cutoff, proj, proj, proj, lam_params.astype(F32), subln_g.astype(F32).reshape(1, LANES))


def _stick_kernel(q_ref, k_ref, v_ref, ntri_ref, o_ref, *, tq, tk):
    qi = pl.program_id(2)
    ratio = tq // tk
    q = q_ref[...]
    lane = lax.broadcasted_iota(jnp.int32, q.shape, 1)
    zero = jnp.zeros_like(q)
    qs = (jnp.where(lane < HEAD_DIM, q, zero), jnp.where(lane >= HEAD_DIM, q, zero))
    rel = (lax.broadcasted_iota(jnp.int32, (tq, tk), 0)
           - lax.broadcasted_iota(jnp.int32, (tq, tk), 1))
    ntri = ntri_ref[...]

    def sweep(tiles, carry):
        loaded = []
        for kt, _ in tiles:
            start = pl.multiple_of(kt * tk, tk)
            loaded.append((k_ref[pl.ds(start, tk), :], v_ref[pl.ds(start, tk), :]))
        new = []
        for hh in range(2):
            run, acc = carry[hh]
            for (kt, offset), (k, v) in zip(tiles, loaded):
                z = _dot_nt(qs[hh], k)
                sp = _softplus(z)
                if offset is None:
                    drop = sp
                else:
                    before = rel > offset
                    drop = jnp.where(before, sp, 0.0)
                hi, lo = _split2(drop)
                later = _dot(hi, ntri) + _dot(lo, ntri)
                w = jnp.exp((z - sp) + later + run)
                if offset is not None:
                    w = jnp.where(before, w, 0.0)
                acc = acc + _dot(w.astype(BF16), v)
                run = run - jnp.sum(drop, axis=-1, keepdims=True)
            new.append((run, acc))
        return tuple(new)

    init_one = (jnp.zeros((tq, 1), F32), jnp.zeros((tq, LANES), F32))
    own = [(qi * ratio + d, (qi * ratio + d) * tk - qi * tq) for d in range(ratio - 1, -1, -1)]
    prev = jnp.maximum(qi * ratio - 1, 0)
    prev_offset = jnp.where(qi > 0, -tk, tq)
    carry = sweep(own + [(prev, prev_offset)], (init_one, init_one))

    def live(c):
        top = jnp.maximum(jnp.max(c[0][0]), jnp.max(c[1][0]))
        return (top > EXP_ZERO_BELOW).astype(jnp.int32)

    def cond(state):
        kt, alive, _ = state
        return (kt >= 0) & (alive > 0)

    def body(state):
        kt, _, c = state
        c = sweep([(kt, None)], c)
        return kt - 1, live(c), c

    _, _, carry = lax.while_loop(cond, body, (qi * ratio - 2, live(carry), carry))
    o_ref[...] = jnp.where(lane < HEAD_DIM, carry[0][1], carry[1][1]).astype(BF16)


def _stick_attention(proj, batch, seq, col0):
    n = proj.shape[0]
    pairs = 4
    tq, tk = SB_Q_TILE, SB_K_TILE
    nq = seq // tq
    ntri = -(jnp.arange(tk)[:, None] > jnp.arange(tk)[None, :]).astype(BF16)
    return pl.pallas_call(
        functools.partial(_stick_kernel, tq=tq, tk=tk),
        grid=(batch, pairs, nq),
        in_specs=[pl.BlockSpec((tq, LANES), lambda b, h, i: (b * nq + i, col0 + h)),
                  pl.BlockSpec((seq, LANES), lambda b, h, i: (b, col0 + pairs + h)),
                  pl.BlockSpec((seq, LANES), lambda b, h, i: (b, col0 + 2 * pairs + h)),
                  pl.BlockSpec((tk, tk), lambda b, h, i: (0, 0))],
        out_specs=pl.BlockSpec((tq, LANES), lambda b, h, i: (b * nq + i, h)),
        out_shape=jax.ShapeDtypeStruct((n, pairs * LANES), BF16),
        compiler_params=_cparams(("arbitrary", "arbitrary", "arbitrary")),
        name="stick_attention",
    )(proj, proj, proj, ntri)


def _out_router_kernel(*refs, n_parts):
    x_ref = refs[0]
    o_refs = refs[1:1 + n_parts]
    w_refs = refs[1 + n_parts:1 + 2 * n_parts]
    (g_ref, wr_hi_ref, wr_lo_ref, br_ref, ltri_ref, xo_ref, h_ref, route_ref, count_ref,
     seen_ref) = refs[1 + 2 * n_parts:]
    mix = _dot(o_refs[0][...], w_refs[0][...])
    for p in range(1, n_parts):
        mix = mix + _dot(o_refs[p][...], w_refs[p][...])
    x = x_ref[...] + mix
    xo_ref[...] = x
    h = _rms_rows(x, g_ref[...])
    h_ref[...] = _pack_bf16_pairs(h)

    h_hi, h_lo = _split2(h)
    logits = (_dot(h_hi, wr_hi_ref[...]) + _dot(h_hi, wr_lo_ref[...]) + _dot(h_lo, wr_hi_ref[...])
              + br_ref[...])
    lane = lax.broadcasted_iota(jnp.int32, logits.shape, 1)

    def first_max(vals):
        top = jnp.max(vals, axis=-1, keepdims=True)
        idx = jnp.min(jnp.where(vals == top, lane, LANES), axis=-1, keepdims=True)
        return top, idx

    g_logits = jnp.where(lane < N_GROUPS, logits, NEG)
    g_top, g_idx = first_max(g_logits)
    g_w = 1.0 / jnp.sum(jnp.exp(g_logits - g_top), axis=-1, keepdims=True)
    lo_lane = N_GROUPS + EXPERTS_PER_GROUP * g_idx
    in_group = (lane >= lo_lane) & (lane < lo_lane + EXPERTS_PER_GROUP)
    e_logits = jnp.where(in_group, logits, NEG)
    v1, i1 = first_max(e_logits)
    v2, i2 = first_max(jnp.where(lane == i1, NEG, e_logits))
    e2 = jnp.exp(v2 - v1)
    w1 = g_w / (1.0 + e2)
    w2 = g_w * e2 / (1.0 + e2)

    @pl.when(pl.program_id(0) == 0)
    def _():
        seen_ref[...] = jnp.zeros_like(seen_ref)

    uses = ((lane == i1) | (lane == i2)).astype(BF16)
    earlier = _dot(ltri_ref[...], uses) + seen_ref[...]
    r1 = jnp.sum(jnp.where(lane == i1, earlier, 0.0), axis=-1, keepdims=True)
    r2 = jnp.sum(jnp.where(lane == i2, earlier, 0.0), axis=-1, keepdims=True)
    seen = earlier[-1:, :] + uses[-1:, :].astype(F32)
    seen_ref[...] = seen
    count_ref[...] = jnp.broadcast_to(seen, count_ref.shape)

    cols = ((i1 - N_GROUPS).astype(F32), (i2 - N_GROUPS).astype(F32), w1, w2, r1, r2)
    route = jnp.zeros(logits.shape, F32)
    for c, val in enumerate(cols):
        route = jnp.where(lane == c, val, route)
    route_ref[...] = route


def _out_router(x, o_parts, w_out, g_ffn, w_group, b_group, w_router, b_router):
    n, d = x.shape
    n_parts = len(o_parts)
    w_bf = w_out.astype(BF16)
    w_parts, r0 = [], 0
    for o in o_parts:
        w_parts.append(w_bf[r0:r0 + o.shape[1]])
        r0 += o.shape[1]
    wr = jnp.zeros((d, LANES), F32).at[:, :N_GROUPS].set(w_group)
    wr = wr.at[:, N_GROUPS:N_GROUPS + N_EXPERTS].set(w_router)
    wr_hi = wr.astype(BF16)
    wr_lo = (wr - wr_hi.astype(F32)).astype(BF16)
    br = jnp.zeros((1, LANES), F32).at[0, :N_GROUPS].set(b_group.astype(F32))
    br = br.at[0, N_GROUPS:N_GROUPS + N_EXPERTS].set(b_router.astype(F32))
    ltri = (jnp.arange(ROW_TILE)[:, None] > jnp.arange(ROW_TILE)[None, :]).astype(BF16)
    rows = lambda width: pl.BlockSpec((ROW_TILE, width), lambda i: (i, 0))
    const = lambda shape: pl.BlockSpec(shape, lambda i: (0, 0))
    return pl.pallas_call(
        functools.partial(_out_router_kernel, n_parts=n_parts),
        grid=(n // ROW_TILE,),
        in_specs=([rows(d)] + [rows(o.shape[1]) for o in o_parts]
                  + [const(w.shape) for w in w_parts]
                  + [const((1, d)), const((d, LANES)), const((d, LANES)), const((1, LANES)),
                     const((ROW_TILE, ROW_TILE))]),
        out_specs=[rows(d), rows(d // 2), rows(LANES), const((8, LANES))],
        out_shape=[jax.ShapeDtypeStruct((n, d), F32), jax.ShapeDtypeStruct((n, d // 2), jnp.uint32),
                   jax.ShapeDtypeStruct((n, LANES), F32), jax.ShapeDtypeStruct((8, LANES), F32)],
        scratch_shapes=[pltpu.VMEM((1, LANES), F32)],
        compiler_params=_cparams(("arbitrary",)),
        name="out_router",
    )(x, *o_parts, *w_parts, g_ffn.reshape(1, d), wr_hi, wr_lo, br, ltri)


def _dispatch_plan(route, counts_row, n_blocks):
    n = route.shape[0]
    e_flat = route[:, 0:2].astype(jnp.int32).reshape(-1)
    rank = route[:, 4:6].astype(jnp.int32).reshape(-1)
    counts = counts_row[N_GROUPS:N_GROUPS + N_EXPERTS].astype(jnp.int32)
    padded = (counts + MOE_TILE - 1) // MOE_TILE * MOE_TILE
    pad_ends = jnp.cumsum(padded)
    dest = ((pad_ends - padded)[e_flat] + rank).astype(jnp.int32)
    p = n_blocks * MOE_TILE
    tok_buf = (jnp.arange(p, dtype=jnp.int32) % n).at[dest].set(
        jnp.arange(2 * n, dtype=jnp.int32) // 2, unique_indices=True, mode='promise_in_bounds')
    blk_start = jnp.arange(n_blocks, dtype=jnp.int32) * MOE_TILE
    blk_e = jnp.sum((pad_ends[None, :] <= blk_start[:, None]).astype(jnp.int32), axis=1)
    blk_e = jnp.minimum(blk_e, N_EXPERTS - 1).astype(jnp.int32)
    n_used = (pad_ends[-1:] // MOE_TILE).astype(jnp.int32)
    return dest, tok_buf, blk_e, n_used


def _moe_kernel(blk_e_ref, tok_ref, nused_ref, h_hbm, wg_ref, wu_ref, wd_ref, y_ref, xbuf, wg_bf, wu_bf, wd_bf, sem):
    i = pl.program_id(0)
    last = pl.num_programs(0) - 1
    slot = i % MOE_SLOTS

    def start_rows(blk, slot_, row_lo=0, row_hi=MOE_TILE):
        for r in range(row_lo, row_hi):
            tok = tok_ref[blk * MOE_TILE + r]
            pltpu.make_async_copy(h_hbm.at[pl.ds(tok, 1), :], xbuf.at[slot_, pl.ds(r, 1), :],
                                  sem.at[slot_]).start(priority=r % 2)

    def wait_rows(slot_):
        pltpu.make_async_copy(h_hbm.at[pl.ds(0, MOE_TILE), :], xbuf.at[slot_], sem.at[slot_]).wait()

    @pl.when(i == 0)
    def _():
        for ahead in range(MOE_SLOTS - 1):
            start_rows(jnp.minimum(ahead, last), ahead)

    @pl.when((i == 0) | (blk_e_ref[i] != blk_e_ref[jnp.maximum(i - 1, 0)]))
    def _():
        wg_bf[...] = wg_ref[...].astype(BF16)
        wu_bf[...] = wu_ref[...].astype(BF16)
        wd_bf[...] = wd_ref[...].astype(BF16)

    n_used = nused_ref[0]

    @pl.when(i < n_used)
    def _():
        wait_rows(slot)
        xb = _unpack_bf16_pairs(xbuf[slot]).astype(BF16)
        nxt, nxt_slot = jnp.minimum(i + MOE_SLOTS - 1, last), (i + MOE_SLOTS - 1) % MOE_SLOTS
        cut1, cut2 = MOE_TILE // 3, 2 * MOE_TILE // 3
        start_rows(nxt, nxt_slot, 0, cut1)
        gate = _dot(xb, wg_bf[...])
        start_rows(nxt, nxt_slot, cut1, cut2)
        up = _dot(xb, wu_bf[...])
        act = (gate * jax.nn.sigmoid(gate) * up).astype(BF16)
        start_rows(nxt, nxt_slot, cut2, MOE_TILE)
        y_ref[...] = _dot(act, wd_bf[...])

        @pl.when(i == n_used - 1)
        def _():
            for ahead in range(1, MOE_SLOTS):
                wait_rows((i + ahead) % MOE_SLOTS)

    @pl.when(i >= n_used)
    def _():
        y_ref[...] = jnp.zeros_like(y_ref)


def _moe_experts(h, tok_buf, blk_e, n_used, w_gate, w_up, w_down, layer, n_blocks):
    n, d = h.shape[0], 2 * h.shape[1]
    f = w_gate.shape[-1]
    p = n_blocks * MOE_TILE
    return pl.pallas_call(
        _moe_kernel,
        grid_spec=pltpu.PrefetchScalarGridSpec(
            num_scalar_prefetch=3,
            grid=(n_blocks,),
            in_specs=[pl.BlockSpec(memory_space=pl.ANY),
                      pl.BlockSpec((None, None, d, f), lambda i, be, tk, nu: (layer, be[i], 0, 0)),
                      pl.BlockSpec((None, None, d, f), lambda i, be, tk, nu: (layer, be[i], 0, 0)),
                      pl.BlockSpec((None, None, f, d), lambda i, be, tk, nu: (layer, be[i], 0, 0))],
            out_specs=pl.BlockSpec((MOE_TILE, d), lambda i, be, tk, nu: (i, 0)),
            scratch_shapes=[pltpu.VMEM((MOE_SLOTS, MOE_TILE, d // 2), jnp.uint32), pltpu.VMEM((d, f), BF16),
                            pltpu.VMEM((d, f), BF16), pltpu.VMEM((f, d), BF16),
                            pltpu.SemaphoreType.DMA((MOE_SLOTS,))]),
        out_shape=jax.ShapeDtypeStruct((p, d), F32),
        compiler_params=_cparams(("arbitrary",)),
        name="moe_experts",
    )(blk_e, tok_buf, n_used, h, w_gate, w_up, w_down)


def _combine_kernel(dest_ref, x_ref, route_ref, y_hbm, o_ref, ybuf, sem):
    i = pl.program_id(0)
    nsteps = pl.num_programs(0)
    slot = i % 2

    def start_gather(step, slot_):
        def body(r, c):
            for k in range(2):
                src = dest_ref[(step * CMB_TILE + r) * 2 + k]
                pltpu.make_async_copy(y_hbm.at[pl.ds(src, 1), :], ybuf.at[slot_, k, pl.ds(r, 1), :],
                                      sem.at[slot_]).start(priority=k)
            return c
        lax.fori_loop(0, CMB_TILE, body, 0, unroll=4)

    @pl.when(i == 0)
    def _():
        start_gather(0, 0)

    @pl.when(i + 1 < nsteps)
    def _():
        start_gather(i + 1, 1 - slot)

    for k in range(2):
        pltpu.make_async_copy(y_hbm.at[pl.ds(0, CMB_TILE), :], ybuf.at[slot, k], sem.at[slot]).wait()
    route = route_ref[...]
    o_ref[...] = x_ref[...] + (ybuf[slot, 0] * route[:, 2:3] + ybuf[slot, 1] * route[:, 3:4])


def _combine(x, y_buf, dest, route):
    n, d = x.shape
    return pl.pallas_call(
        _combine_kernel,
        grid_spec=pltpu.PrefetchScalarGridSpec(
            num_scalar_prefetch=1,
            grid=(n // CMB_TILE,),
            in_specs=[pl.BlockSpec((CMB_TILE, d), lambda i, ds: (i, 0)),
                      pl.BlockSpec((CMB_TILE, LANES), lambda i, ds: (i, 0)),
                      pl.BlockSpec(memory_space=pl.ANY)],
            out_specs=pl.BlockSpec((CMB_TILE, d), lambda i, ds: (i, 0)),
            scratch_shapes=[pltpu.VMEM((2, 2, CMB_TILE, d), F32), pltpu.SemaphoreType.DMA((2,))]),
        out_shape=jax.ShapeDtypeStruct((n, d), F32),
        compiler_params=_cparams(("arbitrary",)),
        name="moe_combine",
    )(dest, x, route, y_buf)


def kernel(x, norm_mix_g, norm_ffn_g, ab_w_in, a_q_gain, a_k_gain, a_lam_q1, a_lam_k1, a_lam_q2,
           a_lam_k2, a_subln_g, ab_w_out, c_w_in, c_b_f, c_q_gain, c_k_gain, c_w_out, moe_w_group,
           moe_b_group, moe_w_router, moe_b_router, moe_w_gate, moe_w_up, moe_w_down):
    batch, seq, d = x.shape
    n = batch * seq
    depth = norm_mix_g.shape[0]
    heads = d // HEAD_DIM
    n_blocks = (2 * n) // MOE_TILE + N_EXPERTS
    xs = x.reshape(n, d).astype(F32)
    moe = None
    for layer in range(depth):
        i = layer // 2
        if layer % 2 == 0:
            outs = _even_in_proj(xs, norm_mix_g[layer], ab_w_in[i], a_q_gain[i], a_k_gain[i], moe)
            proj = outs[0]
            xs = outs[-1] if moe is not None else xs
            lam_params = jnp.stack([a_lam_q1[i], a_lam_k1[i], a_lam_q2[i], a_lam_k2[i]])
            o_a = _diff_attention(proj, lam_params, a_subln_g[i], _zero_cutoff(a_q_gain[i], a_k_gain[i]),
                                  batch, seq, layer)
            o_b = _stick_attention(proj, batch, seq, 3 * A_HEADS)
            o_parts, w_out = [o_a, o_b], ab_w_out[i]
        else:
            outs = _odd_in_proj(xs, norm_mix_g[layer], c_w_in[i], c_b_f[i], c_q_gain[i], c_k_gain[i], seq,
                                moe)
            qkv, cum = outs[0], outs[1]
            xs = outs[-1] if moe is not None else xs
            cum_t = cum[:, :heads].reshape(batch, seq, heads // 2, 2).transpose(0, 2, 3, 1)
            cutoff = _zero_cutoff(c_q_gain[i], c_k_gain[i])
            o_parts, w_out = [_fox_attention(qkv, cum_t, cutoff, batch, seq)], c_w_out[i]
        xs, h, route, counts = _out_router(xs, o_parts, w_out, norm_ffn_g[layer], moe_w_group[layer],
                                           moe_b_group[layer], moe_w_router[layer], moe_b_router[layer])
        dest, tok_buf, blk_e, n_used = _dispatch_plan(route, counts[0], n_blocks)
        y_buf = _moe_experts(h, tok_buf, blk_e, n_used, moe_w_gate, moe_w_up, moe_w_down, layer, n_blocks)
        moe = (y_buf, dest, route)
    xs = _combine(xs, *moe)
    return xs.reshape(batch, seq, d).astype(x.dtype)
```
